```python
import math
import jax
import jax.numpy as jnp
from jax import lax
import numpy as np

D_MODEL = 1024
BATCH = 8
SEQ = 2048
DEPTH = 4
DEC_BATCH = 128
DEC_SEQ = 4
PAST_LEN = 8192
PAGE_SIZE = 128

N_EVEN = (DEPTH + 1) // 2
N_ODD = DEPTH // 2
HEAD_DIM = 64
A_HEADS = 8
A_KV_HEADS = 2
A_GROUP = A_HEADS // A_KV_HEADS
A_Q_W = A_HEADS * HEAD_DIM
A_KV_W = A_KV_HEADS * HEAD_DIM
WINDOW = 128
N_BUCKETS = 32
MAX_DISTANCE = 128
SG_GROUPS = 4
SG_GROUP_DIM = 128
SG_WIDTH = SG_GROUPS * SG_GROUP_DIM
CHUNK = 128
IN_EVEN = A_Q_W + 2 * A_KV_W + 2 * SG_WIDTH
MIX_EVEN = A_Q_W + SG_WIDTH
C_WIDTH = 512
CONV_WIDTH = 3
MLA_HEADS = 8
NOPE_DIM = 64
ROPE_DIM = 32
V_DIM = 64
Q_LORA = 768
KV_LORA = 256
ROPE_THETA = 10000.0
MLA_BLOCK = 128
IN_ODD = 3 * C_WIDTH + Q_LORA + KV_LORA + ROPE_DIM
MIX_ODD = C_WIDTH + MLA_HEADS * V_DIM
D_FF = 2816
EPS = 1e-6
NEG = -1e30

kernel_name = 'hybrid_swa_sgu_conv_mla_decoder_step'


def rms_norm(x, g):
    xf = x.astype(jnp.float32)
    y = xf * lax.rsqrt(jnp.mean(xf * xf, axis=-1, keepdims=True) + EPS)
    return (y * g.astype(jnp.float32)).astype(x.dtype)


def layer_norm(x, g, b):
    xf = x.astype(jnp.float32)
    xc = xf - jnp.mean(xf, axis=-1, keepdims=True)
    y = xc * lax.rsqrt(jnp.mean(xc * xc, axis=-1, keepdims=True) + EPS)
    return (y * g.astype(jnp.float32) + b.astype(jnp.float32)).astype(x.dtype)


def swiglu(x, w_gate, w_up, w_down):
    return (jax.nn.silu(x @ w_gate) * (x @ w_up)) @ w_down


def t5_bucket(dist):
    n = jnp.maximum(dist, 0)
    max_exact = N_BUCKETS // 2
    nf = jnp.maximum(n, 1).astype(jnp.float32)
    large = max_exact + (jnp.log(nf / max_exact) / math.log(MAX_DISTANCE / max_exact)
                         * (N_BUCKETS - max_exact)).astype(jnp.int32)
    return jnp.where(n < max_exact, n, jnp.minimum(large, N_BUCKETS - 1))


def t5_bias(dist, rel_bias):
    b = rel_bias[t5_bucket(dist)].astype(jnp.float32)
    b = jnp.moveaxis(b, -1, 0)
    return b.reshape((A_KV_HEADS, A_GROUP) + dist.shape)


def sink_softmax(s, sinks):
    sk = sinks.astype(jnp.float32).reshape(A_KV_HEADS, A_GROUP, 1, 1)
    m = jnp.maximum(jnp.max(s, axis=-1, keepdims=True), sk)
    p = jnp.exp(s - m)
    return p / (jnp.sum(p, axis=-1, keepdims=True) + jnp.exp(sk - m))


def swa_prompt(q, k, v, sinks, rel_bias):
    b, s = q.shape[:2]
    nb = s // WINDOW
    qb = q.reshape(b, nb, WINDOW, A_KV_HEADS, A_GROUP, HEAD_DIM)
    kb = k.reshape(b, nb, WINDOW, A_KV_HEADS, HEAD_DIM)
    vb = v.reshape(b, nb, WINDOW, A_KV_HEADS, HEAD_DIM)
    pad = ((0, 0), (1, 0), (0, 0), (0, 0), (0, 0))
    kk = jnp.concatenate([jnp.pad(kb, pad)[:, :-1], kb], axis=2)
    vv = jnp.concatenate([jnp.pad(vb, pad)[:, :-1], vb], axis=2)
    qpos = WINDOW + jnp.arange(WINDOW)
    kpos = jnp.arange(2 * WINDOW)
    dist = qpos[:, None] - kpos[None, :]
    band = (dist >= 0) & (dist <= WINDOW)
    has_prev = (jnp.arange(nb)[:, None, None] > 0) | (kpos[None, None, :] >= WINDOW)
    mask = band[None] & has_prev
    sc = jnp.einsum('bnqhgd,bnjhd->bnhgqj', qb, kk,
                    preferred_element_type=jnp.float32) * (HEAD_DIM ** -0.5)
    sc = sc + t5_bias(dist, rel_bias)
    sc = jnp.where(mask[None, :, None, None], sc, NEG)
    p = sink_softmax(sc, sinks)
    o = jnp.einsum('bnhgqj,bnjhd->bnqhgd', p.astype(v.dtype), vv)
    wb = min(WINDOW, s)
    return o.reshape(b, s, A_Q_W), k[:, s - wb:], v[:, s - wb:]


def swa_sample(q, k, v, buf_k, buf_v, sinks, rel_bias):
    bd, t = q.shape[:2]
    wb = buf_k.shape[1]
    kk = jnp.concatenate([buf_k, k], axis=1)
    vv = jnp.concatenate([buf_v, v], axis=1)
    qpos = jnp.arange(t)
    kpos = jnp.arange(wb + t) - wb
    dist = qpos[:, None] - kpos[None, :]
    mask = (dist >= 0) & (dist <= WINDOW)
    qg = q.reshape(bd, t, A_KV_HEADS, A_GROUP, HEAD_DIM)
    sc = jnp.einsum('bqhgd,bjhd->bhgqj', qg, kk,
                    preferred_element_type=jnp.float32) * (HEAD_DIM ** -0.5)
    sc = jnp.where(mask, sc + t5_bias(dist, rel_bias), NEG)
    p = sink_softmax(sc, sinks)
    o = jnp.einsum('bhgqj,bjhd->bqhgd', p.astype(v.dtype), vv)
    return o.reshape(bd, t, A_Q_W), kk[:, t:], vv[:, t:]


def spatial_gate(u, v, ln_g, ln_b, w_s, b_s):
    b, s, _ = u.shape
    u = jax.nn.gelu(u, approximate=False)
    v = layer_norm(jax.nn.gelu(v, approximate=False), ln_g, ln_b)
    L = min(s, CHUNK)
    nc = s // L
    w = jnp.tril(w_s[:, :L, :L])
    vc = v.reshape(b, nc, L, SG_GROUPS, SG_GROUP_DIM)
    y = jnp.einsum('gij,bcjgd->bcigd', w, vc) + b_s[:, :L].T[None, None, :, :, None]
    return u * y.reshape(b, s, SG_WIDTH), v


def even_mixer(h, buf_k, buf_v, w_in, w_out, sinks, rel_bias, ln_g, ln_b, w_s, b_s):
    b, s, _ = h.shape
    p = h @ w_in
    q, k, v, u, g = jnp.split(p, [A_Q_W, A_Q_W + A_KV_W, A_Q_W + 2 * A_KV_W,
                                  A_Q_W + 2 * A_KV_W + SG_WIDTH], axis=-1)
    q = q.reshape(b, s, A_HEADS, HEAD_DIM)
    k = k.reshape(b, s, A_KV_HEADS, HEAD_DIM)
    v = v.reshape(b, s, A_KV_HEADS, HEAD_DIM)
    if buf_k is None:
        a, nk, nv = swa_prompt(q, k, v, sinks, rel_bias)
    else:
        a, nk, nv = swa_sample(q, k, v, buf_k, buf_v, sinks, rel_bias)
    sg, v_rows = spatial_gate(u, g, ln_g, ln_b, w_s, b_s)
    return jnp.concatenate([a, sg], axis=-1) @ w_out, nk, nv, v_rows


def short_conv(x_in, gate_b, gate_c, conv_w, prev):
    s = x_in.shape[1]
    z = jnp.concatenate([prev, gate_c * x_in], axis=1)
    y = conv_w[0] * z[:, 0:s]
    for j in range(1, CONV_WIDTH):
        y = y + conv_w[j] * z[:, j:j + s]
    return gate_b * y, z[:, s:]


def rope(x, pos):
    half = ROPE_DIM // 2
    inv = jnp.exp(-math.log(ROPE_THETA) * jnp.arange(half, dtype=jnp.float32) / half)
    ang = pos.astype(jnp.float32)[:, None] * inv[None, :]
    cos = jnp.cos(ang)[None, :, None, :]
    sin = jnp.sin(ang)[None, :, None, :]
    xf = x.astype(jnp.float32)
    x1, x2 = xf[..., :half], xf[..., half:]
    return jnp.concatenate([x1 * cos - x2 * sin, x2 * cos + x1 * sin], axis=-1).astype(x.dtype)


def mla_project(c_q, c_kv, k_pe, pos, q_norm, w_uq, kv_norm, w_uk):
    b, s, _ = c_q.shape
    q = (rms_norm(c_q, q_norm) @ w_uq).reshape(b, s, MLA_HEADS, NOPE_DIM + ROPE_DIM)
    q_nope = q[..., :NOPE_DIM]
    q_pe = rope(q[..., NOPE_DIM:], pos)
    k_pe = rope(k_pe[:, :, None, :], pos)[:, :, 0]
    ckv = rms_norm(c_kv, kv_norm)
    q_abs = jnp.einsum('bshn,chn->bshc', q_nope, w_uk)
    return q_abs, q_pe, ckv, k_pe


def mla_scores(q_abs, q_pe, kc, kp):
    scale = (NOPE_DIM + ROPE_DIM) ** -0.5
    return (jnp.einsum('bqhc,bjc->bhqj', q_abs, kc, preferred_element_type=jnp.float32)
            + jnp.einsum('bqhr,bjr->bhqj', q_pe, kp, preferred_element_type=jnp.float32)) * scale


def mla_prompt_attend(q_abs, q_pe, ckv, k_pe):
    b, s, h, c = q_abs.shape
    blk = min(MLA_BLOCK, s)
    nb = s // blk
    qa = jnp.moveaxis(q_abs.reshape(b, nb, blk, h, c), 1, 0)
    qp = jnp.moveaxis(q_pe.reshape(b, nb, blk, h, ROPE_DIM), 1, 0)
    kpos = jnp.arange(s)

    def one_block(args):
        i, qa_i, qp_i = args
        sc = mla_scores(qa_i, qp_i, ckv, k_pe)
        qpos = i * blk + jnp.arange(blk)
        sc = jnp.where(kpos[None, :] <= qpos[:, None], sc, NEG)
        p = jax.nn.softmax(sc, axis=-1)
        return jnp.einsum('bhqj,bjc->bqhc', p.astype(ckv.dtype), ckv)

    o = lax.map(one_block, (jnp.arange(nb), qa, qp))
    return jnp.moveaxis(o, 0, 1).reshape(b, s, h, c)


def mla_sample_attend(q_abs, q_pe, ckv, k_pe, pool_ckv, pool_kpe, page_table):
    bd, t, h, c = q_abs.shape
    past = page_table.shape[1] * PAGE_SIZE
    past_ckv = pool_ckv[page_table].reshape(bd, past, c)
    past_kpe = pool_kpe[page_table].reshape(bd, past, ROPE_DIM)
    kc = jnp.concatenate([past_ckv, ckv], axis=1)
    kp = jnp.concatenate([past_kpe, k_pe], axis=1)
    sc = mla_scores(q_abs, q_pe, kc, kp)
    kpos = jnp.arange(past + t)
    qpos = past + jnp.arange(t)
    sc = jnp.where(kpos[None, :] <= qpos[:, None], sc, NEG)
    p = jax.nn.softmax(sc, axis=-1)
    return jnp.einsum('bhqj,bjc->bqhc', p.astype(kc.dtype), kc)


def odd_mixer(h, pos, conv_prev, pool_ckv, pool_kpe, page_table,
              w_in, w_out, conv_w, q_norm, w_uq, kv_norm, w_uk, w_uv):
    b, s, _ = h.shape
    p = h @ w_in
    gate_b, gate_c, x_in, c_q, c_kv, k_pe = jnp.split(
        p, [C_WIDTH, 2 * C_WIDTH, 3 * C_WIDTH, 3 * C_WIDTH + Q_LORA,
            3 * C_WIDTH + Q_LORA + KV_LORA], axis=-1)
    c_out, conv_state = short_conv(x_in, gate_b, gate_c, conv_w, conv_prev)
    q_abs, q_pe, ckv, kpe = mla_project(c_q, c_kv, k_pe, pos, q_norm, w_uq, kv_norm, w_uk)
    if pool_ckv is None:
        o_lat = mla_prompt_attend(q_abs, q_pe, ckv, kpe)
    else:
        o_lat = mla_sample_attend(q_abs, q_pe, ckv, kpe, pool_ckv, pool_kpe, page_table)
    o = jnp.einsum('bshc,chv->bshv', o_lat, w_uv).reshape(b, s, MLA_HEADS * V_DIM)
    return jnp.concatenate([c_out, o], axis=-1) @ w_out, conv_state, ckv, kpe


def setup_inputs(seed: int = 0) -> dict:
    key = jax.random.key(seed)
    keys = jax.random.split(key, 40)
    counter = [0]

    def nxt():
        k = keys[counter[0]]
        counter[0] += 1
        return k

    def nrm(shape, scale):
        return jax.random.normal(nxt(), shape, jnp.float32) * scale

    def gain(shape):
        return 1.0 + nrm(shape, 0.01)

    n_pages = PAST_LEN // PAGE_SIZE
    n_used = DEC_BATCH * n_pages
    n_pool = n_used + n_used // 4
    wb = min(WINDOW, PAST_LEN)
    d = D_MODEL
    inp = {}
    inp['x_prompt'] = nrm((BATCH, SEQ, d), 1.0)
    inp['x_sample'] = nrm((DEC_BATCH, DEC_SEQ, d), 1.0)
    inp['cache_swa_k'] = nrm((N_EVEN, DEC_BATCH, wb, A_KV_HEADS, HEAD_DIM), 1.0)
    inp['cache_swa_v'] = nrm((N_EVEN, DEC_BATCH, wb, A_KV_HEADS, HEAD_DIM), 1.0)
    inp['state_conv'] = nrm((N_ODD, DEC_BATCH, CONV_WIDTH - 1, C_WIDTH), 1.0)
    inp['cache_mla_ckv'] = nrm((N_ODD, n_pool, PAGE_SIZE, KV_LORA), 1.0)
    inp['cache_mla_kpe'] = nrm((N_ODD, n_pool, PAGE_SIZE, ROPE_DIM), 1.0)
    inp['page_table'] = jax.random.permutation(nxt(), n_pool)[:n_used].reshape(
        DEC_BATCH, n_pages).astype(jnp.int32)
    inp['rel_bias'] = nrm((N_BUCKETS, A_HEADS), 0.2)
    inp['ffn1_norm'] = gain((DEPTH, d))
    inp['ffn1_w_gate'] = nrm((DEPTH, d, D_FF), d ** -0.5)
    inp['ffn1_w_up'] = nrm((DEPTH, d, D_FF), d ** -0.5)
    inp['ffn1_w_down'] = nrm((DEPTH, D_FF, d), D_FF ** -0.5)
    inp['mix_norm'] = gain((DEPTH, d))
    inp['ffn2_norm'] = gain((DEPTH, d))
    inp['ffn2_w_gate'] = nrm((DEPTH, d, D_FF), d ** -0.5)
    inp['ffn2_w_up'] = nrm((DEPTH, d, D_FF), d ** -0.5)
    inp['ffn2_w_down'] = nrm((DEPTH, D_FF, d), D_FF ** -0.5)
    inp['even_w_in'] = nrm((N_EVEN, d, IN_EVEN), d ** -0.5)
    inp['even_w_out'] = nrm((N_EVEN, MIX_EVEN, d), MIX_EVEN ** -0.5)
    inp['attn_sinks'] = nrm((N_EVEN, A_HEADS), 1.0)
    inp['sgu_ln_g'] = gain((N_EVEN, SG_WIDTH))
    inp['sgu_ln_b'] = nrm((N_EVEN, SG_WIDTH), 0.01)
    inp['sgu_w'] = nrm((N_EVEN, SG_GROUPS, CHUNK, CHUNK), CHUNK ** -0.5)
    inp['sgu_b'] = nrm((N_EVEN, SG_GROUPS, CHUNK), 0.01)
    inp['odd_w_in'] = nrm((N_ODD, d, IN_ODD), d ** -0.5)
    inp['odd_w_out'] = nrm((N_ODD, MIX_ODD, d), MIX_ODD ** -0.5)
    inp['conv_w'] = nrm((N_ODD, CONV_WIDTH, C_WIDTH), CONV_WIDTH ** -0.5)
    inp['mla_q_norm'] = gain((N_ODD, Q_LORA))
    inp['mla_w_uq'] = nrm((N_ODD, Q_LORA, MLA_HEADS * (NOPE_DIM + ROPE_DIM)), Q_LORA ** -0.5)
    inp['mla_kv_norm'] = gain((N_ODD, KV_LORA))
    inp['mla_w_uk'] = nrm((N_ODD, KV_LORA, MLA_HEADS, NOPE_DIM), KV_LORA ** -0.5)
    inp['mla_w_uv'] = nrm((N_ODD, KV_LORA, MLA_HEADS, V_DIM), KV_LORA ** -0.5)
    inp['final_norm'] = gain((d,))
    return inp


def reference(x_prompt, x_sample, cache_swa_k, cache_swa_v, state_conv, cache_mla_ckv,
              cache_mla_kpe, page_table, rel_bias, ffn1_norm, ffn1_w_gate, ffn1_w_up,
              ffn1_w_down, mix_norm, ffn2_norm, ffn2_w_gate, ffn2_w_up, ffn2_w_down,
              even_w_in, even_w_out, attn_sinks, sgu_ln_g, sgu_ln_b, sgu_w, sgu_b,
              odd_w_in, odd_w_out, conv_w, mla_q_norm, mla_w_uq, mla_kv_norm, mla_w_uk,
              mla_w_uv, final_norm):
    bp, sp, _ = x_prompt.shape
    ts = x_sample.shape[1]
    past = page_table.shape[1] * PAGE_SIZE
    pos_p = jnp.arange(sp)
    pos_s = past + jnp.arange(ts)
    xp, xs = x_prompt, x_sample
    swa_kp, swa_vp, swa_ks, swa_vs, sgu_vs = [], [], [], [], []
    conv_p, conv_s, ckv_p, kpe_p, ckv_s, kpe_s = [], [], [], [], [], []
    for l in range(DEPTH):
        f1 = (ffn1_w_gate[l], ffn1_w_up[l], ffn1_w_down[l])
        xp = xp + 0.5 * swiglu(rms_norm(xp, ffn1_norm[l]), *f1)
        xs = xs + 0.5 * swiglu(rms_norm(xs, ffn1_norm[l]), *f1)
        hp = rms_norm(xp, mix_norm[l])
        hs = rms_norm(xs, mix_norm[l])
        if l % 2 == 0:
            e = l // 2
            ew = (even_w_in[e], even_w_out[e], attn_sinks[e], rel_bias,
                  sgu_ln_g[e], sgu_ln_b[e], sgu_w[e], sgu_b[e])
            mp, kp_new, vp_new, _ = even_mixer(hp, None, None, *ew)
            ms, ks_new, vs_new, v_rows = even_mixer(hs, cache_swa_k[e], cache_swa_v[e], *ew)
            swa_kp.append(kp_new)
            swa_vp.append(vp_new)
            swa_ks.append(ks_new)
            swa_vs.append(vs_new)
            sgu_vs.append(v_rows)
        else:
            j = l // 2
            ow = (odd_w_in[j], odd_w_out[j], conv_w[j], mla_q_norm[j], mla_w_uq[j],
                  mla_kv_norm[j], mla_w_uk[j], mla_w_uv[j])
            zero_prev = jnp.zeros((bp, CONV_WIDTH - 1, C_WIDTH), hp.dtype)
            mp, cp_new, ckvp_new, kpep_new = odd_mixer(hp, pos_p, zero_prev, None, None, None, *ow)
            ms, cs_new, ckvs_new, kpes_new = odd_mixer(hs, pos_s, state_conv[j], cache_mla_ckv[j],
                                                       cache_mla_kpe[j], page_table, *ow)
            conv_p.append(cp_new)
            conv_s.append(cs_new)
            ckv_p.append(ckvp_new)
            kpe_p.append(kpep_new)
            ckv_s.append(ckvs_new)
            kpe_s.append(kpes_new)
        xp = xp + mp
        xs = xs + ms
        f2 = (ffn2_w_gate[l], ffn2_w_up[l], ffn2_w_down[l])
        xp = xp + 0.5 * swiglu(rms_norm(xp, ffn2_norm[l]), *f2)
        xs = xs + 0.5 * swiglu(rms_norm(xs, ffn2_norm[l]), *f2)
    y_prompt = rms_norm(xp, final_norm)
    y_sample = rms_norm(xs, final_norm)
    swa_k_prompt = jnp.stack(swa_kp)
    swa_v_prompt = jnp.stack(swa_vp)
    swa_k_sample = jnp.stack(swa_ks)
    swa_v_sample = jnp.stack(swa_vs)
    sgu_v_sample = jnp.stack(sgu_vs)
    conv_prompt = jnp.stack(conv_p)
    conv_sample = jnp.stack(conv_s)
    mla_ckv_prompt = jnp.stack(ckv_p)
    mla_kpe_prompt = jnp.stack(kpe_p)
    mla_ckv_sample = jnp.stack(ckv_s)
    mla_kpe_sample = jnp.stack(kpe_s)
    return (y_prompt, y_sample, swa_k_prompt, swa_v_prompt, swa_k_sample, swa_v_sample,
            sgu_v_sample, conv_prompt, conv_sample, mla_ckv_prompt, mla_kpe_prompt,
            mla_ckv_sample, mla_kpe_sample)
```

```python
import functools
import math

import numpy as np
import jax
import jax.numpy as jnp
from jax import lax
from jax.experimental import pallas as pl
from jax.experimental.pallas import tpu as pltpu

F32 = jnp.float32
BF16 = jnp.bfloat16

HEAD_DIM = 64
A_HEADS = 8
A_KV_HEADS = 2
A_Q_W = A_HEADS * HEAD_DIM
A_KV_W = A_KV_HEADS * HEAD_DIM
WINDOW = 128
N_BUCKETS = 32
MAX_DISTANCE = 128
SG_GROUPS = 4
SG_GROUP_DIM = 128
SG_WIDTH = SG_GROUPS * SG_GROUP_DIM
CHUNK = 128
C_WIDTH = 512
CONV_WIDTH = 3
MLA_HEADS = 8
NOPE_DIM = 64
ROPE_DIM = 32
V_DIM = 64
Q_LORA = 768
KV_LORA = 256
ROPE_THETA = 10000.0
PAGE_SIZE = 128
EPS = 1e-6
NEG = -1e30

LANE = 128
KC_W = KV_LORA + LANE
TM = 512
FC = 256
TQ_EVEN = 512
TQ_ODD = 256
KB_ODD = 512
BB_EVEN = 16
VMEM_LIMIT = 56 * 1024 * 1024


def _cp(sem, vmem=VMEM_LIMIT):
    return pltpu.CompilerParams(dimension_semantics=sem, vmem_limit_bytes=vmem)


def _dot(a, b):
    return jnp.dot(a, b, preferred_element_type=F32)


def _dot_nt(a, b):
    return lax.dot_general(a, b, (((1,), (1,)), ((), ())), preferred_element_type=F32)


def _rms(x, g):
    ms = jnp.mean(x * x, axis=-1, keepdims=True)
    return x * lax.rsqrt(ms + EPS) * g


def _gelu(x):
    return 0.5 * x * (1.0 + lax.erf(x * np.float32(math.sqrt(0.5))))


def _const_spec(shape):
    n = len(shape)
    return pl.BlockSpec(shape, lambda *_: (0,) * n, pipeline_mode=pl.Buffered(1))


def _smem_spec():
    return pl.BlockSpec(memory_space=pltpu.SMEM)


def _ffn_kernel(*refs, has_mix, has_final, n_prompt_tiles, n_chunks):
    it = iter(refs)
    x_ref = next(it)
    if has_mix:
        mixp_ref, mixs_ref, wout_ref = next(it), next(it), next(it)
    g_ref, wgu_ref, wd_ref = next(it), next(it), next(it)
    if has_final:
        fg_ref = next(it)
    o_ref = next(it)
    acc_ref = next(it)

    x = x_ref[...]
    if has_mix:
        is_sample = pl.program_id(0) >= n_prompt_tiles
        mix = jnp.where(is_sample, mixs_ref[...], mixp_ref[...])
        x = x + _dot(mix, wout_ref[...])
    h = _rms(x, g_ref[...]).astype(BF16)
    for c in range(n_chunks):
        gu = _dot(h, wgu_ref[:, 2 * FC * c:2 * FC * (c + 1)])
        g = gu[:, :FC]
        u = gu[:, FC:]
        a = (g * jax.nn.sigmoid(g) * u).astype(BF16)
        d = _dot(a, wd_ref[FC * c:FC * (c + 1), :])
        if c == 0:
            acc_ref[...] = d
        else:
            acc_ref[...] += d
    y = x + 0.5 * acc_ref[...]
    if has_final:
        y = _rms(y, fg_ref[...])
    o_ref[...] = y


def _ffn(x, norm_g, wgu, wd, n_prompt_tiles, mix=None, final_g=None):
    t, d = x.shape
    f = wd.shape[0]
    n_tiles = t // TM
    row = lambda i: (i, 0)
    args, specs = [x], [pl.BlockSpec((TM, d), row)]
    if mix is not None:
        mix_p, mix_s, w_out = mix
        last_p = n_prompt_tiles - 1
        args += [mix_p, mix_s, w_out]
        specs += [pl.BlockSpec((TM, d), lambda i: (jnp.minimum(i, last_p), 0)),
                  pl.BlockSpec((TM, d), lambda i: (0, 0)),
                  _const_spec((d, d))]
    args += [norm_g.reshape(1, d), wgu, wd]
    specs += [_const_spec((1, d)), _const_spec((d, 2 * f)), _const_spec((f, d))]
    if final_g is not None:
        args.append(final_g.reshape(1, d))
        specs.append(_const_spec((1, d)))
    body = functools.partial(_ffn_kernel, has_mix=mix is not None, has_final=final_g is not None,
                             n_prompt_tiles=n_prompt_tiles, n_chunks=f // FC)
    return pl.pallas_call(
        body,
        grid=(n_tiles,),
        in_specs=specs,
        out_specs=pl.BlockSpec((TM, d), row),
        out_shape=jax.ShapeDtypeStruct((t, d), F32),
        scratch_shapes=[pltpu.VMEM((TM, d), F32)],
        compiler_params=_cp(("arbitrary",)),
        name="ffn",
    )(*args)


def _prep_ffn_weights(w_gate, w_up, w_down):
    d, f = w_gate.shape
    nc = f // FC
    wgu = jnp.stack([w_gate.reshape(d, nc, FC), w_up.reshape(d, nc, FC)], axis=2)
    return wgu.reshape(d, 2 * f).astype(BF16), w_down.astype(BF16)


def _t5_bucket_np(dist):
    n = np.maximum(dist, 0)
    max_exact = N_BUCKETS // 2
    nf = np.maximum(n, 1).astype(np.float32)
    large = max_exact + (np.log(nf / np.float32(max_exact)) / np.float32(math.log(MAX_DISTANCE / max_exact))
                         * np.float32(N_BUCKETS - max_exact)).astype(np.int32)
    return np.where(n < max_exact, n, np.minimum(large, N_BUCKETS - 1)).astype(np.int32)


def _bias_tables_np(ts):
    q = np.arange(WINDOW)[:, None]
    j = np.arange(2 * WINDOW)[None, :]
    bucket_p = _t5_bucket_np(q - j + WINDOW)
    nb = 8 // ts
    cols = nb * WINDOW + 16
    bucket_s = -np.ones((8, cols), np.int32)
    for r in range(8):
        bi, t = divmod(r, ts)
        for c in range(cols):
            if c < nb * WINDOW:
                bj, jj = divmod(c, WINDOW)
                dist = t - (jj - WINDOW)
            elif c < nb * WINDOW + 8:
                bj, tt = divmod(c - nb * WINDOW, ts)
                dist = t - tt
            else:
                continue
            if bj == bi and 0 <= dist <= WINDOW:
                bucket_s[r, c] = _t5_bucket_np(np.array(dist))
    return bucket_p, bucket_s


def _bias_kernel(rb_ref, bp_ref, bs_ref, op_ref, os_ref):
    h = pl.program_id(0)
    bp = bp_ref[...]
    bs = bs_ref[...]
    accp = jnp.zeros(bp.shape, F32)
    accs = jnp.full(bs.shape, NEG, F32)
    for k in range(N_BUCKETS):
        val = rb_ref[k, h]
        accp = jnp.where(bp == k, val, accp)
        accs = jnp.where(bs == k, val, accs)
    op_ref[0] = accp
    os_ref[0] = accs


def _bias_tables(rel_bias, ts):
    bucket_p, bucket_s = _bias_tables_np(ts)
    return pl.pallas_call(
        _bias_kernel,
        grid=(A_HEADS,),
        in_specs=[_smem_spec(),
                  pl.BlockSpec(bucket_p.shape, lambda h: (0, 0)),
                  pl.BlockSpec(bucket_s.shape, lambda h: (0, 0))],
        out_specs=[pl.BlockSpec((1,) + bucket_p.shape, lambda h: (h, 0, 0)),
                   pl.BlockSpec((1,) + bucket_s.shape, lambda h: (h, 0, 0))],
        out_shape=[jax.ShapeDtypeStruct((A_HEADS,) + bucket_p.shape, F32),
                   jax.ShapeDtypeStruct((A_HEADS,) + bucket_s.shape, F32)],
        compiler_params=_cp(("arbitrary",)),
        name="t5_bias",
    )(rel_bias, jnp.asarray(bucket_p), jnp.asarray(bucket_s))


def _inproj_even_kernel(x_ref, g_ref, w_ref, q_ref, kv_ref, ug_ref):
    h = _rms(x_ref[...], g_ref[...]).astype(BF16)
    p = _dot(h, w_ref[...])
    q_ref[...] = (p[:, :A_Q_W] * np.float32(HEAD_DIM ** -0.5)).astype(BF16)
    kv_ref[...] = p[:, A_Q_W:A_Q_W + 2 * A_KV_W]
    ug_ref[...] = p[:, A_Q_W + 2 * A_KV_W:]


def _inproj_even(x, norm_g, w_in):
    t, d = x.shape
    n = w_in.shape[1]
    row = lambda i: (i, 0)
    return pl.pallas_call(
        _inproj_even_kernel,
        grid=(t // TM,),
        in_specs=[pl.BlockSpec((TM, d), row), _const_spec((1, d)), _const_spec((d, n))],
        out_specs=[pl.BlockSpec((TM, A_Q_W), row), pl.BlockSpec((TM, 2 * A_KV_W), row),
                   pl.BlockSpec((TM, 2 * SG_WIDTH), row)],
        out_shape=[jax.ShapeDtypeStruct((t, A_Q_W), BF16), jax.ShapeDtypeStruct((t, 2 * A_KV_W), F32),
                   jax.ShapeDtypeStruct((t, 2 * SG_WIDTH), F32)],
        compiler_params=_cp(("arbitrary",)),
        name="inproj_even",
    )(x, norm_g.reshape(1, d), w_in)


def _head_variants(x):
    lo = lax.broadcasted_iota(jnp.int32, x.shape, 1) < HEAD_DIM
    xr = pltpu.roll(x, HEAD_DIM, 1)
    z = jnp.zeros_like(x)
    return [[jnp.where(lo, x, z).astype(BF16), jnp.where(lo, z, xr).astype(BF16)],
            [jnp.where(lo, xr, z).astype(BF16), jnp.where(lo, z, x).astype(BF16)]]


def _sink_softmax_pv(s, sink, v):
    mx = jnp.maximum(jnp.max(s, axis=-1, keepdims=True), sink)
    p = jnp.exp(s - mx)
    den = jnp.sum(p, axis=-1, keepdims=True) + jnp.exp(sink - mx)
    return _dot(p.astype(BF16), v) * (1.0 / den)


def _layer_norm(x, g, b):
    xc = x - jnp.mean(x, axis=-1, keepdims=True)
    y = xc * lax.rsqrt(jnp.mean(xc * xc, axis=-1, keepdims=True) + EPS)
    return y * g + b


def _even_prompt_kernel(q_ref, kv_ref, kvp_ref, ug_ref, bias_ref, sink_ref, lng_ref, lnb_ref, ws_ref,
                        bst_ref, o_ref, *, tq):
    nblk = tq // WINDOW
    step = pl.program_id(1)
    kv_all = jnp.concatenate([kvp_ref[...], kv_ref[...]], axis=0)
    kvar = _head_variants(kv_all[:, :A_KV_W])
    vvar = _head_variants(kv_all[:, A_KV_W:])
    qi = lax.broadcasted_iota(jnp.int32, (WINDOW, 2 * WINDOW), 0)
    kj = lax.broadcasted_iota(jnp.int32, (WINDOW, 2 * WINDOW), 1)
    dist = qi - kj + WINDOW
    band = (dist >= 0) & (dist <= WINDOW)
    cur = kj >= WINDOW
    ri = lax.broadcasted_iota(jnp.int32, (CHUNK, CHUNK), 0)
    ci = lax.broadcasted_iota(jnp.int32, (CHUNK, CHUNK), 1)
    wt = [jnp.where(ri >= ci, ws_ref[g], 0.0).astype(BF16) for g in range(SG_GROUPS)]
    lng, lnb = lng_ref[...], lnb_ref[...]
    for n in range(nblk):
        r0 = n * WINDOW
        has_prev = (step * nblk + n) > 0
        mask = band & (cur | has_prev)
        for m in range(A_HEADS // 2):
            kvh = (2 * m) // (A_HEADS // A_KV_HEADS)
            qp = q_ref[r0:r0 + WINDOW, LANE * m:LANE * (m + 1)]
            o = None
            for half in range(2):
                hd = 2 * m + half
                s = _dot_nt(qp, kvar[kvh][half][r0:r0 + 2 * WINDOW])
                s = jnp.where(mask, s + bias_ref[hd], NEG)
                oh = _sink_softmax_pv(s, sink_ref[hd], vvar[kvh][half][r0:r0 + 2 * WINDOW])
                o = oh if o is None else o + oh
            o_ref[r0:r0 + WINDOW, LANE * m:LANE * (m + 1)] = o.astype(BF16)
        u = _gelu(ug_ref[r0:r0 + WINDOW, :SG_WIDTH])
        v = _layer_norm(_gelu(ug_ref[r0:r0 + WINDOW, SG_WIDTH:]), lng, lnb)
        for g in range(SG_GROUPS):
            sl = slice(SG_GROUP_DIM * g, SG_GROUP_DIM * (g + 1))
            y = _dot(wt[g], v[:, sl].astype(BF16)) + bst_ref[:, g:g + 1]
            o_ref[r0:r0 + WINDOW, A_Q_W + SG_GROUP_DIM * g:A_Q_W + SG_GROUP_DIM * (g + 1)] = (
                u[:, sl] * y).astype(BF16)


def _mix_even_prompt(q, kv, ug, bias_p, sinks, ln_g, ln_b, w_s, b_s, nb, s):
    tq = TQ_EVEN
    nq = s // tq
    blk = lambda b, i: (b * nq + i, 0)
    prev = lambda b, i: (jnp.maximum((b * nq + i) * (tq // WINDOW) - 1, 0), 0)
    return pl.pallas_call(
        functools.partial(_even_prompt_kernel, tq=tq),
        grid=(nb, nq),
        in_specs=[pl.BlockSpec((tq, A_Q_W), blk), pl.BlockSpec((tq, 2 * A_KV_W), blk),
                  pl.BlockSpec((WINDOW, 2 * A_KV_W), prev), pl.BlockSpec((tq, 2 * SG_WIDTH), blk),
                  _const_spec(bias_p.shape), _smem_spec(), _const_spec((1, SG_WIDTH)),
                  _const_spec((1, SG_WIDTH)), _const_spec(w_s.shape), _const_spec((CHUNK, SG_GROUPS))],
        out_specs=pl.BlockSpec((tq, A_Q_W + SG_WIDTH), blk),
        out_shape=jax.ShapeDtypeStruct((nb * s, A_Q_W + SG_WIDTH), BF16),
        compiler_params=_cp(("arbitrary", "arbitrary")),
        name="mix_even_prompt",
    )(q, kv, kv, ug, bias_p, sinks, ln_g.reshape(1, -1), ln_b.reshape(1, -1), w_s, b_s.T)


def _even_sample_kernel(q_ref, kvn_ref, ck_ref, cv_ref, bias_ref, sink_ref, a_ref, nk_ref, nv_ref, *, ts):
    bb = ck_ref.shape[0]
    gpb = 8 // ts
    wb = ck_ref.shape[1]
    row = lax.broadcasted_iota(jnp.int32, (16, 1), 0)
    zpad = jnp.zeros((8, A_KV_W), F32)
    heads_per_kv = A_HEADS // A_KV_HEADS
    for g in range(bb // gpb):
        r0 = 8 * g
        qg = q_ref[r0:r0 + 8, :]
        kn = kvn_ref[r0:r0 + 8, :A_KV_W]
        vn = kvn_ref[r0:r0 + 8, A_KV_W:]
        kg = jnp.concatenate([ck_ref[gpb * g + i] for i in range(gpb)] + [kn, zpad], axis=0)
        vg = jnp.concatenate([cv_ref[gpb * g + i] for i in range(gpb)] + [vn, zpad], axis=0)
        kvar = _head_variants(kg)
        vvar = _head_variants(vg)
        for kvh in range(A_KV_HEADS):
            c0 = 2 * LANE * kvh
            qk = jnp.concatenate([qg[:, c0:c0 + LANE], qg[:, c0 + LANE:c0 + 2 * LANE]], axis=0).astype(BF16)
            o = None
            for half in range(2):
                h0 = heads_per_kv * kvh + half
                h1 = h0 + 2
                bm = jnp.concatenate([bias_ref[h0], bias_ref[h1]], axis=0)
                sink = jnp.where(row < 8, sink_ref[h0], sink_ref[h1])
                s = _dot_nt(qk, kvar[kvh][half])
                s = jnp.where(bm > 0.5 * NEG, s + bm, NEG)
                oh = _sink_softmax_pv(s, sink, vvar[kvh][half])
                o = oh if o is None else o + oh
            a_ref[r0:r0 + 8, c0:c0 + LANE] = o[:8]
            a_ref[r0:r0 + 8, c0 + LANE:c0 + 2 * LANE] = o[8:]
        for i in range(gpb):
            b = gpb * g + i
            nk_ref[b, 0:wb - ts, :] = ck_ref[b, ts:wb, :]
            nk_ref[b, wb - ts:wb, :] = kn[ts * i:ts * (i + 1)]
            nv_ref[b, 0:wb - ts, :] = cv_ref[b, ts:wb, :]
            nv_ref[b, wb - ts:wb, :] = vn[ts * i:ts * (i + 1)]


def _mix_even_sample(q_b, kvn_b, cache_k, cache_v, bias_s, sinks, ts):
    bd, wb, _ = cache_k.shape
    bb = BB_EVEN
    rows = bb * ts
    blk2 = lambda i: (i, 0)
    blk3 = lambda i: (i, 0, 0)
    return pl.pallas_call(
        functools.partial(_even_sample_kernel, ts=ts),
        grid=(bd // bb,),
        in_specs=[pl.BlockSpec((rows, A_Q_W), blk2), pl.BlockSpec((rows, 2 * A_KV_W), blk2),
                  pl.BlockSpec((bb, wb, A_KV_W), blk3), pl.BlockSpec((bb, wb, A_KV_W), blk3),
                  _const_spec(bias_s.shape), _smem_spec()],
        out_specs=[pl.BlockSpec((rows, A_Q_W), blk2), pl.BlockSpec((bb, wb, A_KV_W), blk3),
                   pl.BlockSpec((bb, wb, A_KV_W), blk3)],
        out_shape=[jax.ShapeDtypeStruct((bd * ts, A_Q_W), F32),
                   jax.ShapeDtypeStruct(cache_k.shape, F32), jax.ShapeDtypeStruct(cache_v.shape, F32)],
        compiler_params=_cp(("arbitrary",)),
        name="mix_even_sample",
    )(q_b, kvn_b, cache_k, cache_v, bias_s, sinks)


def _sgu_sample_kernel(ug_ref, lng_ref, lnb_ref, w_ref, b_ref, sg_ref, v_ref, *, ts, bd):
    u = _gelu(ug_ref[:, :SG_WIDTH])
    v = _layer_norm(_gelu(ug_ref[:, SG_WIDTH:]), lng_ref[...], lnb_ref[...])
    v_ref[...] = v
    for i in range(ts):
        for g in range(SG_GROUPS):
            sl = slice(SG_GROUP_DIM * g, SG_GROUP_DIM * (g + 1))
            y = jnp.full((bd, SG_GROUP_DIM), b_ref[g, i], F32)
            for j in range(i + 1):
                y = y + w_ref[g, i * ts + j] * v[bd * j:bd * (j + 1), sl]
            sg_ref[bd * i:bd * (i + 1), sl] = (u[bd * i:bd * (i + 1), sl] * y).astype(BF16)


def _sgu_sample(ug, ln_g, ln_b, w_s, b_s, n_prompt_rows, ts, bd):
    rows = ts * bd
    off = n_prompt_rows // rows
    w_small = w_s[:, :ts, :ts].reshape(SG_GROUPS, ts * ts)
    b_small = b_s[:, :ts]
    return pl.pallas_call(
        functools.partial(_sgu_sample_kernel, ts=ts, bd=bd),
        grid=(1,),
        in_specs=[pl.BlockSpec((rows, 2 * SG_WIDTH), lambda i: (off, 0)), _const_spec((1, SG_WIDTH)),
                  _const_spec((1, SG_WIDTH)), _smem_spec(), _smem_spec()],
        out_specs=[pl.BlockSpec((rows, SG_WIDTH), lambda i: (0, 0)),
                   pl.BlockSpec((rows, SG_WIDTH), lambda i: (0, 0))],
        out_shape=[jax.ShapeDtypeStruct((rows, SG_WIDTH), BF16), jax.ShapeDtypeStruct((rows, SG_WIDTH), F32)],
        compiler_params=_cp(("arbitrary",)),
        name="sgu_sample",
    )(ug, ln_g.reshape(1, -1), ln_b.reshape(1, -1), w_small, b_small)


def _rope_tables_np(s, past, ts, bd):
    half = ROPE_DIM // 2
    inv = np.exp(-math.log(ROPE_THETA) * np.arange(half, dtype=np.float32) / np.float32(half)).astype(np.float32)
    pos = np.concatenate([np.arange(s), past + np.repeat(np.arange(ts), bd)]).astype(np.float32)
    ang = (pos[:, None] * inv[None, :]).astype(np.float32).astype(np.float64)
    cos = np.cos(ang)
    sin = np.sin(ang)
    cos_t = np.tile(np.concatenate([cos, cos], axis=1), (1, MLA_HEADS)).astype(np.float32)
    sin_t = np.tile(np.concatenate([-sin, sin], axis=1), (1, MLA_HEADS)).astype(np.float32)
    return cos_t, sin_t


def _rope(x, cos_t, sin_t):
    n = x.shape[1]
    half = ROPE_DIM // 2
    first = (lax.broadcasted_iota(jnp.int32, x.shape, 1) % ROPE_DIM) < half
    partner = jnp.where(first, pltpu.roll(x, n - half, 1), pltpu.roll(x, half, 1))
    return x * cos_t + partner * sin_t


def _inproj_odd_kernel(x_ref, g_ref, w_ref, qg_ref, wuq_ref, kvg_ref, cos_ref, sin_ref,
                       ci_ref, qn_ref, qp_ref, kc_ref, ckv_ref, kpe_ref):
    h = _rms(x_ref[...], g_ref[...]).astype(BF16)
    p = _dot(h, w_ref[...])
    o_q = 3 * C_WIDTH
    o_kv = o_q + Q_LORA
    o_pe = o_kv + KV_LORA
    ci_ref[...] = p[:, :o_q].astype(BF16)
    scale = np.float32((NOPE_DIM + ROPE_DIM) ** -0.5)
    q = _dot(_rms(p[:, o_q:o_kv], qg_ref[...]).astype(BF16), wuq_ref[...])
    n_nope = MLA_HEADS * NOPE_DIM
    cos_t, sin_t = cos_ref[...], sin_ref[...]
    qn_ref[...] = (q[:, :n_nope] * scale).astype(BF16)
    qp_ref[...] = (_rope(q[:, n_nope:], cos_t, sin_t) * scale).astype(BF16)
    ckv = _rms(p[:, o_kv:o_pe], kvg_ref[...])
    kpe = _rope(p[:, o_pe:], cos_t[:, :LANE], sin_t[:, :LANE])
    ckv_ref[...] = ckv
    kpe_ref[...] = kpe
    kc_ref[:, :KV_LORA] = ckv.astype(BF16)
    kc_ref[:, KV_LORA:] = kpe.astype(BF16)


def _inproj_odd(x, norm_g, w_in, q_norm, w_uq, kv_norm, cos_t, sin_t, n_prompt_tiles, s):
    t, d = x.shape
    n = w_in.shape[1]
    row = lambda i: (i, 0)
    per_seq = s // TM
    tab = lambda i: (jnp.where(i < n_prompt_tiles, i % per_seq, per_seq), 0)
    n_pe = MLA_HEADS * ROPE_DIM
    widths = [(3 * C_WIDTH, BF16), (MLA_HEADS * NOPE_DIM, BF16), (n_pe, BF16), (KC_W, BF16),
              (KV_LORA, F32), (LANE, F32)]
    return pl.pallas_call(
        _inproj_odd_kernel,
        grid=(t // TM,),
        in_specs=[pl.BlockSpec((TM, d), row), _const_spec((1, d)), _const_spec((d, n)),
                  _const_spec((1, Q_LORA)), _const_spec(w_uq.shape), _const_spec((1, KV_LORA)),
                  pl.BlockSpec((TM, n_pe), tab), pl.BlockSpec((TM, n_pe), tab)],
        out_specs=[pl.BlockSpec((TM, w), row) for w, _ in widths],
        out_shape=[jax.ShapeDtypeStruct((t, w), dt) for w, dt in widths],
        compiler_params=_cp(("arbitrary",)),
        name="inproj_odd",
    )(x, norm_g.reshape(1, d), w_in, q_norm.reshape(1, -1), w_uq, kv_norm.reshape(1, -1), cos_t, sin_t)


def _build_qx(qn_ref, qp_ref, wuk_ref, qx_ref, rows):
    qp = qp_ref[...].astype(F32)
    keep = lax.broadcasted_iota(jnp.int32, (rows, LANE), 1) < ROPE_DIM
    n_pe = MLA_HEADS * ROPE_DIM
    for h in range(MLA_HEADS):
        pair = qn_ref[:, LANE * (h // 2):LANE * (h // 2 + 1)]
        qx_ref[h * rows:(h + 1) * rows, :KV_LORA] = _dot(pair, wuk_ref[h]).astype(BF16)
        pe = qp if h == 0 else pltpu.roll(qp, n_pe - ROPE_DIM * h, 1)
        qx_ref[h * rows:(h + 1) * rows, KV_LORA:] = jnp.where(keep, pe[:, :LANE], 0.0).astype(BF16)


def _odd_prompt_kernel(ci_ref, cih_ref, qn_ref, qp_ref, kc_ref, wuk_ref, wuv_ref, cw_ref,
                       o_ref, cst_ref, qx_ref, s_ref, p_ref, m_ref, l_ref, acc_ref, z_ref, *, tq, kb):
    step = pl.program_id(1)
    halo = 16
    zh = cih_ref[:, C_WIDTH:2 * C_WIDTH].astype(F32) * cih_ref[:, 2 * C_WIDTH:].astype(F32)
    z_ref[0:halo, :] = jnp.where(step > 0, zh, 0.0)
    z = ci_ref[:, C_WIDTH:2 * C_WIDTH].astype(F32) * ci_ref[:, 2 * C_WIDTH:].astype(F32)
    z_ref[halo:halo + tq, :] = z
    cw = cw_ref[...]
    y = cw[0:1] * z_ref[halo - 2:halo - 2 + tq, :] + cw[1:2] * z_ref[halo - 1:halo - 1 + tq, :] + cw[2:3] * z
    o_ref[:, :C_WIDTH] = (ci_ref[:, :C_WIDTH].astype(F32) * y).astype(BF16)
    cst_ref[0] = z[tq - 8:tq]

    _build_qx(qn_ref, qp_ref, wuk_ref, qx_ref, tq)
    rows = MLA_HEADS * tq
    m_ref[...] = jnp.full((rows, 1), NEG, F32)
    l_ref[...] = jnp.zeros((rows, 1), F32)
    acc_ref[...] = jnp.zeros((rows, KV_LORA), F32)
    n_kv = (step * tq + tq + kb - 1) // kb

    def kv_step(j, masked):
        k0 = pl.multiple_of(j * kb, kb)
        kc = kc_ref[pl.ds(k0, kb), :]
        s_ref[...] = _dot_nt(qx_ref[...], kc)

        def head(h, carry):
            r0 = pl.multiple_of(h * tq, tq)
            s = s_ref[pl.ds(r0, tq), :]
            if masked:
                qpos = step * tq + lax.broadcasted_iota(jnp.int32, (tq, kb), 0)
                kpos = k0 + lax.broadcasted_iota(jnp.int32, (tq, kb), 1)
                s = jnp.where(kpos <= qpos, s, NEG)
            m_old = m_ref[pl.ds(r0, tq), :]
            m_new = jnp.maximum(m_old, jnp.max(s, axis=-1, keepdims=True))
            alpha = jnp.exp(m_old - m_new)
            p = jnp.exp(s - m_new)
            l_ref[pl.ds(r0, tq), :] = alpha * l_ref[pl.ds(r0, tq), :] + jnp.sum(p, axis=-1, keepdims=True)
            m_ref[pl.ds(r0, tq), :] = m_new
            acc_ref[pl.ds(r0, tq), :] = alpha * acc_ref[pl.ds(r0, tq), :]
            p_ref[pl.ds(r0, tq), :] = p.astype(BF16)
            return carry

        lax.fori_loop(0, MLA_HEADS, head, 0)
        acc_ref[...] += _dot(p_ref[...], kc[:, :KV_LORA])

    def full_step(j, carry):
        kv_step(j, False)
        return carry

    lax.fori_loop(0, n_kv - 1, full_step, 0)
    kv_step(n_kv - 1, True)

    for m in range(MLA_HEADS // 2):
        o = None
        for half in range(2):
            h = 2 * m + half
            ol = (acc_ref[h * tq:(h + 1) * tq, :] * (1.0 / l_ref[h * tq:(h + 1) * tq, :])).astype(BF16)
            oh = _dot(ol, wuv_ref[h])
            o = oh if o is None else o + oh
        o_ref[:, C_WIDTH + LANE * m:C_WIDTH + LANE * (m + 1)] = o.astype(BF16)


def _mix_odd_prompt(ci, qn, qp, kc, wuk_ext, wuv_ext, conv_w, nb, s):
    tq, kb = TQ_ODD, KB_ODD
    nq = s // tq
    blk = lambda b, i: (b * nq + i, 0)
    halo = lambda b, i: (jnp.maximum((b * nq + i) * (tq // 16) - 1, 0), 0)
    rows = MLA_HEADS * tq
    n_out = C_WIDTH + MLA_HEADS * V_DIM
    return pl.pallas_call(
        functools.partial(_odd_prompt_kernel, tq=tq, kb=kb),
        grid=(nb, nq),
        in_specs=[pl.BlockSpec((tq, 3 * C_WIDTH), blk), pl.BlockSpec((16, 3 * C_WIDTH), halo),
                  pl.BlockSpec((tq, MLA_HEADS * NOPE_DIM), blk), pl.BlockSpec((tq, MLA_HEADS * ROPE_DIM), blk),
                  pl.BlockSpec((s, KC_W), lambda b, i: (b, 0)),
                  _const_spec(wuk_ext.shape), _const_spec(wuv_ext.shape), _const_spec(conv_w.shape)],
        out_specs=[pl.BlockSpec((tq, n_out), blk), pl.BlockSpec((1, 8, C_WIDTH), lambda b, i: (b, 0, 0))],
        out_shape=[jax.ShapeDtypeStruct((nb * s, n_out), BF16), jax.ShapeDtypeStruct((nb, 8, C_WIDTH), F32)],
        scratch_shapes=[pltpu.VMEM((rows, KC_W), BF16), pltpu.VMEM((rows, kb), F32), pltpu.VMEM((rows, kb), BF16),
                        pltpu.VMEM((rows, 1), F32), pltpu.VMEM((rows, 1), F32), pltpu.VMEM((rows, KV_LORA), F32),
                        pltpu.VMEM((tq + 16, C_WIDTH), F32)],
        compiler_params=_cp(("arbitrary", "arbitrary")),
        name="mix_odd_prompt",
    )(ci, ci, qn, qp, kc, wuk_ext, wuv_ext, conv_w)


def _qx_sample_kernel(qn_ref, qp_ref, wuk_ref, qx_ref, *, rows):
    _build_qx(qn_ref, qp_ref, wuk_ref, qx_ref, rows)


def _qx_sample(qn, qp, wuk_ext, n_prompt_rows, rows):
    off = n_prompt_rows // rows
    return pl.pallas_call(
        functools.partial(_qx_sample_kernel, rows=rows),
        grid=(1,),
        in_specs=[pl.BlockSpec((rows, MLA_HEADS * NOPE_DIM), lambda i: (off, 0)),
                  pl.BlockSpec((rows, MLA_HEADS * ROPE_DIM), lambda i: (off, 0)), _const_spec(wuk_ext.shape)],
        out_specs=pl.BlockSpec((MLA_HEADS * rows, KC_W), lambda i: (0, 0)),
        out_shape=jax.ShapeDtypeStruct((MLA_HEADS * rows, KC_W), BF16),
        compiler_params=_cp(("arbitrary",)),
        name="qx_sample",
    )(qn, qp, wuk_ext)


def _page_copies(pt_ref, pool_ckv, pool_kpe, ckv_buf, kpe_buf, sem, b, slot, n_pages):
    copies = []
    for p in range(n_pages):
        page = pt_ref[b, p]
        rows = pl.ds(p * PAGE_SIZE, PAGE_SIZE)
        copies.append(pltpu.make_async_copy(pool_ckv.at[page], ckv_buf.at[slot, rows], sem.at[0, slot]))
        copies.append(pltpu.make_async_copy(pool_kpe.at[page], kpe_buf.at[slot, rows], sem.at[1, slot]))
    return copies


def _mla_sample_kernel(pt_ref, qx_ref, kn_ref, pool_ckv, pool_kpe, o_ref, ckv_buf, kpe_buf, sem, *, ts, n_pages):
    b = pl.program_id(0)
    nb = pl.num_programs(0)
    slot = b % 2

    @pl.when(b == 0)
    def _():
        for c in _page_copies(pt_ref, pool_ckv, pool_kpe, ckv_buf, kpe_buf, sem, 0, 0, n_pages):
            c.start()

    @pl.when(b + 1 < nb)
    def _():
        for c in _page_copies(pt_ref, pool_ckv, pool_kpe, ckv_buf, kpe_buf, sem, b + 1, 1 - slot, n_pages):
            c.start()

    for c in _page_copies(pt_ref, pool_ckv, pool_kpe, ckv_buf, kpe_buf, sem, b, slot, n_pages):
        c.wait()

    qx = qx_ref[0]
    kc = ckv_buf[slot].astype(BF16)
    kp = kpe_buf[slot].astype(BF16)
    s = _dot_nt(qx[:, :KV_LORA], kc) + _dot_nt(qx[:, KV_LORA:KV_LORA + ROPE_DIM], kp)
    kn = kn_ref[0].astype(BF16)
    sn = _dot_nt(qx, kn)
    rows = qx.shape[0]
    t_q = lax.broadcasted_iota(jnp.int32, (rows, 8), 0) % ts
    t_k = lax.broadcasted_iota(jnp.int32, (rows, 8), 1)
    sn = jnp.where(t_k <= t_q, sn, NEG)
    m = jnp.maximum(jnp.max(s, axis=-1, keepdims=True), jnp.max(sn, axis=-1, keepdims=True))
    p = jnp.exp(s - m)
    pn = jnp.exp(sn - m)
    den = jnp.sum(p, axis=-1, keepdims=True) + jnp.sum(pn, axis=-1, keepdims=True)
    o = _dot(p.astype(BF16), kc) + _dot(pn.astype(BF16), kn[:, :KV_LORA])
    o_ref[0] = o * (1.0 / den)


def _mla_sample(page_table, qx_b, kn_b, pool_ckv, pool_kpe, ts):
    bd, n_pages = page_table.shape
    past = n_pages * PAGE_SIZE
    rows = MLA_HEADS * ts
    grid_spec = pltpu.PrefetchScalarGridSpec(
        num_scalar_prefetch=1,
        grid=(bd,),
        in_specs=[pl.BlockSpec((1, rows, KC_W), lambda b, pt: (b, 0, 0)),
                  pl.BlockSpec((1, 8, KC_W), lambda b, pt: (b, 0, 0)),
                  pl.BlockSpec(memory_space=pl.ANY), pl.BlockSpec(memory_space=pl.ANY)],
        out_specs=pl.BlockSpec((1, rows, KV_LORA), lambda b, pt: (b, 0, 0)),
        scratch_shapes=[pltpu.VMEM((2, past, KV_LORA), F32), pltpu.VMEM((2, past, ROPE_DIM), F32),
                        pltpu.SemaphoreType.DMA((2, 2))],
    )
    return pl.pallas_call(
        functools.partial(_mla_sample_kernel, ts=ts, n_pages=n_pages),
        grid_spec=grid_spec,
        out_shape=jax.ShapeDtypeStruct((bd, rows, KV_LORA), F32),
        compiler_params=_cp(("arbitrary",)),
        name="mla_sample",
    )(page_table, qx_b, kn_b, pool_ckv, pool_kpe)


def _odd_sample_tail_kernel(ci_ref, st_ref, cw_ref, ol_ref, wuv_ref, o_ref, nst_ref, *, ts, bd):
    cw = cw_ref[...]
    zz = [st_ref[0], st_ref[1]]
    for t in range(ts):
        r = slice(bd * t, bd * (t + 1))
        zz.append(ci_ref[r, C_WIDTH:2 * C_WIDTH].astype(F32) * ci_ref[r, 2 * C_WIDTH:].astype(F32))
    for t in range(ts):
        r = slice(bd * t, bd * (t + 1))
        y = cw[0:1] * zz[t] + cw[1:2] * zz[t + 1] + cw[2:3] * zz[t + 2]
        o_ref[r, :C_WIDTH] = (ci_ref[r, :C_WIDTH].astype(F32) * y).astype(BF16)
    nst_ref[0] = zz[ts]
    nst_ref[1] = zz[ts + 1]
    for m in range(MLA_HEADS // 2):
        o = _dot(ol_ref[2 * m], wuv_ref[2 * m]) + _dot(ol_ref[2 * m + 1], wuv_ref[2 * m + 1])
        o_ref[:, C_WIDTH + LANE * m:C_WIDTH + LANE * (m + 1)] = o.astype(BF16)


def _odd_sample_tail(ci, state_t, conv_w, o_lat_t, wuv_ext, n_prompt_rows, ts, bd):
    rows = ts * bd
    off = n_prompt_rows // rows
    n_out = C_WIDTH + MLA_HEADS * V_DIM
    return pl.pallas_call(
        functools.partial(_odd_sample_tail_kernel, ts=ts, bd=bd),
        grid=(1,),
        in_specs=[pl.BlockSpec((rows, 3 * C_WIDTH), lambda i: (off, 0)), _const_spec(state_t.shape),
                  _const_spec(conv_w.shape), _const_spec(o_lat_t.shape), _const_spec(wuv_ext.shape)],
        out_specs=[pl.BlockSpec((rows, n_out), lambda i: (0, 0)),
                   pl.BlockSpec(state_t.shape, lambda i: (0, 0, 0))],
        out_shape=[jax.ShapeDtypeStruct((rows, n_out), BF16), jax.ShapeDtypeStruct(state_t.shape, F32)],
        compiler_params=_cp(("arbitrary",)),
        name="odd_sample_tail",
    )(ci, state_t, conv_w, o_lat_t, wuv_ext)


def _prep_odd_weights(w_in, w_uq, w_uk, w_uv):
    d, n = w_in.shape
    n_pad = -n % LANE
    w_in_p = jnp.pad(w_in, ((0, 0), (0, n_pad))).astype(BF16)
    per = NOPE_DIM + ROPE_DIM
    wq = w_uq.reshape(Q_LORA, MLA_HEADS, per)
    w_uq_r = jnp.concatenate([wq[:, :, :NOPE_DIM].reshape(Q_LORA, -1), wq[:, :, NOPE_DIM:].reshape(Q_LORA, -1)],
                             axis=1).astype(BF16)
    ukt = jnp.transpose(w_uk, (1, 2, 0))
    uvt = jnp.transpose(w_uv, (1, 0, 2))
    wuk_ext, wuv_ext = [], []
    for h in range(MLA_HEADS):
        lo = (h % 2) * NOPE_DIM
        wuk_ext.append(jnp.pad(ukt[h], ((lo, LANE - NOPE_DIM - lo), (0, 0))))
        wuv_ext.append(jnp.pad(uvt[h], ((0, 0), (lo, LANE - V_DIM - lo))))
    return w_in_p, w_uq_r, jnp.stack(wuk_ext).astype(BF16), jnp.stack(wuv_ext).astype(BF16)


def kernel(x_prompt, x_sample, cache_swa_k, cache_swa_v, state_conv, cache_mla_ckv, cache_mla_kpe, page_table,
           rel_bias, ffn1_norm, ffn1_w_gate, ffn1_w_up, ffn1_w_down, mix_norm, ffn2_norm, ffn2_w_gate,
           ffn2_w_up, ffn2_w_down, even_w_in, even_w_out, attn_sinks, sgu_ln_g, sgu_ln_b, sgu_w, sgu_b,
           odd_w_in, odd_w_out, conv_w, mla_q_norm, mla_w_uq, mla_kv_norm, mla_w_uk, mla_w_uv, final_norm):
    nb, s, d = x_prompt.shape
    bd, ts, _ = x_sample.shape
    depth = ffn1_norm.shape[0]
    n_p, n_s = nb * s, bd * ts
    past = page_table.shape[1] * PAGE_SIZE
    wb = cache_swa_k.shape[2]
    assert n_s == TM and n_p % TM == 0 and s % TM == 0, "sample group must fill exactly one token tile"
    assert 8 % ts == 0 and bd % BB_EVEN == 0 and bd % 8 == 0 and wb == WINDOW
    assert s % KB_ODD == 0 and s % TQ_EVEN == 0 and KB_ODD % TQ_ODD == 0
    n_pt = n_p // TM

    def to_b_major(a):
        return a.reshape(ts, bd, -1).transpose(1, 0, 2)

    x = jnp.concatenate([x_prompt.reshape(n_p, d), x_sample.transpose(1, 0, 2).reshape(n_s, d)], axis=0)
    bias_p, bias_s = _bias_tables(rel_bias, ts)
    cos_np, sin_np = _rope_tables_np(s, past, ts, bd)
    cos_t, sin_t = jnp.asarray(cos_np), jnp.asarray(sin_np)

    outs = {k: [] for k in ("kp", "vp", "ks", "vs", "sv", "cp", "cs", "ckp", "kpp", "cks", "kps")}
    for l in range(depth):
        wgu, wd = _prep_ffn_weights(ffn1_w_gate[l], ffn1_w_up[l], ffn1_w_down[l])
        x = _ffn(x, ffn1_norm[l], wgu, wd, n_pt)
        if l % 2 == 0:
            e = l // 2
            q, kv, ug = _inproj_even(x, mix_norm[l], even_w_in[e].astype(BF16))
            mix_p = _mix_even_prompt(q, kv, ug, bias_p, attn_sinks[e], sgu_ln_g[e], sgu_ln_b[e], sgu_w[e],
                                     sgu_b[e], nb, s)
            q_b = to_b_major(q[n_p:].astype(F32)).reshape(n_s, A_Q_W)
            kvn_b = to_b_major(kv[n_p:]).reshape(n_s, 2 * A_KV_W)
            a_b, new_k, new_v = _mix_even_sample(q_b, kvn_b, cache_swa_k[e].reshape(bd, wb, A_KV_W),
                                                 cache_swa_v[e].reshape(bd, wb, A_KV_W), bias_s, attn_sinks[e], ts)
            a_t = a_b.reshape(bd, ts, A_Q_W).transpose(1, 0, 2).reshape(n_s, A_Q_W).astype(BF16)
            sg_s, v_rows = _sgu_sample(ug, sgu_ln_g[e], sgu_ln_b[e], sgu_w[e], sgu_b[e], n_p, ts, bd)
            mix_s = jnp.concatenate([a_t, sg_s], axis=1)
            w_out = even_w_out[e].astype(BF16)
            kv_p = kv[:n_p].reshape(nb, s, 2 * A_KV_W)[:, s - wb:]
            outs["kp"].append(kv_p[..., :A_KV_W].reshape(nb, wb, A_KV_HEADS, HEAD_DIM))
            outs["vp"].append(kv_p[..., A_KV_W:].reshape(nb, wb, A_KV_HEADS, HEAD_DIM))
            outs["ks"].append(new_k.reshape(bd, wb, A_KV_HEADS, HEAD_DIM))
            outs["vs"].append(new_v.reshape(bd, wb, A_KV_HEADS, HEAD_DIM))
            outs["sv"].append(to_b_major(v_rows))
        else:
            j = l // 2
            w_in_p, w_uq_r, wuk_ext, wuv_ext = _prep_odd_weights(odd_w_in[j], mla_w_uq[j], mla_w_uk[j], mla_w_uv[j])
            ci, qn, qp, kc, ckv, kpe = _inproj_odd(x, mix_norm[l], w_in_p, mla_q_norm[j], w_uq_r, mla_kv_norm[j],
                                                   cos_t, sin_t, n_pt, s)
            mix_p, cst_p = _mix_odd_prompt(ci, qn, qp, kc, wuk_ext, wuv_ext, conv_w[j], nb, s)
            qx = _qx_sample(qn, qp, wuk_ext, n_p, n_s)
            qx_b = qx.reshape(MLA_HEADS * ts, bd, KC_W).transpose(1, 0, 2)
            kn_b = jnp.pad(to_b_major(kc[n_p:].astype(F32)), ((0, 0), (0, 8 - ts), (0, 0)))
            o_lat = _mla_sample(page_table, qx_b, kn_b, cache_mla_ckv[j], cache_mla_kpe[j], ts)
            o_lat_t = o_lat.reshape(bd, MLA_HEADS, ts, KV_LORA).transpose(1, 2, 0, 3).reshape(
                MLA_HEADS, n_s, KV_LORA).astype(BF16)
            mix_s, cst_s = _odd_sample_tail(ci, state_conv[j].transpose(1, 0, 2), conv_w[j], o_lat_t, wuv_ext,
                                            n_p, ts, bd)
            w_out = odd_w_out[j].astype(BF16)
            outs["cp"].append(cst_p[:, 8 - (CONV_WIDTH - 1):])
            outs["cs"].append(cst_s.transpose(1, 0, 2))
            outs["ckp"].append(ckv[:n_p].reshape(nb, s, KV_LORA))
            outs["kpp"].append(kpe[:n_p, :ROPE_DIM].reshape(nb, s, ROPE_DIM))
            outs["cks"].append(to_b_major(ckv[n_p:]))
            outs["kps"].append(to_b_major(kpe[n_p:, :ROPE_DIM]))
        wgu, wd = _prep_ffn_weights(ffn2_w_gate[l], ffn2_w_up[l], ffn2_w_down[l])
        x = _ffn(x, ffn2_norm[l], wgu, wd, n_pt, mix=(mix_p, mix_s, w_out),
                 final_g=final_norm if l == depth - 1 else None)
    y_prompt = x[:n_p].reshape(nb, s, d)
    y_sample = x[n_p:].reshape(ts, bd, d).transpose(1, 0, 2)
    st = lambda k: jnp.stack(outs[k])
    return (y_prompt, y_sample, st("kp"), st("vp"), st("ks"), st("vs"), st("sv"), st("cp"), st("cs"),
            st("ckp"), st("kpp"), st("cks"), st("kps"))
```

```python
import functools
import math

import numpy as np
import jax
import jax.numpy as jnp
from jax import lax
from jax.experimental import pallas as pl
from jax.experimental.pallas import tpu as pltpu

F32 = jnp.float32
BF16 = jnp.bfloat16

HEAD_DIM = 64
A_HEADS = 8
A_KV_HEADS = 2
A_Q_W = A_HEADS * HEAD_DIM
A_KV_W = A_KV_HEADS * HEAD_DIM
WINDOW = 128
N_BUCKETS = 32
MAX_DISTANCE = 128
SG_GROUPS = 4
SG_GROUP_DIM = 128
SG_WIDTH = SG_GROUPS * SG_GROUP_DIM
CHUNK = 128
C_WIDTH = 512
CONV_WIDTH = 3
MLA_HEADS = 8
NOPE_DIM = 64
ROPE_DIM = 32
V_DIM = 64
Q_LORA = 768
KV_LORA = 256
ROPE_THETA = 10000.0
PAGE_SIZE = 128
EPS = 1e-6
NEG = -1e30

LANE = 128
KC_W = KV_LORA + LANE
TM = 512
FC = 512
TQ_EVEN = 512
TQ_ODD = 256
KB_ODD = 512
BB_EVEN = 16
MLA_SAMPLE_CHUNK = 2048
VMEM_LIMIT = 56 * 1024 * 1024


def _cp(sem, vmem=VMEM_LIMIT):
    return pltpu.CompilerParams(dimension_semantics=sem, vmem_limit_bytes=vmem)


def _dot(a, b):
    return jnp.dot(a, b, preferred_element_type=F32)


def _dot_nt(a, b):
    return lax.dot_general(a, b, (((1,), (1,)), ((), ())), preferred_element_type=F32)


def _rms(x, g):
    ms = jnp.mean(x * x, axis=-1, keepdims=True)
    return x * lax.rsqrt(ms + EPS) * g


def _gelu(x):
    return 0.5 * x * (1.0 + lax.erf(x * np.float32(math.sqrt(0.5))))


def _const_spec(shape):
    n = len(shape)
    return pl.BlockSpec(shape, lambda *_: (0,) * n, pipeline_mode=pl.Buffered(1))


def _smem_spec():
    return pl.BlockSpec(memory_space=pltpu.SMEM)


def _ffn_chunks(f):
    chunks = [(c, FC) for c in range(0, f - f % FC, FC)]
    if f % FC:
        chunks.append((f - f % FC, f % FC))
    return chunks


def _ffn_kernel(*refs, has_mix, has_final, n_prompt_tiles, chunks):
    it = iter(refs)
    x_ref = next(it)
    if has_mix:
        mixp_ref, mixs_ref, wout_ref = next(it), next(it), next(it)
    g_ref, wg_ref, wu_ref, wd_ref = next(it), next(it), next(it), next(it)
    if has_final:
        fg_ref = next(it)
    o_ref = next(it)
    acc_ref = next(it)

    x = x_ref[...]
    if has_mix:
        is_sample = pl.program_id(0) >= n_prompt_tiles
        mix = jnp.where(is_sample, mixs_ref[...], mixp_ref[...])
        x = x + _dot(mix, wout_ref[...])
    h = _rms(x, g_ref[...]).astype(BF16)
    for k, (c0, cw) in enumerate(chunks):
        g = _dot(h, wg_ref[:, c0:c0 + cw])
        u = _dot(h, wu_ref[:, c0:c0 + cw])
        a = (g * jax.nn.sigmoid(g) * u).astype(BF16)
        d = _dot(a, wd_ref[c0:c0 + cw, :])
        if k == 0:
            acc_ref[...] = d
        else:
            acc_ref[...] += d
    y = x + 0.5 * acc_ref[...]
    if has_final:
        y = _rms(y, fg_ref[...])
    o_ref[...] = y


def _layer_spec(shape, layer):
    n = len(shape)
    return pl.BlockSpec((None,) + tuple(shape), lambda *_: (layer,) + (0,) * n, pipeline_mode=pl.Buffered(1))


def _ffn(x, norm_g, w_gate, w_up, w_down, layer, n_prompt_tiles, mix=None, final_g=None):
    t, d = x.shape
    f = w_down.shape[1]
    n_tiles = t // TM
    row = lambda i: (i, 0)
    args, specs = [x], [pl.BlockSpec((TM, d), row)]
    if mix is not None:
        mix_p, mix_s, w_out, e = mix
        last_p = n_prompt_tiles - 1
        dm = w_out.shape[1]
        args += [mix_p, mix_s, w_out]
        specs += [pl.BlockSpec((TM, dm), lambda i: (jnp.minimum(i, last_p), 0)),
                  pl.BlockSpec((TM, dm), lambda i: (0, 0)),
                  _layer_spec((dm, d), e)]
    args += [norm_g.reshape(1, d), w_gate, w_up, w_down]
    specs += [_const_spec((1, d)), _layer_spec((d, f), layer), _layer_spec((d, f), layer),
              _layer_spec((f, d), layer)]
    if final_g is not None:
        args.append(final_g.reshape(1, d))
        specs.append(_const_spec((1, d)))
    body = functools.partial(_ffn_kernel, has_mix=mix is not None, has_final=final_g is not None,
                             n_prompt_tiles=n_prompt_tiles, chunks=_ffn_chunks(f))
    return pl.pallas_call(
        body,
        grid=(n_tiles,),
        in_specs=specs,
        out_specs=pl.BlockSpec((TM, d), row),
        out_shape=jax.ShapeDtypeStruct((t, d), F32),
        scratch_shapes=[pltpu.VMEM((TM, d), F32)],
        compiler_params=_cp(("arbitrary",)),
        name="ffn",
    )(*args)


def _t5_bucket_np(dist):
    n = np.maximum(dist, 0)
    max_exact = N_BUCKETS // 2
    nf = np.maximum(n, 1).astype(np.float32)
    large = max_exact + (np.log(nf / np.float32(max_exact)) / np.float32(math.log(MAX_DISTANCE / max_exact))
                         * np.float32(N_BUCKETS - max_exact)).astype(np.int32)
    return np.where(n < max_exact, n, np.minimum(large, N_BUCKETS - 1)).astype(np.int32)


def _bias_tables_np(ts):
    q = np.arange(WINDOW)[:, None]
    j = np.arange(2 * WINDOW)[None, :]
    bucket_p = _t5_bucket_np(q - j + WINDOW)
    nb = 8 // ts
    cols = nb * WINDOW + 16
    bucket_s = -np.ones((8, cols), np.int32)
    for r in range(8):
        bi, t = divmod(r, ts)
        for c in range(cols):
            if c < nb * WINDOW:
                bj, jj = divmod(c, WINDOW)
                dist = t - (jj - WINDOW)
            elif c < nb * WINDOW + 8:
                bj, tt = divmod(c - nb * WINDOW, ts)
                dist = t - tt
            else:
                continue
            if bj == bi and 0 <= dist <= WINDOW:
                bucket_s[r, c] = _t5_bucket_np(np.array(dist))
    return bucket_p, bucket_s


def _bias_kernel(rb_ref, bp_ref, bs_ref, op_ref, os_ref):
    h = pl.program_id(0)
    bp = bp_ref[...]
    bs = bs_ref[...]
    accp = jnp.zeros(bp.shape, F32)
    accs = jnp.full(bs.shape, NEG, F32)
    for k in range(N_BUCKETS):
        val = rb_ref[k, h]
        accp = jnp.where(bp == k, val, accp)
        accs = jnp.where(bs == k, val, accs)
    op_ref[0] = accp
    os_ref[0] = accs


def _bias_tables(rel_bias, ts):
    bucket_p, bucket_s = _bias_tables_np(ts)
    return pl.pallas_call(
        _bias_kernel,
        grid=(A_HEADS,),
        in_specs=[_smem_spec(),
                  pl.BlockSpec(bucket_p.shape, lambda h: (0, 0)),
                  pl.BlockSpec(bucket_s.shape, lambda h: (0, 0))],
        out_specs=[pl.BlockSpec((1,) + bucket_p.shape, lambda h: (h, 0, 0)),
                   pl.BlockSpec((1,) + bucket_s.shape, lambda h: (h, 0, 0))],
        out_shape=[jax.ShapeDtypeStruct((A_HEADS,) + bucket_p.shape, F32),
                   jax.ShapeDtypeStruct((A_HEADS,) + bucket_s.shape, F32)],
        compiler_params=_cp(("arbitrary",)),
        name="t5_bias",
    )(rel_bias, jnp.asarray(bucket_p), jnp.asarray(bucket_s))


def _inproj_even_kernel(x_ref, g_ref, w_ref, q_ref, kv_ref, ug_ref):
    h = _rms(x_ref[...], g_ref[...]).astype(BF16)
    p = _dot(h, w_ref[...])
    q_ref[...] = (p[:, :A_Q_W] * np.float32(HEAD_DIM ** -0.5)).astype(BF16)
    kv_ref[...] = p[:, A_Q_W:A_Q_W + 2 * A_KV_W]
    ug_ref[...] = p[:, A_Q_W + 2 * A_KV_W:]


def _inproj_even(x, norm_g, w_in):
    t, d = x.shape
    n = w_in.shape[1]
    row = lambda i: (i, 0)
    return pl.pallas_call(
        _inproj_even_kernel,
        grid=(t // TM,),
        in_specs=[pl.BlockSpec((TM, d), row), _const_spec((1, d)), _const_spec((d, n))],
        out_specs=[pl.BlockSpec((TM, A_Q_W), row), pl.BlockSpec((TM, 2 * A_KV_W), row),
                   pl.BlockSpec((TM, 2 * SG_WIDTH), row)],
        out_shape=[jax.ShapeDtypeStruct((t, A_Q_W), BF16), jax.ShapeDtypeStruct((t, 2 * A_KV_W), F32),
                   jax.ShapeDtypeStruct((t, 2 * SG_WIDTH), F32)],
        compiler_params=_cp(("arbitrary",)),
        name="inproj_even",
    )(x, norm_g.reshape(1, d), w_in)


def _head_variants(x):
    lo = lax.broadcasted_iota(jnp.int32, x.shape, 1) < HEAD_DIM
    xr = pltpu.roll(x, HEAD_DIM, 1)
    z = jnp.zeros_like(x)
    return [[jnp.where(lo, x, z).astype(BF16), jnp.where(lo, z, xr).astype(BF16)],
            [jnp.where(lo, xr, z).astype(BF16), jnp.where(lo, z, x).astype(BF16)]]


def _sink_softmax_pv(s, sink, v):
    mx = jnp.maximum(jnp.max(s, axis=-1, keepdims=True), sink)
    p = jnp.exp(s - mx)
    den = jnp.sum(p, axis=-1, keepdims=True) + jnp.exp(sink - mx)
    return _dot(p.astype(BF16), v) * (1.0 / den)


def _layer_norm(x, g, b):
    xc = x - jnp.mean(x, axis=-1, keepdims=True)
    y = xc * lax.rsqrt(jnp.mean(xc * xc, axis=-1, keepdims=True) + EPS)
    return y * g + b


def _even_prompt_kernel(q_ref, kv_ref, kvp_ref, ug_ref, bias_ref, sink_ref, lng_ref, lnb_ref, ws_ref,
                        bst_ref, o_ref, *, tq):
    nblk = tq // WINDOW
    step = pl.program_id(1)
    kv_all = jnp.concatenate([kvp_ref[...], kv_ref[...]], axis=0)
    kvar = _head_variants(kv_all[:, :A_KV_W])
    vvar = _head_variants(kv_all[:, A_KV_W:])
    qi = lax.broadcasted_iota(jnp.int32, (WINDOW, 2 * WINDOW), 0)
    kj = lax.broadcasted_iota(jnp.int32, (WINDOW, 2 * WINDOW), 1)
    dist = qi - kj + WINDOW
    band = (dist >= 0) & (dist <= WINDOW)
    cur = kj >= WINDOW
    ri = lax.broadcasted_iota(jnp.int32, (CHUNK, CHUNK), 0)
    ci = lax.broadcasted_iota(jnp.int32, (CHUNK, CHUNK), 1)
    wt = [jnp.where(ri >= ci, ws_ref[g], 0.0).astype(BF16) for g in range(SG_GROUPS)]
    lng, lnb = lng_ref[...], lnb_ref[...]
    for n in range(nblk):
        r0 = n * WINDOW
        has_prev = (step * nblk + n) > 0
        mask = band & (cur | has_prev)
        for m in range(A_HEADS // 2):
            kvh = (2 * m) // (A_HEADS // A_KV_HEADS)
            qp = q_ref[r0:r0 + WINDOW, LANE * m:LANE * (m + 1)]
            o = None
            for half in range(2):
                hd = 2 * m + half
                s = _dot_nt(qp, kvar[kvh][half][r0:r0 + 2 * WINDOW])
                s = jnp.where(mask, s + bias_ref[hd], NEG)
                oh = _sink_softmax_pv(s, sink_ref[hd], vvar[kvh][half][r0:r0 + 2 * WINDOW])
                o = oh if o is None else o + oh
            o_ref[r0:r0 + WINDOW, LANE * m:LANE * (m + 1)] = o.astype(BF16)
        u = _gelu(ug_ref[r0:r0 + WINDOW, :SG_WIDTH])
        v = _layer_norm(_gelu(ug_ref[r0:r0 + WINDOW, SG_WIDTH:]), lng, lnb)
        for g in range(SG_GROUPS):
            sl = slice(SG_GROUP_DIM * g, SG_GROUP_DIM * (g + 1))
            y = _dot(wt[g], v[:, sl].astype(BF16)) + bst_ref[:, g:g + 1]
            o_ref[r0:r0 + WINDOW, A_Q_W + SG_GROUP_DIM * g:A_Q_W + SG_GROUP_DIM * (g + 1)] = (
                u[:, sl] * y).astype(BF16)


def _mix_even_prompt(q, kv, ug, bias_p, sinks, ln_g, ln_b, w_s, b_s, nb, s):
    tq = TQ_EVEN
    nq = s // tq
    blk = lambda b, i: (b * nq + i, 0)
    prev = lambda b, i: (jnp.maximum((b * nq + i) * (tq // WINDOW) - 1, 0), 0)
    return pl.pallas_call(
        functools.partial(_even_prompt_kernel, tq=tq),
        grid=(nb, nq),
        in_specs=[pl.BlockSpec((tq, A_Q_W), blk), pl.BlockSpec((tq, 2 * A_KV_W), blk),
                  pl.BlockSpec((WINDOW, 2 * A_KV_W), prev), pl.BlockSpec((tq, 2 * SG_WIDTH), blk),
                  _const_spec(bias_p.shape), _smem_spec(), _const_spec((1, SG_WIDTH)),
                  _const_spec((1, SG_WIDTH)), _const_spec(w_s.shape), _const_spec((CHUNK, SG_GROUPS))],
        out_specs=pl.BlockSpec((tq, A_Q_W + SG_WIDTH), blk),
        out_shape=jax.ShapeDtypeStruct((nb * s, A_Q_W + SG_WIDTH), BF16),
        compiler_params=_cp(("arbitrary", "arbitrary")),
        name="mix_even_prompt",
    )(q, kv, kv, ug, bias_p, sinks, ln_g.reshape(1, -1), ln_b.reshape(1, -1), w_s, b_s.T)


def _even_sample_kernel(q_ref, kvn_ref, ck_ref, cv_ref, bias_ref, sink_ref, a_ref, nk_ref, nv_ref, *, ts):
    bb = ck_ref.shape[0]
    gpb = 8 // ts
    wb = ck_ref.shape[1]
    row = lax.broadcasted_iota(jnp.int32, (16, 1), 0)
    zpad = jnp.zeros((8, A_KV_W), F32)
    heads_per_kv = A_HEADS // A_KV_HEADS
    for g in range(bb // gpb):
        r0 = 8 * g
        qg = q_ref[r0:r0 + 8, :]
        kn = kvn_ref[r0:r0 + 8, :A_KV_W]
        vn = kvn_ref[r0:r0 + 8, A_KV_W:]
        kg = jnp.concatenate([ck_ref[gpb * g + i] for i in range(gpb)] + [kn, zpad], axis=0)
        vg = jnp.concatenate([cv_ref[gpb * g + i] for i in range(gpb)] + [vn, zpad], axis=0)
        kvar = _head_variants(kg)
        vvar = _head_variants(vg)
        for kvh in range(A_KV_HEADS):
            c0 = 2 * LANE * kvh
            qk = jnp.concatenate([qg[:, c0:c0 + LANE], qg[:, c0 + LANE:c0 + 2 * LANE]], axis=0).astype(BF16)
            o = None
            for half in range(2):
                h0 = heads_per_kv * kvh + half
                h1 = h0 + 2
                bm = jnp.concatenate([bias_ref[h0], bias_ref[h1]], axis=0)
                sink = jnp.where(row < 8, sink_ref[h0], sink_ref[h1])
                s = _dot_nt(qk, kvar[kvh][half])
                s = jnp.where(bm > 0.5 * NEG, s + bm, NEG)
                oh = _sink_softmax_pv(s, sink, vvar[kvh][half])
                o = oh if o is None else o + oh
            a_ref[r0:r0 + 8, c0:c0 + LANE] = o[:8]
            a_ref[r0:r0 + 8, c0 + LANE:c0 + 2 * LANE] = o[8:]
        for i in range(gpb):
            b = gpb * g + i
            nk_ref[b, 0:wb - ts, :] = ck_ref[b, ts:wb, :]
            nk_ref[b, wb - ts:wb, :] = kn[ts * i:ts * (i + 1)]
            nv_ref[b, 0:wb - ts, :] = cv_ref[b, ts:wb, :]
            nv_ref[b, wb - ts:wb, :] = vn[ts * i:ts * (i + 1)]


def _mix_even_sample(q_b, kvn_b, cache_k, cache_v, bias_s, sinks, ts):
    bd, wb, _ = cache_k.shape
    bb = BB_EVEN
    rows = bb * ts
    blk2 = lambda i: (i, 0)
    blk3 = lambda i: (i, 0, 0)
    return pl.pallas_call(
        functools.partial(_even_sample_kernel, ts=ts),
        grid=(bd // bb,),
        in_specs=[pl.BlockSpec((rows, A_Q_W), blk2), pl.BlockSpec((rows, 2 * A_KV_W), blk2),
                  pl.BlockSpec((bb, wb, A_KV_W), blk3), pl.BlockSpec((bb, wb, A_KV_W), blk3),
                  _const_spec(bias_s.shape), _smem_spec()],
        out_specs=[pl.BlockSpec((rows, A_Q_W), blk2), pl.BlockSpec((bb, wb, A_KV_W), blk3),
                   pl.BlockSpec((bb, wb, A_KV_W), blk3)],
        out_shape=[jax.ShapeDtypeStruct((bd * ts, A_Q_W), F32),
                   jax.ShapeDtypeStruct(cache_k.shape, F32), jax.ShapeDtypeStruct(cache_v.shape, F32)],
        compiler_params=_cp(("arbitrary",)),
        name="mix_even_sample",
    )(q_b, kvn_b, cache_k, cache_v, bias_s, sinks)


def _sgu_sample_kernel(ug_ref, lng_ref, lnb_ref, w_ref, b_ref, sg_ref, v_ref, *, ts, bd):
    u = _gelu(ug_ref[:, :SG_WIDTH])
    v = _layer_norm(_gelu(ug_ref[:, SG_WIDTH:]), lng_ref[...], lnb_ref[...])
    v_ref[...] = v
    for i in range(ts):
        for g in range(SG_GROUPS):
            sl = slice(SG_GROUP_DIM * g, SG_GROUP_DIM * (g + 1))
            y = jnp.full((bd, SG_GROUP_DIM), b_ref[g, i], F32)
            for j in range(i + 1):
                y = y + w_ref[g, i * ts + j] * v[bd * j:bd * (j + 1), sl]
            sg_ref[bd * i:bd * (i + 1), sl] = (u[bd * i:bd * (i + 1), sl] * y).astype(BF16)


def _sgu_sample(ug, ln_g, ln_b, w_s, b_s, n_prompt_rows, ts, bd):
    rows = ts * bd
    off = n_prompt_rows // rows
    w_small = w_s[:, :ts, :ts].reshape(SG_GROUPS, ts * ts)
    b_small = b_s[:, :ts]
    return pl.pallas_call(
        functools.partial(_sgu_sample_kernel, ts=ts, bd=bd),
        grid=(1,),
        in_specs=[pl.BlockSpec((rows, 2 * SG_WIDTH), lambda i: (off, 0)), _const_spec((1, SG_WIDTH)),
                  _const_spec((1, SG_WIDTH)), _smem_spec(), _smem_spec()],
        out_specs=[pl.BlockSpec((rows, SG_WIDTH), lambda i: (0, 0)),
                   pl.BlockSpec((rows, SG_WIDTH), lambda i: (0, 0))],
        out_shape=[jax.ShapeDtypeStruct((rows, SG_WIDTH), BF16), jax.ShapeDtypeStruct((rows, SG_WIDTH), F32)],
        compiler_params=_cp(("arbitrary",)),
        name="sgu_sample",
    )(ug, ln_g.reshape(1, -1), ln_b.reshape(1, -1), w_small, b_small)


def _rope_tables_np(s, past, ts, bd):
    half = ROPE_DIM // 2
    inv = np.exp(-math.log(ROPE_THETA) * np.arange(half, dtype=np.float32) / np.float32(half)).astype(np.float32)
    pos = np.concatenate([np.arange(s), past + np.repeat(np.arange(ts), bd)]).astype(np.float32)
    ang = (pos[:, None] * inv[None, :]).astype(np.float32).astype(np.float64)
    cos = np.cos(ang)
    sin = np.sin(ang)
    cos_t = np.tile(np.concatenate([cos, cos], axis=1), (1, MLA_HEADS)).astype(np.float32)
    sin_t = np.tile(np.concatenate([-sin, sin], axis=1), (1, MLA_HEADS)).astype(np.float32)
    return cos_t, sin_t


def _rope(x, cos_t, sin_t):
    n = x.shape[1]
    half = ROPE_DIM // 2
    first = (lax.broadcasted_iota(jnp.int32, x.shape, 1) % ROPE_DIM) < half
    partner = jnp.where(first, pltpu.roll(x, n - half, 1), pltpu.roll(x, half, 1))
    return x * cos_t + partner * sin_t


def _inproj_odd_kernel(x_ref, g_ref, w_ref, qg_ref, wuq_ref, kvg_ref, cos_ref, sin_ref,
                       ci_ref, qn_ref, qp_ref, kc_ref, ckv_ref, kpe_ref):
    h = _rms(x_ref[...], g_ref[...]).astype(BF16)
    p = _dot(h, w_ref[...])
    o_q = 3 * C_WIDTH
    o_kv = o_q + Q_LORA
    o_pe = o_kv + KV_LORA
    ci_ref[...] = p[:, :o_q].astype(BF16)
    scale = np.float32((NOPE_DIM + ROPE_DIM) ** -0.5)
    q = _dot(_rms(p[:, o_q:o_kv], qg_ref[...]).astype(BF16), wuq_ref[...])
    n_nope = MLA_HEADS * NOPE_DIM
    cos_t, sin_t = cos_ref[...], sin_ref[...]
    qn_ref[...] = (q[:, :n_nope] * scale).astype(BF16)
    qp_ref[...] = (_rope(q[:, n_nope:], cos_t, sin_t) * scale).astype(BF16)
    ckv = _rms(p[:, o_kv:o_pe], kvg_ref[...])
    kpe = _rope(p[:, o_pe:], cos_t[:, :LANE], sin_t[:, :LANE])
    ckv_ref[...] = ckv
    kpe_ref[...] = kpe
    kc_ref[:, :KV_LORA] = ckv.astype(BF16)
    kc_ref[:, KV_LORA:] = kpe.astype(BF16)


def _inproj_odd(x, norm_g, w_in, q_norm, w_uq, kv_norm, cos_t, sin_t, n_prompt_tiles, s):
    t, d = x.shape
    n = w_in.shape[1]
    row = lambda i: (i, 0)
    per_seq = s // TM
    tab = lambda i: (jnp.where(i < n_prompt_tiles, i % per_seq, per_seq), 0)
    n_pe = MLA_HEADS * ROPE_DIM
    widths = [(3 * C_WIDTH, BF16), (MLA_HEADS * NOPE_DIM, BF16), (n_pe, BF16), (KC_W, BF16),
              (KV_LORA, F32), (LANE, F32)]
    return pl.pallas_call(
        _inproj_odd_kernel,
        grid=(t // TM,),
        in_specs=[pl.BlockSpec((TM, d), row), _const_spec((1, d)), _const_spec((d, n)),
                  _const_spec((1, Q_LORA)), _const_spec(w_uq.shape), _const_spec((1, KV_LORA)),
                  pl.BlockSpec((TM, n_pe), tab), pl.BlockSpec((TM, n_pe), tab)],
        out_specs=[pl.BlockSpec((TM, w), row) for w, _ in widths],
        out_shape=[jax.ShapeDtypeStruct((t, w), dt) for w, dt in widths],
        compiler_params=_cp(("arbitrary",)),
        name="inproj_odd",
    )(x, norm_g.reshape(1, d), w_in, q_norm.reshape(1, -1), w_uq, kv_norm.reshape(1, -1), cos_t, sin_t)


def _build_qx(qn_ref, qp_ref, wuk_ref, qx_ref, rows):
    qp = qp_ref[...].astype(F32)
    keep = lax.broadcasted_iota(jnp.int32, (rows, LANE), 1) < ROPE_DIM
    n_pe = MLA_HEADS * ROPE_DIM
    for h in range(MLA_HEADS):
        pair = qn_ref[:, LANE * (h // 2):LANE * (h // 2 + 1)]
        qx_ref[h * rows:(h + 1) * rows, :KV_LORA] = _dot(pair, wuk_ref[h]).astype(BF16)
        pe = qp if h == 0 else pltpu.roll(qp, n_pe - ROPE_DIM * h, 1)
        qx_ref[h * rows:(h + 1) * rows, KV_LORA:] = jnp.where(keep, pe[:, :LANE], 0.0).astype(BF16)


def _odd_prompt_kernel(ci_ref, cih_ref, qn_ref, qp_ref, kc_ref, wuk_ref, wuv_ref, cw_ref,
                       o_ref, cst_ref, qx_ref, m_ref, l_ref, acc_ref, z_ref, *, tq, kb):
    step = pl.program_id(1)
    halo = 16
    zh = cih_ref[:, C_WIDTH:2 * C_WIDTH].astype(F32) * cih_ref[:, 2 * C_WIDTH:].astype(F32)
    z_ref[0:halo, :] = jnp.where(step > 0, zh, 0.0)
    z = ci_ref[:, C_WIDTH:2 * C_WIDTH].astype(F32) * ci_ref[:, 2 * C_WIDTH:].astype(F32)
    z_ref[halo:halo + tq, :] = z
    cw = cw_ref[...]
    y = cw[0:1] * z_ref[halo - 2:halo - 2 + tq, :] + cw[1:2] * z_ref[halo - 1:halo - 1 + tq, :] + cw[2:3] * z
    o_ref[:, :C_WIDTH] = (ci_ref[:, :C_WIDTH].astype(F32) * y).astype(BF16)
    cst_ref[0] = z[tq - 8:tq]

    _build_qx(qn_ref, qp_ref, wuk_ref, qx_ref, tq)
    rows = MLA_HEADS * tq
    m_ref[...] = jnp.full((rows, LANE), NEG, F32)
    l_ref[...] = jnp.zeros((rows, LANE), F32)
    acc_ref[...] = jnp.zeros((rows, KV_LORA), F32)
    n_kv = (step * tq + tq + kb - 1) // kb
    n_grp = kb // LANE
    hpc = MLA_HEADS // 2

    def kv_step(j, masked):
        k0 = pl.multiple_of(j * kb, kb)
        kc = kc_ref[pl.ds(k0, kb), :]
        if masked:
            qpos = step * tq + lax.broadcasted_iota(jnp.int32, (tq, LANE), 0)
            kpos = k0 + lax.broadcasted_iota(jnp.int32, (tq, LANE), 1)
            visible = [kpos + LANE * g <= qpos for g in range(n_grp)]
        for c in range(2):
            rc = slice(c * hpc * tq, (c + 1) * hpc * tq)
            s_all = _dot_nt(qx_ref[rc, :], kc)
            ps, alphas = [], []
            for hh in range(hpc):
                r = slice((c * hpc + hh) * tq, (c * hpc + hh + 1) * tq)
                sg = [s_all[hh * tq:(hh + 1) * tq, LANE * g:LANE * (g + 1)] for g in range(n_grp)]
                if masked:
                    sg = [jnp.where(visible[g], sg[g], NEG) for g in range(n_grp)]
                m_old = m_ref[r, :]
                m_new = jnp.maximum(m_old, jnp.max(functools.reduce(jnp.maximum, sg), axis=-1, keepdims=True))
                alpha = jnp.exp(m_old - m_new)
                pg = [jnp.exp(x - m_new) for x in sg]
                l_ref[r, :] = alpha * l_ref[r, :] + jnp.sum(functools.reduce(jnp.add, pg), axis=-1, keepdims=True)
                m_ref[r, :] = m_new
                ps.append(jnp.concatenate([x.astype(BF16) for x in pg], axis=1))
                alphas.append(jnp.concatenate([alpha] * (KV_LORA // LANE), axis=1))
            pv = _dot(jnp.concatenate(ps, axis=0), kc[:, :KV_LORA])
            acc_ref[rc, :] = jnp.concatenate(alphas, axis=0) * acc_ref[rc, :] + pv

    def full_step(j, carry):
        kv_step(j, False)
        return carry

    lax.fori_loop(0, n_kv - 1, full_step, 0)
    kv_step(n_kv - 1, True)

    for m in range(MLA_HEADS // 2):
        o = None
        for half in range(2):
            h = 2 * m + half
            inv_l = 1.0 / l_ref[h * tq:(h + 1) * tq, :]
            ol = (acc_ref[h * tq:(h + 1) * tq, :] * jnp.concatenate([inv_l] * (KV_LORA // LANE), axis=1)).astype(BF16)
            oh = _dot(ol, wuv_ref[h])
            o = oh if o is None else o + oh
        o_ref[:, C_WIDTH + LANE * m:C_WIDTH + LANE * (m + 1)] = o.astype(BF16)


def _mix_odd_prompt(ci, qn, qp, kc, wuk_ext, wuv_ext, conv_w, nb, s):
    tq, kb = TQ_ODD, KB_ODD
    nq = s // tq
    blk = lambda b, i: (b * nq + i, 0)
    halo = lambda b, i: (jnp.maximum((b * nq + i) * (tq // 16) - 1, 0), 0)
    rows = MLA_HEADS * tq
    n_out = C_WIDTH + MLA_HEADS * V_DIM
    return pl.pallas_call(
        functools.partial(_odd_prompt_kernel, tq=tq, kb=kb),
        grid=(nb, nq),
        in_specs=[pl.BlockSpec((tq, 3 * C_WIDTH), blk), pl.BlockSpec((16, 3 * C_WIDTH), halo),
                  pl.BlockSpec((tq, MLA_HEADS * NOPE_DIM), blk), pl.BlockSpec((tq, MLA_HEADS * ROPE_DIM), blk),
                  pl.BlockSpec((s, KC_W), lambda b, i: (b, 0)),
                  _const_spec(wuk_ext.shape), _const_spec(wuv_ext.shape), _const_spec(conv_w.shape)],
        out_specs=[pl.BlockSpec((tq, n_out), blk), pl.BlockSpec((1, 8, C_WIDTH), lambda b, i: (b, 0, 0))],
        out_shape=[jax.ShapeDtypeStruct((nb * s, n_out), BF16), jax.ShapeDtypeStruct((nb, 8, C_WIDTH), F32)],
        scratch_shapes=[pltpu.VMEM((rows, KC_W), BF16),
                        pltpu.VMEM((rows, LANE), F32), pltpu.VMEM((rows, LANE), F32), pltpu.VMEM((rows, KV_LORA), F32),
                        pltpu.VMEM((tq + 16, C_WIDTH), F32)],
        compiler_params=_cp(("arbitrary", "arbitrary")),
        name="mix_odd_prompt",
    )(ci, ci, qn, qp, kc, wuk_ext, wuv_ext, conv_w)


def _qx_sample_kernel(qn_ref, qp_ref, wuk_ref, qx_ref, *, rows):
    _build_qx(qn_ref, qp_ref, wuk_ref, qx_ref, rows)


def _qx_sample(qn, qp, wuk_ext, n_prompt_rows, rows):
    off = n_prompt_rows // rows
    return pl.pallas_call(
        functools.partial(_qx_sample_kernel, rows=rows),
        grid=(1,),
        in_specs=[pl.BlockSpec((rows, MLA_HEADS * NOPE_DIM), lambda i: (off, 0)),
                  pl.BlockSpec((rows, MLA_HEADS * ROPE_DIM), lambda i: (off, 0)), _const_spec(wuk_ext.shape)],
        out_specs=pl.BlockSpec((MLA_HEADS * rows, KC_W), lambda i: (0, 0)),
        out_shape=jax.ShapeDtypeStruct((MLA_HEADS * rows, KC_W), BF16),
        compiler_params=_cp(("arbitrary",)),
        name="qx_sample",
    )(qn, qp, wuk_ext)


def _page_copies(pt_ref, pool_ckv, pool_kpet, ckv_buf, kpe_buf, sem, layer, b, slot, n_pages):
    copies = []
    for p in range(n_pages):
        page = pt_ref[b, p]
        keys = pl.ds(p * PAGE_SIZE, PAGE_SIZE)
        copies.append(pltpu.make_async_copy(pool_ckv.at[layer, page], ckv_buf.at[slot, keys], sem.at[0, slot]))
        copies.append(pltpu.make_async_copy(pool_kpet.at[layer, page], kpe_buf.at[slot, :, keys], sem.at[1, slot]))
    return copies


def _mla_sample_kernel(pt_ref, qx_ref, kn_ref, pool_ckv, pool_kpet, o_ref, ckv_buf, kpe_buf, sem, *,
                       ts, n_pages, layer, chunk):
    b = pl.program_id(0)
    nb = pl.num_programs(0)
    slot = b % 2
    copies = functools.partial(_page_copies, pt_ref, pool_ckv, pool_kpet, ckv_buf, kpe_buf, sem, layer)

    @pl.when(b == 0)
    def _():
        for c in copies(0, 0, n_pages):
            c.start()

    @pl.when(b + 1 < nb)
    def _():
        for c in copies(b + 1, 1 - slot, n_pages):
            c.start()

    for c in copies(b, slot, n_pages):
        c.wait()

    qx = qx_ref[0]
    q_lat = qx[:, :KV_LORA]
    q_pe = qx[:, KV_LORA:KV_LORA + ROPE_DIM]
    rows = qx.shape[0]
    kn = kn_ref[0].astype(BF16)
    sn = _dot_nt(qx, kn)
    t_q = lax.broadcasted_iota(jnp.int32, (rows, 8), 0) % ts
    t_k = lax.broadcasted_iota(jnp.int32, (rows, 8), 1)
    sn = jnp.where(t_k <= t_q, sn, NEG)
    past = n_pages * PAGE_SIZE
    starts = range(0, past, chunk)
    kcs = [ckv_buf[slot, k0:k0 + chunk, :].astype(BF16) for k0 in starts]
    ss = [_dot_nt(q_lat, kc) + _dot(q_pe, kpe_buf[slot, :, k0:k0 + chunk].astype(BF16))
          for k0, kc in zip(starts, kcs)]
    m = functools.reduce(jnp.maximum, [jnp.max(s, axis=-1, keepdims=True) for s in ss],
                         jnp.max(sn, axis=-1, keepdims=True))
    pn = jnp.exp(sn - m)
    den = jnp.sum(pn, axis=-1, keepdims=True)
    acc = _dot(pn.astype(BF16), kn[:, :KV_LORA])
    for s, kc in zip(ss, kcs):
        p = jnp.exp(s - m)
        den = den + jnp.sum(p, axis=-1, keepdims=True)
        acc = acc + _dot(p.astype(BF16), kc)
    o_ref[0] = acc * (1.0 / den)


def _mla_sample(page_table, qx_b, kn_b, pool_ckv, pool_kpet, layer, ts):
    bd, n_pages = page_table.shape
    past = n_pages * PAGE_SIZE
    rows = MLA_HEADS * ts
    grid_spec = pltpu.PrefetchScalarGridSpec(
        num_scalar_prefetch=1,
        grid=(bd,),
        in_specs=[pl.BlockSpec((1, rows, KC_W), lambda b, pt: (b, 0, 0)),
                  pl.BlockSpec((1, 8, KC_W), lambda b, pt: (b, 0, 0)),
                  pl.BlockSpec(memory_space=pl.ANY), pl.BlockSpec(memory_space=pl.ANY)],
        out_specs=pl.BlockSpec((1, rows, KV_LORA), lambda b, pt: (b, 0, 0)),
        scratch_shapes=[pltpu.VMEM((2, past, KV_LORA), F32), pltpu.VMEM((2, ROPE_DIM, past), F32),
                        pltpu.SemaphoreType.DMA((2, 2))],
    )
    return pl.pallas_call(
        functools.partial(_mla_sample_kernel, ts=ts, n_pages=n_pages, layer=layer,
                          chunk=min(MLA_SAMPLE_CHUNK, past)),
        grid_spec=grid_spec,
        out_shape=jax.ShapeDtypeStruct((bd, rows, KV_LORA), F32),
        compiler_params=_cp(("arbitrary",)),
        name="mla_sample",
    )(page_table, qx_b, kn_b, pool_ckv, pool_kpet)


def _odd_sample_tail_kernel(ci_ref, st_ref, cw_ref, ol_ref, wuv_ref, o_ref, nst_ref, *, ts, bd):
    cw = cw_ref[...]
    zz = [st_ref[0], st_ref[1]]
    for t in range(ts):
        r = slice(bd * t, bd * (t + 1))
        zz.append(ci_ref[r, C_WIDTH:2 * C_WIDTH].astype(F32) * ci_ref[r, 2 * C_WIDTH:].astype(F32))
    for t in range(ts):
        r = slice(bd * t, bd * (t + 1))
        y = cw[0:1] * zz[t] + cw[1:2] * zz[t + 1] + cw[2:3] * zz[t + 2]
        o_ref[r, :C_WIDTH] = (ci_ref[r, :C_WIDTH].astype(F32) * y).astype(BF16)
    nst_ref[0] = zz[ts]
    nst_ref[1] = zz[ts + 1]
    for m in range(MLA_HEADS // 2):
        o = _dot(ol_ref[2 * m], wuv_ref[2 * m]) + _dot(ol_ref[2 * m + 1], wuv_ref[2 * m + 1])
        o_ref[:, C_WIDTH + LANE * m:C_WIDTH + LANE * (m + 1)] = o.astype(BF16)


def _odd_sample_tail(ci, state_t, conv_w, o_lat_t, wuv_ext, n_prompt_rows, ts, bd):
    rows = ts * bd
    off = n_prompt_rows // rows
    n_out = C_WIDTH + MLA_HEADS * V_DIM
    return pl.pallas_call(
        functools.partial(_odd_sample_tail_kernel, ts=ts, bd=bd),
        grid=(1,),
        in_specs=[pl.BlockSpec((rows, 3 * C_WIDTH), lambda i: (off, 0)), _const_spec(state_t.shape),
                  _const_spec(conv_w.shape), _const_spec(o_lat_t.shape), _const_spec(wuv_ext.shape)],
        out_specs=[pl.BlockSpec((rows, n_out), lambda i: (0, 0)),
                   pl.BlockSpec(state_t.shape, lambda i: (0, 0, 0))],
        out_shape=[jax.ShapeDtypeStruct((rows, n_out), BF16), jax.ShapeDtypeStruct(state_t.shape, F32)],
        compiler_params=_cp(("arbitrary",)),
        name="odd_sample_tail",
    )(ci, state_t, conv_w, o_lat_t, wuv_ext)


def _prep_odd_weights(w_in, w_uq, w_uk, w_uv):
    d, n = w_in.shape
    n_pad = -n % LANE
    w_in_p = jnp.pad(w_in, ((0, 0), (0, n_pad))).astype(BF16)
    per = NOPE_DIM + ROPE_DIM
    wq = w_uq.reshape(Q_LORA, MLA_HEADS, per)
    w_uq_r = jnp.concatenate([wq[:, :, :NOPE_DIM].reshape(Q_LORA, -1), wq[:, :, NOPE_DIM:].reshape(Q_LORA, -1)],
                             axis=1).astype(BF16)
    ukt = jnp.transpose(w_uk, (1, 2, 0))
    uvt = jnp.transpose(w_uv, (1, 0, 2))
    wuk_ext, wuv_ext = [], []
    for h in range(MLA_HEADS):
        lo = (h % 2) * NOPE_DIM
        wuk_ext.append(jnp.pad(ukt[h], ((lo, LANE - NOPE_DIM - lo), (0, 0))))
        wuv_ext.append(jnp.pad(uvt[h], ((0, 0), (lo, LANE - V_DIM - lo))))
    return w_in_p, w_uq_r, jnp.stack(wuk_ext).astype(BF16), jnp.stack(wuv_ext).astype(BF16)


def kernel(x_prompt, x_sample, cache_swa_k, cache_swa_v, state_conv, cache_mla_ckv, cache_mla_kpe, page_table,
           rel_bias, ffn1_norm, ffn1_w_gate, ffn1_w_up, ffn1_w_down, mix_norm, ffn2_norm, ffn2_w_gate,
           ffn2_w_up, ffn2_w_down, even_w_in, even_w_out, attn_sinks, sgu_ln_g, sgu_ln_b, sgu_w, sgu_b,
           odd_w_in, odd_w_out, conv_w, mla_q_norm, mla_w_uq, mla_kv_norm, mla_w_uk, mla_w_uv, final_norm):
    nb, s, d = x_prompt.shape
    bd, ts, _ = x_sample.shape
    depth = ffn1_norm.shape[0]
    n_p, n_s = nb * s, bd * ts
    past = page_table.shape[1] * PAGE_SIZE
    wb = cache_swa_k.shape[2]
    assert n_s == TM and n_p % TM == 0 and s % TM == 0, "sample group must fill exactly one token tile"
    assert 8 % ts == 0 and bd % BB_EVEN == 0 and bd % 8 == 0 and wb == WINDOW
    assert s % KB_ODD == 0 and s % TQ_EVEN == 0 and KB_ODD % TQ_ODD == 0
    n_pt = n_p // TM

    def to_b_major(a):
        return a.reshape(ts, bd, -1).transpose(1, 0, 2)

    x = jnp.concatenate([x_prompt.reshape(n_p, d), x_sample.transpose(1, 0, 2).reshape(n_s, d)], axis=0)
    bias_p, bias_s = _bias_tables(rel_bias, ts)
    cos_np, sin_np = _rope_tables_np(s, past, ts, bd)
    cos_t, sin_t = jnp.asarray(cos_np), jnp.asarray(sin_np)

    w1 = tuple(w.astype(BF16) for w in (ffn1_w_gate, ffn1_w_up, ffn1_w_down))
    w2 = tuple(w.astype(BF16) for w in (ffn2_w_gate, ffn2_w_up, ffn2_w_down))
    w_out_even, w_out_odd = even_w_out.astype(BF16), odd_w_out.astype(BF16)
    pool_kpet = jnp.swapaxes(cache_mla_kpe, 2, 3)

    outs = {k: [] for k in ("kp", "vp", "ks", "vs", "sv", "cp", "cs", "ckp", "kpp", "cks", "kps")}
    for l in range(depth):
        x = _ffn(x, ffn1_norm[l], *w1, l, n_pt)
        if l % 2 == 0:
            e = l // 2
            q, kv, ug = _inproj_even(x, mix_norm[l], even_w_in[e].astype(BF16))
            mix_p = _mix_even_prompt(q, kv, ug, bias_p, attn_sinks[e], sgu_ln_g[e], sgu_ln_b[e], sgu_w[e],
                                     sgu_b[e], nb, s)
            q_b = to_b_major(q[n_p:].astype(F32)).reshape(n_s, A_Q_W)
            kvn_b = to_b_major(kv[n_p:]).reshape(n_s, 2 * A_KV_W)
            a_b, new_k, new_v = _mix_even_sample(q_b, kvn_b, cache_swa_k[e].reshape(bd, wb, A_KV_W),
                                                 cache_swa_v[e].reshape(bd, wb, A_KV_W), bias_s, attn_sinks[e], ts)
            a_t = a_b.reshape(bd, ts, A_Q_W).transpose(1, 0, 2).reshape(n_s, A_Q_W).astype(BF16)
            sg_s, v_rows = _sgu_sample(ug, sgu_ln_g[e], sgu_ln_b[e], sgu_w[e], sgu_b[e], n_p, ts, bd)
            mix_s = jnp.concatenate([a_t, sg_s], axis=1)
            w_out = (w_out_even, e)
            kv_p = kv[:n_p].reshape(nb, s, 2 * A_KV_W)[:, s - wb:]
            outs["kp"].append(kv_p[..., :A_KV_W].reshape(nb, wb, A_KV_HEADS, HEAD_DIM))
            outs["vp"].append(kv_p[..., A_KV_W:].reshape(nb, wb, A_KV_HEADS, HEAD_DIM))
            outs["ks"].append(new_k.reshape(bd, wb, A_KV_HEADS, HEAD_DIM))
            outs["vs"].append(new_v.reshape(bd, wb, A_KV_HEADS, HEAD_DIM))
            outs["sv"].append(to_b_major(v_rows))
        else:
            j = l // 2
            w_in_p, w_uq_r, wuk_ext, wuv_ext = _prep_odd_weights(odd_w_in[j], mla_w_uq[j], mla_w_uk[j], mla_w_uv[j])
            ci, qn, qp, kc, ckv, kpe = _inproj_odd(x, mix_norm[l], w_in_p, mla_q_norm[j], w_uq_r, mla_kv_norm[j],
                                                   cos_t, sin_t, n_pt, s)
            mix_p, cst_p = _mix_odd_prompt(ci, qn, qp, kc, wuk_ext, wuv_ext, conv_w[j], nb, s)
            qx = _qx_sample(qn, qp, wuk_ext, n_p, n_s)
            qx_b = qx.reshape(MLA_HEADS * ts, bd, KC_W).transpose(1, 0, 2)
            kn_b = jnp.pad(to_b_major(kc[n_p:].astype(F32)), ((0, 0), (0, 8 - ts), (0, 0)))
            o_lat = _mla_sample(page_table, qx_b, kn_b, cache_mla_ckv, pool_kpet, j, ts)
            o_lat_t = o_lat.reshape(bd, MLA_HEADS, ts, KV_LORA).transpose(1, 2, 0, 3).reshape(
                MLA_HEADS, n_s, KV_LORA).astype(BF16)
            mix_s, cst_s = _odd_sample_tail(ci, state_conv[j].transpose(1, 0, 2), conv_w[j], o_lat_t, wuv_ext,
                                            n_p, ts, bd)
            w_out = (w_out_odd, j)
            outs["cp"].append(cst_p[:, 8 - (CONV_WIDTH - 1):])
            outs["cs"].append(cst_s.transpose(1, 0, 2))
            outs["ckp"].append(ckv[:n_p].reshape(nb, s, KV_LORA))
            outs["kpp"].append(kpe[:n_p, :ROPE_DIM].reshape(nb, s, ROPE_DIM))
            outs["cks"].append(to_b_major(ckv[n_p:]))
            outs["kps"].append(to_b_major(kpe[n_p:, :ROPE_DIM]))
        x = _ffn(x, ffn2_norm[l], *w2, l, n_pt, mix=(mix_p, mix_s) + w_out,
                 final_g=final_norm if l == depth - 1 else None)
    y_prompt = x[:n_p].reshape(nb, s, d)
    y_sample = x[n_p:].reshape(ts, bd, d).transpose(1, 0, 2)
    st = lambda k: jnp.stack(outs[k])
    return (y_prompt, y_sample, st("kp"), st("vp"), st("ks"), st("vs"), st("sv"), st("cp"), st("cs"),
            st("ckp"), st("kpp"), st("cks"), st("kps"))
```

```python
import functools
import math

import numpy as np
import jax
import jax.numpy as jnp
from jax import lax
from jax.experimental import pallas as pl
from jax.experimental.pallas import tpu as pltpu

F32 = jnp.float32
BF16 = jnp.bfloat16

HEAD_DIM = 64
A_HEADS = 8
A_KV_HEADS = 2
A_Q_W = A_HEADS * HEAD_DIM
A_KV_W = A_KV_HEADS * HEAD_DIM
WINDOW = 128
N_BUCKETS = 32
MAX_DISTANCE = 128
SG_GROUPS = 4
SG_GROUP_DIM = 128
SG_WIDTH = SG_GROUPS * SG_GROUP_DIM
CHUNK = 128
C_WIDTH = 512
CONV_WIDTH = 3
MLA_HEADS = 8
NOPE_DIM = 64
ROPE_DIM = 32
V_DIM = 64
Q_LORA = 768
KV_LORA = 256
ROPE_THETA = 10000.0
PAGE_SIZE = 128
EPS = 1e-6
NEG = -1e30

LANE = 128
KC_W = KV_LORA + LANE
TM = 512
FC = 512
TQ_EVEN = 512
TQ_ODD = 256
KB_ODD = 512
CHAINS_ODD = 2
BB_EVEN = 16
MLA_SAMPLE_CHUNK = 2048
MLA_SAMPLE_BATCHES = 2
MLA_SAMPLE_VMEM = 60 * 1024 * 1024
VMEM_LIMIT = 56 * 1024 * 1024


def _cp(sem, vmem=VMEM_LIMIT):
    return pltpu.CompilerParams(dimension_semantics=sem, vmem_limit_bytes=vmem)


def _dot(a, b):
    return jnp.dot(a, b, preferred_element_type=F32)


def _dot_nt(a, b):
    return lax.dot_general(a, b, (((1,), (1,)), ((), ())), preferred_element_type=F32)


def _rms(x, g):
    ms = jnp.mean(x * x, axis=-1, keepdims=True)
    return x * lax.rsqrt(ms + EPS) * g


def _gelu(x):
    return 0.5 * x * (1.0 + lax.erf(x * np.float32(math.sqrt(0.5))))


def _const_spec(shape):
    n = len(shape)
    return pl.BlockSpec(shape, lambda *_: (0,) * n, pipeline_mode=pl.Buffered(1))


def _smem_spec():
    return pl.BlockSpec(memory_space=pltpu.SMEM)


def _ffn_chunks(f):
    chunks = [(c, FC) for c in range(0, f - f % FC, FC)]
    if f % FC:
        chunks.append((f - f % FC, f % FC))
    return chunks


def _ffn_kernel(*refs, split_in, has_mix, has_final, n_prompt_tiles, chunks):
    it = iter(refs)
    x_ref = next(it)
    if split_in:
        xs_ref = next(it)
    if has_mix:
        mixp_ref, mixs_ref, wout_ref = next(it), next(it), next(it)
    g_ref, wg_ref, wu_ref, wd_ref = next(it), next(it), next(it), next(it)
    if has_final:
        fg_ref = next(it)
    o_ref = next(it)
    if has_final:
        os_ref = next(it)
    acc_ref = next(it)

    is_sample = pl.program_id(0) >= n_prompt_tiles
    x = x_ref[...]
    if split_in:
        x = jnp.where(is_sample, xs_ref[...], x)
    if has_mix:
        mix = jnp.where(is_sample, mixs_ref[...], mixp_ref[...])
        x = x + _dot(mix, wout_ref[...])
    h = _rms(x, g_ref[...]).astype(BF16)
    for k, (c0, cw) in enumerate(chunks):
        g = _dot(h, wg_ref[:, c0:c0 + cw])
        u = _dot(h, wu_ref[:, c0:c0 + cw])
        a = (g * jax.nn.sigmoid(g) * u).astype(BF16)
        d = _dot(a, wd_ref[c0:c0 + cw, :])
        if k == 0:
            acc_ref[...] = d
        else:
            acc_ref[...] += d
    y = x + 0.5 * acc_ref[...]
    if has_final:
        y = _rms(y, fg_ref[...])

        @pl.when(jnp.logical_not(is_sample))
        def _():
            o_ref[...] = y

        @pl.when(is_sample)
        def _():
            os_ref[...] = y
    else:
        o_ref[...] = y


def _layer_spec(shape, layer):
    n = len(shape)
    return pl.BlockSpec((None,) + tuple(shape), lambda *_: (layer,) + (0,) * n, pipeline_mode=pl.Buffered(1))


def _ffn(x, norm_g, w_gate, w_up, w_down, layer, n_prompt_tiles, mix=None, final_g=None):
    n_tiles = n_prompt_tiles + 1
    last_p = n_prompt_tiles - 1
    row = lambda i: (i, 0)
    prompt_row = lambda i: (jnp.minimum(i, last_p), 0)
    sample_row = lambda i: (0, 0)
    split_in = isinstance(x, tuple)
    d = x[0].shape[1] if split_in else x.shape[1]
    t = n_tiles * TM
    f = w_down.shape[1]
    if split_in:
        args, specs = list(x), [pl.BlockSpec((TM, d), prompt_row), pl.BlockSpec((TM, d), sample_row)]
    else:
        args, specs = [x], [pl.BlockSpec((TM, d), row)]
    if mix is not None:
        mix_p, mix_s, w_out, e = mix
        dm = w_out.shape[1]
        args += [mix_p, mix_s, w_out]
        specs += [pl.BlockSpec((TM, dm), prompt_row), pl.BlockSpec((TM, dm), sample_row),
                  _layer_spec((dm, d), e)]
    args += [norm_g.reshape(1, d), w_gate, w_up, w_down]
    specs += [_const_spec((1, d)), _layer_spec((d, f), layer), _layer_spec((d, f), layer),
              _layer_spec((f, d), layer)]
    if final_g is not None:
        args.append(final_g.reshape(1, d))
        specs.append(_const_spec((1, d)))
    body = functools.partial(_ffn_kernel, split_in=split_in, has_mix=mix is not None,
                             has_final=final_g is not None, n_prompt_tiles=n_prompt_tiles, chunks=_ffn_chunks(f))
    if final_g is not None:
        out_specs = [pl.BlockSpec((TM, d), prompt_row), pl.BlockSpec((TM, d), sample_row)]
        out_shape = [jax.ShapeDtypeStruct((n_prompt_tiles * TM, d), F32), jax.ShapeDtypeStruct((TM, d), F32)]
    else:
        out_specs = pl.BlockSpec((TM, d), row)
        out_shape = jax.ShapeDtypeStruct((t, d), F32)
    return pl.pallas_call(
        body,
        grid=(n_tiles,),
        in_specs=specs,
        out_specs=out_specs,
        out_shape=out_shape,
        scratch_shapes=[pltpu.VMEM((TM, d), F32)],
        compiler_params=_cp(("arbitrary",)),
        name="ffn",
    )(*args)


def _t5_bucket_np(dist):
    n = np.maximum(dist, 0)
    max_exact = N_BUCKETS // 2
    nf = np.maximum(n, 1).astype(np.float32)
    large = max_exact + (np.log(nf / np.float32(max_exact)) / np.float32(math.log(MAX_DISTANCE / max_exact))
                         * np.float32(N_BUCKETS - max_exact)).astype(np.int32)
    return np.where(n < max_exact, n, np.minimum(large, N_BUCKETS - 1)).astype(np.int32)


def _bias_tables_np(ts):
    q = np.arange(WINDOW)[:, None]
    j = np.arange(2 * WINDOW)[None, :]
    bucket_p = _t5_bucket_np(q - j + WINDOW)
    nb = 8 // ts
    cols = nb * WINDOW + 16
    bucket_s = -np.ones((8, cols), np.int32)
    for r in range(8):
        bi, t = divmod(r, ts)
        for c in range(cols):
            if c < nb * WINDOW:
                bj, jj = divmod(c, WINDOW)
                dist = t - (jj - WINDOW)
            elif c < nb * WINDOW + 8:
                bj, tt = divmod(c - nb * WINDOW, ts)
                dist = t - tt
            else:
                continue
            if bj == bi and 0 <= dist <= WINDOW:
                bucket_s[r, c] = _t5_bucket_np(np.array(dist))
    return bucket_p, bucket_s


def _bias_kernel(rb_ref, bp_ref, bs_ref, op_ref, os_ref):
    h = pl.program_id(0)
    bp = bp_ref[...]
    bs = bs_ref[...]
    accp = jnp.zeros(bp.shape, F32)
    accs = jnp.full(bs.shape, NEG, F32)
    for k in range(N_BUCKETS):
        val = rb_ref[k, h]
        accp = jnp.where(bp == k, val, accp)
        accs = jnp.where(bs == k, val, accs)
    op_ref[0] = accp
    os_ref[0] = accs


def _bias_tables(rel_bias, ts):
    bucket_p, bucket_s = _bias_tables_np(ts)
    return pl.pallas_call(
        _bias_kernel,
        grid=(A_HEADS,),
        in_specs=[_smem_spec(),
                  pl.BlockSpec(bucket_p.shape, lambda h: (0, 0)),
                  pl.BlockSpec(bucket_s.shape, lambda h: (0, 0))],
        out_specs=[pl.BlockSpec((1,) + bucket_p.shape, lambda h: (h, 0, 0)),
                   pl.BlockSpec((1,) + bucket_s.shape, lambda h: (h, 0, 0))],
        out_shape=[jax.ShapeDtypeStruct((A_HEADS,) + bucket_p.shape, F32),
                   jax.ShapeDtypeStruct((A_HEADS,) + bucket_s.shape, F32)],
        compiler_params=_cp(("arbitrary",)),
        name="t5_bias",
    )(rel_bias, jnp.asarray(bucket_p), jnp.asarray(bucket_s))


def _inproj_even_kernel(x_ref, g_ref, w_ref, q_ref, kv_ref, ug_ref):
    h = _rms(x_ref[...], g_ref[...]).astype(BF16)
    p = _dot(h, w_ref[...])
    q_ref[...] = (p[:, :A_Q_W] * np.float32(HEAD_DIM ** -0.5)).astype(BF16)
    kv_ref[...] = p[:, A_Q_W:A_Q_W + 2 * A_KV_W]
    ug_ref[...] = p[:, A_Q_W + 2 * A_KV_W:]


def _inproj_even(x, norm_g, w_in):
    t, d = x.shape
    n = w_in.shape[1]
    row = lambda i: (i, 0)
    return pl.pallas_call(
        _inproj_even_kernel,
        grid=(t // TM,),
        in_specs=[pl.BlockSpec((TM, d), row), _const_spec((1, d)), _const_spec((d, n))],
        out_specs=[pl.BlockSpec((TM, A_Q_W), row), pl.BlockSpec((TM, 2 * A_KV_W), row),
                   pl.BlockSpec((TM, 2 * SG_WIDTH), row)],
        out_shape=[jax.ShapeDtypeStruct((t, A_Q_W), BF16), jax.ShapeDtypeStruct((t, 2 * A_KV_W), F32),
                   jax.ShapeDtypeStruct((t, 2 * SG_WIDTH), F32)],
        compiler_params=_cp(("arbitrary",)),
        name="inproj_even",
    )(x, norm_g.reshape(1, d), w_in)


def _head_variants(x):
    lo = lax.broadcasted_iota(jnp.int32, x.shape, 1) < HEAD_DIM
    xr = pltpu.roll(x, HEAD_DIM, 1)
    z = jnp.zeros_like(x)
    return [[jnp.where(lo, x, z).astype(BF16), jnp.where(lo, z, xr).astype(BF16)],
            [jnp.where(lo, xr, z).astype(BF16), jnp.where(lo, z, x).astype(BF16)]]


def _sink_softmax_pv(s, sink, v):
    mx = jnp.maximum(jnp.max(s, axis=-1, keepdims=True), sink)
    p = jnp.exp(s - mx)
    den = jnp.sum(p, axis=-1, keepdims=True) + jnp.exp(sink - mx)
    return _dot(p.astype(BF16), v) * (1.0 / den)


def _layer_norm(x, g, b):
    xc = x - jnp.mean(x, axis=-1, keepdims=True)
    y = xc * lax.rsqrt(jnp.mean(xc * xc, axis=-1, keepdims=True) + EPS)
    return y * g + b


def _even_prompt_kernel(q_ref, kv_ref, kvp_ref, ug_ref, bias_ref, sink_ref, lng_ref, lnb_ref, ws_ref,
                        bst_ref, o_ref, *, tq):
    nblk = tq // WINDOW
    step = pl.program_id(1)
    kv_all = jnp.concatenate([kvp_ref[...], kv_ref[...]], axis=0)
    kvar = _head_variants(kv_all[:, :A_KV_W])
    vvar = _head_variants(kv_all[:, A_KV_W:])
    qi = lax.broadcasted_iota(jnp.int32, (WINDOW, 2 * WINDOW), 0)
    kj = lax.broadcasted_iota(jnp.int32, (WINDOW, 2 * WINDOW), 1)
    dist = qi - kj + WINDOW
    band = (dist >= 0) & (dist <= WINDOW)
    cur = kj >= WINDOW
    ri = lax.broadcasted_iota(jnp.int32, (CHUNK, CHUNK), 0)
    ci = lax.broadcasted_iota(jnp.int32, (CHUNK, CHUNK), 1)
    wt = [jnp.where(ri >= ci, ws_ref[g], 0.0).astype(BF16) for g in range(SG_GROUPS)]
    lng, lnb = lng_ref[...], lnb_ref[...]
    for n in range(nblk):
        r0 = n * WINDOW
        has_prev = (step * nblk + n) > 0
        mask = band & (cur | has_prev)
        for m in range(A_HEADS // 2):
            kvh = (2 * m) // (A_HEADS // A_KV_HEADS)
            qp = q_ref[r0:r0 + WINDOW, LANE * m:LANE * (m + 1)]
            o = None
            for half in range(2):
                hd = 2 * m + half
                s = _dot_nt(qp, kvar[kvh][half][r0:r0 + 2 * WINDOW])
                s = jnp.where(mask, s + bias_ref[hd], NEG)
                oh = _sink_softmax_pv(s, sink_ref[hd], vvar[kvh][half][r0:r0 + 2 * WINDOW])
                o = oh if o is None else o + oh
            o_ref[r0:r0 + WINDOW, LANE * m:LANE * (m + 1)] = o.astype(BF16)
        u = _gelu(ug_ref[r0:r0 + WINDOW, :SG_WIDTH])
        v = _layer_norm(_gelu(ug_ref[r0:r0 + WINDOW, SG_WIDTH:]), lng, lnb)
        for g in range(SG_GROUPS):
            sl = slice(SG_GROUP_DIM * g, SG_GROUP_DIM * (g + 1))
            y = _dot(wt[g], v[:, sl].astype(BF16)) + bst_ref[:, g:g + 1]
            o_ref[r0:r0 + WINDOW, A_Q_W + SG_GROUP_DIM * g:A_Q_W + SG_GROUP_DIM * (g + 1)] = (
                u[:, sl] * y).astype(BF16)


def _mix_even_prompt(q, kv, ug, bias_p, sinks, ln_g, ln_b, w_s, b_s, nb, s):
    tq = TQ_EVEN
    nq = s // tq
    blk = lambda b, i: (b * nq + i, 0)
    prev = lambda b, i: (jnp.maximum((b * nq + i) * (tq // WINDOW) - 1, 0), 0)
    return pl.pallas_call(
        functools.partial(_even_prompt_kernel, tq=tq),
        grid=(nb, nq),
        in_specs=[pl.BlockSpec((tq, A_Q_W), blk), pl.BlockSpec((tq, 2 * A_KV_W), blk),
                  pl.BlockSpec((WINDOW, 2 * A_KV_W), prev), pl.BlockSpec((tq, 2 * SG_WIDTH), blk),
                  _const_spec(bias_p.shape), _smem_spec(), _const_spec((1, SG_WIDTH)),
                  _const_spec((1, SG_WIDTH)), _const_spec(w_s.shape), _const_spec((CHUNK, SG_GROUPS))],
        out_specs=pl.BlockSpec((tq, A_Q_W + SG_WIDTH), blk),
        out_shape=jax.ShapeDtypeStruct((nb * s, A_Q_W + SG_WIDTH), BF16),
        compiler_params=_cp(("arbitrary", "arbitrary")),
        name="mix_even_prompt",
    )(q, kv, kv, ug, bias_p, sinks, ln_g.reshape(1, -1), ln_b.reshape(1, -1), w_s, b_s.T)


def _even_sample_kernel(q_ref, kvn_ref, ck_ref, cv_ref, bias_ref, sink_ref, a_ref, nk_ref, nv_ref, *, ts):
    bb = ck_ref.shape[0]
    gpb = 8 // ts
    wb = ck_ref.shape[1]
    row = lax.broadcasted_iota(jnp.int32, (16, 1), 0)
    zpad = jnp.zeros((8, A_KV_W), F32)
    heads_per_kv = A_HEADS // A_KV_HEADS
    for g in range(bb // gpb):
        r0 = 8 * g
        qg = q_ref[r0:r0 + 8, :]
        kn = kvn_ref[r0:r0 + 8, :A_KV_W]
        vn = kvn_ref[r0:r0 + 8, A_KV_W:]
        kg = jnp.concatenate([ck_ref[gpb * g + i] for i in range(gpb)] + [kn, zpad], axis=0)
        vg = jnp.concatenate([cv_ref[gpb * g + i] for i in range(gpb)] + [vn, zpad], axis=0)
        kvar = _head_variants(kg)
        vvar = _head_variants(vg)
        for kvh in range(A_KV_HEADS):
            c0 = 2 * LANE * kvh
            qk = jnp.concatenate([qg[:, c0:c0 + LANE], qg[:, c0 + LANE:c0 + 2 * LANE]], axis=0).astype(BF16)
            o = None
            for half in range(2):
                h0 = heads_per_kv * kvh + half
                h1 = h0 + 2
                bm = jnp.concatenate([bias_ref[h0], bias_ref[h1]], axis=0)
                sink = jnp.where(row < 8, sink_ref[h0], sink_ref[h1])
                s = _dot_nt(qk, kvar[kvh][half])
                s = jnp.where(bm > 0.5 * NEG, s + bm, NEG)
                oh = _sink_softmax_pv(s, sink, vvar[kvh][half])
                o = oh if o is None else o + oh
            a_ref[r0:r0 + 8, c0:c0 + LANE] = o[:8]
            a_ref[r0:r0 + 8, c0 + LANE:c0 + 2 * LANE] = o[8:]
        for i in range(gpb):
            b = gpb * g + i
            nk_ref[b, 0:wb - ts, :] = ck_ref[b, ts:wb, :]
            nk_ref[b, wb - ts:wb, :] = kn[ts * i:ts * (i + 1)]
            nv_ref[b, 0:wb - ts, :] = cv_ref[b, ts:wb, :]
            nv_ref[b, wb - ts:wb, :] = vn[ts * i:ts * (i + 1)]


def _mix_even_sample(q_b, kvn_b, cache_k, cache_v, bias_s, sinks, ts):
    bd, wb, _ = cache_k.shape
    bb = BB_EVEN
    rows = bb * ts
    blk2 = lambda i: (i, 0)
    blk3 = lambda i: (i, 0, 0)
    return pl.pallas_call(
        functools.partial(_even_sample_kernel, ts=ts),
        grid=(bd // bb,),
        in_specs=[pl.BlockSpec((rows, A_Q_W), blk2), pl.BlockSpec((rows, 2 * A_KV_W), blk2),
                  pl.BlockSpec((bb, wb, A_KV_W), blk3), pl.BlockSpec((bb, wb, A_KV_W), blk3),
                  _const_spec(bias_s.shape), _smem_spec()],
        out_specs=[pl.BlockSpec((rows, A_Q_W), blk2), pl.BlockSpec((bb, wb, A_KV_W), blk3),
                   pl.BlockSpec((bb, wb, A_KV_W), blk3)],
        out_shape=[jax.ShapeDtypeStruct((bd * ts, A_Q_W), F32),
                   jax.ShapeDtypeStruct(cache_k.shape, F32), jax.ShapeDtypeStruct(cache_v.shape, F32)],
        compiler_params=_cp(("arbitrary",)),
        name="mix_even_sample",
    )(q_b, kvn_b, cache_k, cache_v, bias_s, sinks)


def _sgu_sample_kernel(ug_ref, lng_ref, lnb_ref, w_ref, b_ref, sg_ref, v_ref, *, ts, bd):
    u = _gelu(ug_ref[:, :SG_WIDTH])
    v = _layer_norm(_gelu(ug_ref[:, SG_WIDTH:]), lng_ref[...], lnb_ref[...])
    v_ref[...] = v
    for i in range(ts):
        for g in range(SG_GROUPS):
            sl = slice(SG_GROUP_DIM * g, SG_GROUP_DIM * (g + 1))
            y = jnp.full((bd, SG_GROUP_DIM), b_ref[g, i], F32)
            for j in range(i + 1):
                y = y + w_ref[g, i * ts + j] * v[bd * j:bd * (j + 1), sl]
            sg_ref[bd * i:bd * (i + 1), sl] = (u[bd * i:bd * (i + 1), sl] * y).astype(BF16)


def _sgu_sample(ug, ln_g, ln_b, w_s, b_s, n_prompt_rows, ts, bd):
    rows = ts * bd
    off = n_prompt_rows // rows
    w_small = w_s[:, :ts, :ts].reshape(SG_GROUPS, ts * ts)
    b_small = b_s[:, :ts]
    return pl.pallas_call(
        functools.partial(_sgu_sample_kernel, ts=ts, bd=bd),
        grid=(1,),
        in_specs=[pl.BlockSpec((rows, 2 * SG_WIDTH), lambda i: (off, 0)), _const_spec((1, SG_WIDTH)),
                  _const_spec((1, SG_WIDTH)), _smem_spec(), _smem_spec()],
        out_specs=[pl.BlockSpec((rows, SG_WIDTH), lambda i: (0, 0)),
                   pl.BlockSpec((rows, SG_WIDTH), lambda i: (0, 0))],
        out_shape=[jax.ShapeDtypeStruct((rows, SG_WIDTH), BF16), jax.ShapeDtypeStruct((rows, SG_WIDTH), F32)],
        compiler_params=_cp(("arbitrary",)),
        name="sgu_sample",
    )(ug, ln_g.reshape(1, -1), ln_b.reshape(1, -1), w_small, b_small)


def _rope_tables_np(s, past, ts, bd):
    half = ROPE_DIM // 2
    inv = np.exp(-math.log(ROPE_THETA) * np.arange(half, dtype=np.float32) / np.float32(half)).astype(np.float32)
    pos = np.concatenate([np.arange(s), past + np.repeat(np.arange(ts), bd)]).astype(np.float32)
    ang = (pos[:, None] * inv[None, :]).astype(np.float32).astype(np.float64)
    cos = np.cos(ang)
    sin = np.sin(ang)
    cos_t = np.tile(np.concatenate([cos, cos], axis=1), (1, MLA_HEADS)).astype(np.float32)
    sin_t = np.tile(np.concatenate([-sin, sin], axis=1), (1, MLA_HEADS)).astype(np.float32)
    return cos_t, sin_t


def _rope(x, cos_t, sin_t):
    n = x.shape[1]
    half = ROPE_DIM // 2
    first = (lax.broadcasted_iota(jnp.int32, x.shape, 1) % ROPE_DIM) < half
    partner = jnp.where(first, pltpu.roll(x, n - half, 1), pltpu.roll(x, half, 1))
    return x * cos_t + partner * sin_t


def _inproj_odd_kernel(x_ref, g_ref, w_ref, qg_ref, wuq_ref, kvg_ref, cos_ref, sin_ref,
                       ci_ref, qn_ref, qp_ref, kc_ref, ckv_ref, kpe_ref):
    h = _rms(x_ref[...], g_ref[...]).astype(BF16)
    p = _dot(h, w_ref[...])
    o_q = 3 * C_WIDTH
    o_kv = o_q + Q_LORA
    o_pe = o_kv + KV_LORA
    ci_ref[...] = p[:, :o_q].astype(BF16)
    scale = np.float32((NOPE_DIM + ROPE_DIM) ** -0.5)
    q = _dot(_rms(p[:, o_q:o_kv], qg_ref[...]).astype(BF16), wuq_ref[...])
    n_nope = MLA_HEADS * NOPE_DIM
    cos_t, sin_t = cos_ref[...], sin_ref[...]
    qn_ref[...] = (q[:, :n_nope] * scale).astype(BF16)
    qp_ref[...] = (_rope(q[:, n_nope:], cos_t, sin_t) * scale).astype(BF16)
    ckv = _rms(p[:, o_kv:o_pe], kvg_ref[...])
    kpe = _rope(p[:, o_pe:], cos_t[:, :LANE], sin_t[:, :LANE])
    ckv_ref[...] = ckv
    kpe_ref[...] = kpe
    kc_ref[:, :KV_LORA] = ckv.astype(BF16)
    kc_ref[:, KV_LORA:] = kpe.astype(BF16)


def _inproj_odd(x, norm_g, w_in, q_norm, w_uq, kv_norm, cos_t, sin_t, n_prompt_tiles, s):
    t, d = x.shape
    n = w_in.shape[1]
    row = lambda i: (i, 0)
    per_seq = s // TM
    tab = lambda i: (jnp.where(i < n_prompt_tiles, i % per_seq, per_seq), 0)
    n_pe = MLA_HEADS * ROPE_DIM
    widths = [(3 * C_WIDTH, BF16), (MLA_HEADS * NOPE_DIM, BF16), (n_pe, BF16), (KC_W, BF16),
              (KV_LORA, F32), (LANE, F32)]
    return pl.pallas_call(
        _inproj_odd_kernel,
        grid=(t // TM,),
        in_specs=[pl.BlockSpec((TM, d), row), _const_spec((1, d)), _const_spec((d, n)),
                  _const_spec((1, Q_LORA)), _const_spec(w_uq.shape), _const_spec((1, KV_LORA)),
                  pl.BlockSpec((TM, n_pe), tab), pl.BlockSpec((TM, n_pe), tab)],
        out_specs=[pl.BlockSpec((TM, w), row) for w, _ in widths],
        out_shape=[jax.ShapeDtypeStruct((t, w), dt) for w, dt in widths],
        compiler_params=_cp(("arbitrary",)),
        name="inproj_odd",
    )(x, norm_g.reshape(1, d), w_in, q_norm.reshape(1, -1), w_uq, kv_norm.reshape(1, -1), cos_t, sin_t)


def _build_qx(qn_ref, qp_ref, wuk_ref, qx_ref, rows):
    qp = qp_ref[...].astype(F32)
    keep = lax.broadcasted_iota(jnp.int32, (rows, LANE), 1) < ROPE_DIM
    n_pe = MLA_HEADS * ROPE_DIM
    for h in range(MLA_HEADS):
        pair = qn_ref[:, LANE * (h // 2):LANE * (h // 2 + 1)]
        qx_ref[h * rows:(h + 1) * rows, :KV_LORA] = _dot(pair, wuk_ref[h]).astype(BF16)
        pe = qp if h == 0 else pltpu.roll(qp, n_pe - ROPE_DIM * h, 1)
        qx_ref[h * rows:(h + 1) * rows, KV_LORA:] = jnp.where(keep, pe[:, :LANE], 0.0).astype(BF16)


def _odd_prompt_kernel(ci_ref, cih_ref, qn_ref, qp_ref, kc_ref, wuk_ref, wuv_ref, cw_ref,
                       o_ref, cst_ref, qx_ref, m_ref, l_ref, acc_ref, z_ref, *, tq, kb):
    step = pl.program_id(1)
    halo = 16
    zh = cih_ref[:, C_WIDTH:2 * C_WIDTH].astype(F32) * cih_ref[:, 2 * C_WIDTH:].astype(F32)
    z_ref[0:halo, :] = jnp.where(step > 0, zh, 0.0)
    z = ci_ref[:, C_WIDTH:2 * C_WIDTH].astype(F32) * ci_ref[:, 2 * C_WIDTH:].astype(F32)
    z_ref[halo:halo + tq, :] = z
    cw = cw_ref[...]
    y = cw[0:1] * z_ref[halo - 2:halo - 2 + tq, :] + cw[1:2] * z_ref[halo - 1:halo - 1 + tq, :] + cw[2:3] * z
    o_ref[:, :C_WIDTH] = (ci_ref[:, :C_WIDTH].astype(F32) * y).astype(BF16)
    cst_ref[0] = z[tq - 8:tq]

    _build_qx(qn_ref, qp_ref, wuk_ref, qx_ref, tq)
    rows = MLA_HEADS * tq
    m_ref[...] = jnp.full((rows, LANE), NEG, F32)
    l_ref[...] = jnp.zeros((rows, LANE), F32)
    acc_ref[...] = jnp.zeros((rows, KV_LORA), F32)
    n_full = (step * tq) // kb
    hpc = MLA_HEADS // CHAINS_ODD

    def kv_step(j, width, masked):
        k0 = pl.multiple_of(j * kb, kb)
        kc = kc_ref[pl.ds(k0, width), :]
        n_grp = width // LANE
        if masked:
            qpos = step * tq + lax.broadcasted_iota(jnp.int32, (tq, LANE), 0)
            kpos = k0 + lax.broadcasted_iota(jnp.int32, (tq, LANE), 1)
            visible = [kpos + LANE * g <= qpos for g in range(n_grp)]
        for c in range(CHAINS_ODD):
            rc = slice(c * hpc * tq, (c + 1) * hpc * tq)
            s_all = _dot_nt(qx_ref[rc, :], kc)
            ps, alphas = [], []
            for hh in range(hpc):
                r = slice((c * hpc + hh) * tq, (c * hpc + hh + 1) * tq)
                sg = [s_all[hh * tq:(hh + 1) * tq, LANE * g:LANE * (g + 1)] for g in range(n_grp)]
                if masked:
                    sg = [jnp.where(visible[g], sg[g], NEG) for g in range(n_grp)]
                m_old = m_ref[r, :]
                m_new = jnp.maximum(m_old, jnp.max(functools.reduce(jnp.maximum, sg), axis=-1, keepdims=True))
                alpha = jnp.exp(m_old - m_new)
                pg = [jnp.exp(x - m_new) for x in sg]
                l_ref[r, :] = alpha * l_ref[r, :] + jnp.sum(functools.reduce(jnp.add, pg), axis=-1, keepdims=True)
                m_ref[r, :] = m_new
                ps.append(jnp.concatenate([x.astype(BF16) for x in pg], axis=1))
                alphas.append(jnp.concatenate([alpha] * (KV_LORA // LANE), axis=1))
            pv = _dot(jnp.concatenate(ps, axis=0), kc[:, :KV_LORA])
            acc_ref[rc, :] = jnp.concatenate(alphas, axis=0) * acc_ref[rc, :] + pv

    def full_step(j, carry):
        kv_step(j, kb, False)
        return carry

    lax.fori_loop(0, n_full, full_step, 0)
    for i, width in enumerate(range(tq, kb + 1, tq)):
        @pl.when(step % (kb // tq) == i)
        def _(width=width):
            kv_step(n_full, width, True)

    for m in range(MLA_HEADS // 2):
        o = None
        for half in range(2):
            h = 2 * m + half
            inv_l = 1.0 / l_ref[h * tq:(h + 1) * tq, :]
            ol = (acc_ref[h * tq:(h + 1) * tq, :] * jnp.concatenate([inv_l] * (KV_LORA // LANE), axis=1)).astype(BF16)
            oh = _dot(ol, wuv_ref[h])
            o = oh if o is None else o + oh
        o_ref[:, C_WIDTH + LANE * m:C_WIDTH + LANE * (m + 1)] = o.astype(BF16)


def _mix_odd_prompt(ci, qn, qp, kc, wuk_ext, wuv_ext, conv_w, nb, s):
    tq, kb = TQ_ODD, KB_ODD
    nq = s // tq
    blk = lambda b, i: (b * nq + i, 0)
    halo = lambda b, i: (jnp.maximum((b * nq + i) * (tq // 16) - 1, 0), 0)
    rows = MLA_HEADS * tq
    n_out = C_WIDTH + MLA_HEADS * V_DIM
    return pl.pallas_call(
        functools.partial(_odd_prompt_kernel, tq=tq, kb=kb),
        grid=(nb, nq),
        in_specs=[pl.BlockSpec((tq, 3 * C_WIDTH), blk), pl.BlockSpec((16, 3 * C_WIDTH), halo),
                  pl.BlockSpec((tq, MLA_HEADS * NOPE_DIM), blk), pl.BlockSpec((tq, MLA_HEADS * ROPE_DIM), blk),
                  pl.BlockSpec((s, KC_W), lambda b, i: (b, 0)),
                  _const_spec(wuk_ext.shape), _const_spec(wuv_ext.shape), _const_spec(conv_w.shape)],
        out_specs=[pl.BlockSpec((tq, n_out), blk), pl.BlockSpec((1, 8, C_WIDTH), lambda b, i: (b, 0, 0))],
        out_shape=[jax.ShapeDtypeStruct((nb * s, n_out), BF16), jax.ShapeDtypeStruct((nb, 8, C_WIDTH), F32)],
        scratch_shapes=[pltpu.VMEM((rows, KC_W), BF16),
                        pltpu.VMEM((rows, LANE), F32), pltpu.VMEM((rows, LANE), F32), pltpu.VMEM((rows, KV_LORA), F32),
                        pltpu.VMEM((tq + 16, C_WIDTH), F32)],
        compiler_params=_cp(("arbitrary", "arbitrary")),
        name="mix_odd_prompt",
    )(ci, ci, qn, qp, kc, wuk_ext, wuv_ext, conv_w)


def _qx_sample_kernel(qn_ref, qp_ref, wuk_ref, qx_ref, *, rows):
    _build_qx(qn_ref, qp_ref, wuk_ref, qx_ref, rows)


def _qx_sample(qn, qp, wuk_ext, n_prompt_rows, rows):
    off = n_prompt_rows // rows
    return pl.pallas_call(
        functools.partial(_qx_sample_kernel, rows=rows),
        grid=(1,),
        in_specs=[pl.BlockSpec((rows, MLA_HEADS * NOPE_DIM), lambda i: (off, 0)),
                  pl.BlockSpec((rows, MLA_HEADS * ROPE_DIM), lambda i: (off, 0)), _const_spec(wuk_ext.shape)],
        out_specs=pl.BlockSpec((MLA_HEADS * rows, KC_W), lambda i: (0, 0)),
        out_shape=jax.ShapeDtypeStruct((MLA_HEADS * rows, KC_W), BF16),
        compiler_params=_cp(("arbitrary",)),
        name="qx_sample",
    )(qn, qp, wuk_ext)


def _page_copies(pt_ref, pool_ckv, pool_kpet, ckv_buf, kpe_buf, sem, layer, n_pages, step, slot):
    copies = []
    for i in range(MLA_SAMPLE_BATCHES):
        b = step * MLA_SAMPLE_BATCHES + i
        for p in range(n_pages):
            page = pt_ref[b, p]
            keys = pl.ds(p * PAGE_SIZE, PAGE_SIZE)
            copies.append(pltpu.make_async_copy(pool_ckv.at[layer, page], ckv_buf.at[slot, i, keys],
                                                sem.at[0, slot]))
            copies.append(pltpu.make_async_copy(pool_kpet.at[layer, page], kpe_buf.at[slot, i, :, keys],
                                                sem.at[1, slot]))
    return copies


def _mla_sample_kernel(pt_ref, qx_ref, kn_ref, pool_ckv, pool_kpet, o_ref, ckv_buf, kpe_buf, sem, *,
                       ts, n_pages, layer, chunk):
    step = pl.program_id(0)
    n_steps = pl.num_programs(0)
    slot = step % 2
    copies = functools.partial(_page_copies, pt_ref, pool_ckv, pool_kpet, ckv_buf, kpe_buf, sem, layer, n_pages)

    @pl.when(step == 0)
    def _():
        for c in copies(0, 0):
            c.start()

    @pl.when(step + 1 < n_steps)
    def _():
        for c in copies(step + 1, 1 - slot):
            c.start()

    for c in copies(step, slot):
        c.wait()

    rows = qx_ref.shape[1]
    t_q = lax.broadcasted_iota(jnp.int32, (rows, 8), 0) % ts
    t_k = lax.broadcasted_iota(jnp.int32, (rows, 8), 1)
    past = n_pages * PAGE_SIZE
    starts = range(0, past, chunk)
    for i in range(MLA_SAMPLE_BATCHES):
        qx = qx_ref[i]
        q_lat = qx[:, :KV_LORA]
        q_pe = qx[:, KV_LORA:KV_LORA + ROPE_DIM]
        kn = kn_ref[i].astype(BF16)
        sn = jnp.where(t_k <= t_q, _dot_nt(qx, kn), NEG)
        kcs = [ckv_buf[slot, i, k0:k0 + chunk, :].astype(BF16) for k0 in starts]
        ss = [_dot_nt(q_lat, kc) + _dot(q_pe, kpe_buf[slot, i, :, k0:k0 + chunk].astype(BF16))
              for k0, kc in zip(starts, kcs)]
        m = functools.reduce(jnp.maximum, [jnp.max(s, axis=-1, keepdims=True) for s in ss],
                             jnp.max(sn, axis=-1, keepdims=True))
        pn = jnp.exp(sn - m)
        den = jnp.sum(pn, axis=-1, keepdims=True)
        acc = _dot(pn.astype(BF16), kn[:, :KV_LORA])
        for s, kc in zip(ss, kcs):
            p = jnp.exp(s - m)
            den = den + jnp.sum(p, axis=-1, keepdims=True)
            acc = acc + _dot(p.astype(BF16), kc)
        o_ref[i] = acc * (1.0 / den)


def _mla_sample(page_table, qx_b, kn_b, pool_ckv, pool_kpet, layer, ts):
    bd, n_pages = page_table.shape
    past = n_pages * PAGE_SIZE
    rows = MLA_HEADS * ts
    nbs = MLA_SAMPLE_BATCHES
    blk = lambda g, pt: (g, 0, 0)
    grid_spec = pltpu.PrefetchScalarGridSpec(
        num_scalar_prefetch=1,
        grid=(bd // nbs,),
        in_specs=[pl.BlockSpec((nbs, rows, KC_W), blk), pl.BlockSpec((nbs, 8, KC_W), blk),
                  pl.BlockSpec(memory_space=pl.ANY), pl.BlockSpec(memory_space=pl.ANY)],
        out_specs=pl.BlockSpec((nbs, rows, KV_LORA), blk),
        scratch_shapes=[pltpu.VMEM((2, nbs, past, KV_LORA), F32), pltpu.VMEM((2, nbs, ROPE_DIM, past), F32),
                        pltpu.SemaphoreType.DMA((2, 2))],
    )
    return pl.pallas_call(
        functools.partial(_mla_sample_kernel, ts=ts, n_pages=n_pages, layer=layer,
                          chunk=min(MLA_SAMPLE_CHUNK, past)),
        grid_spec=grid_spec,
        out_shape=jax.ShapeDtypeStruct((bd, rows, KV_LORA), F32),
        compiler_params=_cp(("arbitrary",), vmem=MLA_SAMPLE_VMEM),
        name="mla_sample",
    )(page_table, qx_b, kn_b, pool_ckv, pool_kpet)


def _odd_sample_tail_kernel(ci_ref, st_ref, cw_ref, ol_ref, wuv_ref, o_ref, nst_ref, *, ts, bd):
    cw = cw_ref[...]
    zz = [st_ref[0], st_ref[1]]
    for t in range(ts):
        r = slice(bd * t, bd * (t + 1))
        zz.append(ci_ref[r, C_WIDTH:2 * C_WIDTH].astype(F32) * ci_ref[r, 2 * C_WIDTH:].astype(F32))
    for t in range(ts):
        r = slice(bd * t, bd * (t + 1))
        y = cw[0:1] * zz[t] + cw[1:2] * zz[t + 1] + cw[2:3] * zz[t + 2]
        o_ref[r, :C_WIDTH] = (ci_ref[r, :C_WIDTH].astype(F32) * y).astype(BF16)
    nst_ref[0] = zz[ts]
    nst_ref[1] = zz[ts + 1]
    for m in range(MLA_HEADS // 2):
        o = _dot(ol_ref[2 * m], wuv_ref[2 * m]) + _dot(ol_ref[2 * m + 1], wuv_ref[2 * m + 1])
        o_ref[:, C_WIDTH + LANE * m:C_WIDTH + LANE * (m + 1)] = o.astype(BF16)


def _odd_sample_tail(ci, state_t, conv_w, o_lat_t, wuv_ext, n_prompt_rows, ts, bd):
    rows = ts * bd
    off = n_prompt_rows // rows
    n_out = C_WIDTH + MLA_HEADS * V_DIM
    return pl.pallas_call(
        functools.partial(_odd_sample_tail_kernel, ts=ts, bd=bd),
        grid=(1,),
        in_specs=[pl.BlockSpec((rows, 3 * C_WIDTH), lambda i: (off, 0)), _const_spec(state_t.shape),
                  _const_spec(conv_w.shape), _const_spec(o_lat_t.shape), _const_spec(wuv_ext.shape)],
        out_specs=[pl.BlockSpec((rows, n_out), lambda i: (0, 0)),
                   pl.BlockSpec(state_t.shape, lambda i: (0, 0, 0))],
        out_shape=[jax.ShapeDtypeStruct((rows, n_out), BF16), jax.ShapeDtypeStruct(state_t.shape, F32)],
        compiler_params=_cp(("arbitrary",)),
        name="odd_sample_tail",
    )(ci, state_t, conv_w, o_lat_t, wuv_ext)


def _prep_odd_weights(w_in, w_uq, w_uk, w_uv):
    d, n = w_in.shape
    n_pad = -n % LANE
    w_in_p = jnp.pad(w_in, ((0, 0), (0, n_pad))).astype(BF16)
    per = NOPE_DIM + ROPE_DIM
    wq = w_uq.reshape(Q_LORA, MLA_HEADS, per)
    w_uq_r = jnp.concatenate([wq[:, :, :NOPE_DIM].reshape(Q_LORA, -1), wq[:, :, NOPE_DIM:].reshape(Q_LORA, -1)],
                             axis=1).astype(BF16)
    ukt = jnp.transpose(w_uk, (1, 2, 0))
    uvt = jnp.transpose(w_uv, (1, 0, 2))
    wuk_ext, wuv_ext = [], []
    for h in range(MLA_HEADS):
        lo = (h % 2) * NOPE_DIM
        wuk_ext.append(jnp.pad(ukt[h], ((lo, LANE - NOPE_DIM - lo), (0, 0))))
        wuv_ext.append(jnp.pad(uvt[h], ((0, 0), (lo, LANE - V_DIM - lo))))
    return w_in_p, w_uq_r, jnp.stack(wuk_ext).astype(BF16), jnp.stack(wuv_ext).astype(BF16)


def kernel(x_prompt, x_sample, cache_swa_k, cache_swa_v, state_conv, cache_mla_ckv, cache_mla_kpe, page_table,
           rel_bias, ffn1_norm, ffn1_w_gate, ffn1_w_up, ffn1_w_down, mix_norm, ffn2_norm, ffn2_w_gate,
           ffn2_w_up, ffn2_w_down, even_w_in, even_w_out, attn_sinks, sgu_ln_g, sgu_ln_b, sgu_w, sgu_b,
           odd_w_in, odd_w_out, conv_w, mla_q_norm, mla_w_uq, mla_kv_norm, mla_w_uk, mla_w_uv, final_norm):
    nb, s, d = x_prompt.shape
    bd, ts, _ = x_sample.shape
    depth = ffn1_norm.shape[0]
    n_p, n_s = nb * s, bd * ts
    past = page_table.shape[1] * PAGE_SIZE
    wb = cache_swa_k.shape[2]
    assert n_s == TM and n_p % TM == 0 and s % TM == 0, "sample group must fill exactly one token tile"
    assert 8 % ts == 0 and bd % BB_EVEN == 0 and bd % 8 == 0 and wb == WINDOW
    assert s % KB_ODD == 0 and s % TQ_EVEN == 0 and KB_ODD % TQ_ODD == 0
    n_pt = n_p // TM

    def to_b_major(a):
        return a.reshape(ts, bd, -1).transpose(1, 0, 2)

    x = (x_prompt.reshape(n_p, d), x_sample.transpose(1, 0, 2).reshape(n_s, d))
    bias_p, bias_s = _bias_tables(rel_bias, ts)
    cos_np, sin_np = _rope_tables_np(s, past, ts, bd)
    cos_t, sin_t = jnp.asarray(cos_np), jnp.asarray(sin_np)

    w1 = tuple(w.astype(BF16) for w in (ffn1_w_gate, ffn1_w_up, ffn1_w_down))
    w2 = tuple(w.astype(BF16) for w in (ffn2_w_gate, ffn2_w_up, ffn2_w_down))
    w_out_even, w_out_odd = even_w_out.astype(BF16), odd_w_out.astype(BF16)
    pool_kpet = jnp.swapaxes(cache_mla_kpe, 2, 3)

    outs = {k: [] for k in ("kp", "vp", "ks", "vs", "sv", "cp", "cs", "ckp", "kpp", "cks", "kps")}
    for l in range(depth):
        x = _ffn(x, ffn1_norm[l], *w1, l, n_pt)
        if l % 2 == 0:
            e = l // 2
            q, kv, ug = _inproj_even(x, mix_norm[l], even_w_in[e].astype(BF16))
            mix_p = _mix_even_prompt(q, kv, ug, bias_p, attn_sinks[e], sgu_ln_g[e], sgu_ln_b[e], sgu_w[e],
                                     sgu_b[e], nb, s)
            q_b = to_b_major(q[n_p:].astype(F32)).reshape(n_s, A_Q_W)
            kvn_b = to_b_major(kv[n_p:]).reshape(n_s, 2 * A_KV_W)
            a_b, new_k, new_v = _mix_even_sample(q_b, kvn_b, cache_swa_k[e].reshape(bd, wb, A_KV_W),
                                                 cache_swa_v[e].reshape(bd, wb, A_KV_W), bias_s, attn_sinks[e], ts)
            a_t = a_b.reshape(bd, ts, A_Q_W).transpose(1, 0, 2).reshape(n_s, A_Q_W).astype(BF16)
            sg_s, v_rows = _sgu_sample(ug, sgu_ln_g[e], sgu_ln_b[e], sgu_w[e], sgu_b[e], n_p, ts, bd)
            mix_s = jnp.concatenate([a_t, sg_s], axis=1)
            w_out = (w_out_even, e)
            kv_p = kv[:n_p].reshape(nb, s, 2 * A_KV_W)[:, s - wb:]
            outs["kp"].append(kv_p[..., :A_KV_W].reshape(nb, wb, A_KV_HEADS, HEAD_DIM))
            outs["vp"].append(kv_p[..., A_KV_W:].reshape(nb, wb, A_KV_HEADS, HEAD_DIM))
            outs["ks"].append(new_k.reshape(bd, wb, A_KV_HEADS, HEAD_DIM))
            outs["vs"].append(new_v.reshape(bd, wb, A_KV_HEADS, HEAD_DIM))
            outs["sv"].append(to_b_major(v_rows))
        else:
            j = l // 2
            w_in_p, w_uq_r, wuk_ext, wuv_ext = _prep_odd_weights(odd_w_in[j], mla_w_uq[j], mla_w_uk[j], mla_w_uv[j])
            ci, qn, qp, kc, ckv, kpe = _inproj_odd(x, mix_norm[l], w_in_p, mla_q_norm[j], w_uq_r, mla_kv_norm[j],
                                                   cos_t, sin_t, n_pt, s)
            mix_p, cst_p = _mix_odd_prompt(ci, qn, qp, kc, wuk_ext, wuv_ext, conv_w[j], nb, s)
            qx = _qx_sample(qn, qp, wuk_ext, n_p, n_s)
            qx_b = qx.reshape(MLA_HEADS * ts, bd, KC_W).transpose(1, 0, 2)
            kn_b = jnp.pad(to_b_major(kc[n_p:].astype(F32)), ((0, 0), (0, 8 - ts), (0, 0)))
            o_lat = _mla_sample(page_table, qx_b, kn_b, cache_mla_ckv, pool_kpet, j, ts)
            o_lat_t = o_lat.reshape(bd, MLA_HEADS, ts, KV_LORA).transpose(1, 2, 0, 3).reshape(
                MLA_HEADS, n_s, KV_LORA).astype(BF16)
            mix_s, cst_s = _odd_sample_tail(ci, state_conv[j].transpose(1, 0, 2), conv_w[j], o_lat_t, wuv_ext,
                                            n_p, ts, bd)
            w_out = (w_out_odd, j)
            outs["cp"].append(cst_p[:, 8 - (CONV_WIDTH - 1):])
            outs["cs"].append(cst_s.transpose(1, 0, 2))
            outs["ckp"].append(ckv[:n_p].reshape(nb, s, KV_LORA))
            outs["kpp"].append(kpe[:n_p, :ROPE_DIM].reshape(nb, s, ROPE_DIM))
            outs["cks"].append(to_b_major(ckv[n_p:]))
            outs["kps"].append(to_b_major(kpe[n_p:, :ROPE_DIM]))
        x = _ffn(x, ffn2_norm[l], *w2, l, n_pt, mix=(mix_p, mix_s) + w_out,
                 final_g=final_norm if l == depth - 1 else None)
    y_prompt = x[0].reshape(nb, s, d)
    y_sample = x[1].reshape(ts, bd, d).transpose(1, 0, 2)
    st = lambda k: jnp.stack(outs[k])
    return (y_prompt, y_sample, st("kp"), st("vp"), st("ks"), st("vs"), st("sv"), st("cp"), st("cs"),
            st("ckp"), st("kpp"), st("cks"), st("kps"))
```

```python
import functools
import math

import numpy as np
import jax
import jax.numpy as jnp
from jax import lax
from jax.experimental import pallas as pl
from jax.experimental.pallas import tpu as pltpu

F32 = jnp.float32
BF16 = jnp.bfloat16

HEAD_DIM = 64
A_HEADS = 8
A_KV_HEADS = 2
A_Q_W = A_HEADS * HEAD_DIM
A_KV_W = A_KV_HEADS * HEAD_DIM
WINDOW = 128
N_BUCKETS = 32
MAX_DISTANCE = 128
SG_GROUPS = 4
SG_GROUP_DIM = 128
SG_WIDTH = SG_GROUPS * SG_GROUP_DIM
CHUNK = 128
C_WIDTH = 512
CONV_WIDTH = 3
MLA_HEADS = 8
NOPE_DIM = 64
ROPE_DIM = 32
V_DIM = 64
Q_LORA = 768
KV_LORA = 256
ROPE_THETA = 10000.0
PAGE_SIZE = 128
EPS = 1e-6
NEG = -1e30

LANE = 128
KC_W = KV_LORA + LANE
TM = 512
FC = 512
TQ_EVEN = 512
TQ_ODD = 256
KB_ODD = 512
CHAINS_ODD = 2
BB_EVEN = 16
MLA_SAMPLE_CHUNK = 2048
MLA_SAMPLE_BATCHES = 2
MLA_SAMPLE_VMEM = 60 * 1024 * 1024
VMEM_LIMIT = 56 * 1024 * 1024


def _cp(sem, vmem=VMEM_LIMIT):
    return pltpu.CompilerParams(dimension_semantics=sem, vmem_limit_bytes=vmem)


def _dot(a, b):
    return jnp.dot(a, b, preferred_element_type=F32)


def _dot_nt(a, b):
    return lax.dot_general(a, b, (((1,), (1,)), ((), ())), preferred_element_type=F32)


def _rms(x, g):
    ms = jnp.mean(x * x, axis=-1, keepdims=True)
    return x * lax.rsqrt(ms + EPS) * g


def _gelu(x):
    return 0.5 * x * (1.0 + lax.erf(x * np.float32(math.sqrt(0.5))))


def _const_spec(shape):
    n = len(shape)
    return pl.BlockSpec(shape, lambda *_: (0,) * n, pipeline_mode=pl.Buffered(1))


def _smem_spec():
    return pl.BlockSpec(memory_space=pltpu.SMEM)


def _ffn_chunks(f):
    chunks = [(c, FC) for c in range(0, f - f % FC, FC)]
    if f % FC:
        chunks.append((f - f % FC, f % FC))
    return chunks


def _ffn_kernel(*refs, split_in, has_mix, has_final, n_prompt_tiles, chunks):
    it = iter(refs)
    x_ref = next(it)
    if split_in:
        xs_ref = next(it)
    if has_mix:
        mixp_ref, mixs_ref, wout_ref = next(it), next(it), next(it)
    g_ref, wg_ref, wu_ref, wd_ref = next(it), next(it), next(it), next(it)
    if has_final:
        fg_ref = next(it)
    o_ref = next(it)
    if has_final:
        os_ref = next(it)
    acc_ref = next(it)

    is_sample = pl.program_id(0) >= n_prompt_tiles
    x = x_ref[...]
    if split_in:
        x = jnp.where(is_sample, xs_ref[...], x)
    if has_mix:
        mix = jnp.where(is_sample, mixs_ref[...], mixp_ref[...])
        x = x + _dot(mix, wout_ref[...])
    h = _rms(x, g_ref[...]).astype(BF16)
    for k, (c0, cw) in enumerate(chunks):
        g = _dot(h, wg_ref[:, c0:c0 + cw])
        u = _dot(h, wu_ref[:, c0:c0 + cw])
        a = (g * jax.nn.sigmoid(g) * u).astype(BF16)
        d = _dot(a, wd_ref[c0:c0 + cw, :])
        if k == 0:
            acc_ref[...] = d
        else:
            acc_ref[...] += d
    y = x + 0.5 * acc_ref[...]
    if has_final:
        y = _rms(y, fg_ref[...])

        @pl.when(jnp.logical_not(is_sample))
        def _():
            o_ref[...] = y

        @pl.when(is_sample)
        def _():
            os_ref[...] = y
    else:
        o_ref[...] = y


def _layer_spec(shape, layer):
    n = len(shape)
    return pl.BlockSpec((None,) + tuple(shape), lambda *_: (layer,) + (0,) * n, pipeline_mode=pl.Buffered(1))


def _ffn(x, norm_g, w_gate, w_up, w_down, layer, n_prompt_tiles, mix=None, final_g=None):
    n_tiles = n_prompt_tiles + 1
    last_p = n_prompt_tiles - 1
    row = lambda i: (i, 0)
    prompt_row = lambda i: (jnp.minimum(i, last_p), 0)
    sample_row = lambda i: (0, 0)
    split_in = isinstance(x, tuple)
    d = x[0].shape[1] if split_in else x.shape[1]
    t = n_tiles * TM
    f = w_down.shape[1]
    if split_in:
        args, specs = list(x), [pl.BlockSpec((TM, d), prompt_row), pl.BlockSpec((TM, d), sample_row)]
    else:
        args, specs = [x], [pl.BlockSpec((TM, d), row)]
    if mix is not None:
        mix_p, mix_s, w_out, e = mix
        dm = w_out.shape[1]
        args += [mix_p, mix_s, w_out]
        specs += [pl.BlockSpec((TM, dm), prompt_row), pl.BlockSpec((TM, dm), sample_row),
                  _layer_spec((dm, d), e)]
    args += [norm_g.reshape(1, d), w_gate, w_up, w_down]
    specs += [_const_spec((1, d)), _layer_spec((d, f), layer), _layer_spec((d, f), layer),
              _layer_spec((f, d), layer)]
    if final_g is not None:
        args.append(final_g.reshape(1, d))
        specs.append(_const_spec((1, d)))
    body = functools.partial(_ffn_kernel, split_in=split_in, has_mix=mix is not None,
                             has_final=final_g is not None, n_prompt_tiles=n_prompt_tiles, chunks=_ffn_chunks(f))
    if final_g is not None:
        out_specs = [pl.BlockSpec((TM, d), prompt_row), pl.BlockSpec((TM, d), sample_row)]
        out_shape = [jax.ShapeDtypeStruct((n_prompt_tiles * TM, d), F32), jax.ShapeDtypeStruct((TM, d), F32)]
    else:
        out_specs = pl.BlockSpec((TM, d), row)
        out_shape = jax.ShapeDtypeStruct((t, d), F32)
    return pl.pallas_call(
        body,
        grid=(n_tiles,),
        in_specs=specs,
        out_specs=out_specs,
        out_shape=out_shape,
        scratch_shapes=[pltpu.VMEM((TM, d), F32)],
        compiler_params=_cp(("arbitrary",)),
        name="ffn",
    )(*args)


def _t5_bucket_np(dist):
    n = np.maximum(dist, 0)
    max_exact = N_BUCKETS // 2
    nf = np.maximum(n, 1).astype(np.float32)
    large = max_exact + (np.log(nf / np.float32(max_exact)) / np.float32(math.log(MAX_DISTANCE / max_exact))
                         * np.float32(N_BUCKETS - max_exact)).astype(np.int32)
    return np.where(n < max_exact, n, np.minimum(large, N_BUCKETS - 1)).astype(np.int32)


def _bias_tables_np(ts):
    q = np.arange(WINDOW)[:, None]
    j = np.arange(2 * WINDOW)[None, :]
    bucket_p = _t5_bucket_np(q - j + WINDOW)
    nb = 8 // ts
    cols = nb * WINDOW + 16
    bucket_s = -np.ones((8, cols), np.int32)
    for r in range(8):
        bi, t = divmod(r, ts)
        for c in range(cols):
            if c < nb * WINDOW:
                bj, jj = divmod(c, WINDOW)
                dist = t - (jj - WINDOW)
            elif c < nb * WINDOW + 8:
                bj, tt = divmod(c - nb * WINDOW, ts)
                dist = t - tt
            else:
                continue
            if bj == bi and 0 <= dist <= WINDOW:
                bucket_s[r, c] = _t5_bucket_np(np.array(dist))
    return bucket_p, bucket_s


def _bias_kernel(rb_ref, bp_ref, bs_ref, op_ref, os_ref):
    h = pl.program_id(0)
    bp = bp_ref[...]
    bs = bs_ref[...]
    accp = jnp.zeros(bp.shape, F32)
    accs = jnp.full(bs.shape, NEG, F32)
    for k in range(N_BUCKETS):
        val = rb_ref[k, h]
        accp = jnp.where(bp == k, val, accp)
        accs = jnp.where(bs == k, val, accs)
    op_ref[0] = accp
    os_ref[0] = accs


def _bias_tables(rel_bias, ts):
    bucket_p, bucket_s = _bias_tables_np(ts)
    return pl.pallas_call(
        _bias_kernel,
        grid=(A_HEADS,),
        in_specs=[_smem_spec(),
                  pl.BlockSpec(bucket_p.shape, lambda h: (0, 0)),
                  pl.BlockSpec(bucket_s.shape, lambda h: (0, 0))],
        out_specs=[pl.BlockSpec((1,) + bucket_p.shape, lambda h: (h, 0, 0)),
                   pl.BlockSpec((1,) + bucket_s.shape, lambda h: (h, 0, 0))],
        out_shape=[jax.ShapeDtypeStruct((A_HEADS,) + bucket_p.shape, F32),
                   jax.ShapeDtypeStruct((A_HEADS,) + bucket_s.shape, F32)],
        compiler_params=_cp(("arbitrary",)),
        name="t5_bias",
    )(rel_bias, jnp.asarray(bucket_p), jnp.asarray(bucket_s))


def _inproj_even_kernel(x_ref, g_ref, w_ref, q_ref, kv_ref, ug_ref, last_ref, *, tiles_per_seq, n_prompt_tiles):
    h = _rms(x_ref[...], g_ref[...]).astype(BF16)
    p = _dot(h, w_ref[...])
    q_ref[...] = (p[:, :A_Q_W] * np.float32(HEAD_DIM ** -0.5)).astype(BF16)
    kv = p[:, A_Q_W:A_Q_W + 2 * A_KV_W]
    kv_ref[...] = kv
    ug_ref[...] = p[:, A_Q_W + 2 * A_KV_W:]
    i = pl.program_id(0)

    @pl.when(((i + 1) % tiles_per_seq == 0) & (i < n_prompt_tiles))
    def _():
        last_ref[...] = kv[TM - WINDOW:]


def _inproj_even(x, norm_g, w_in, n_prompt_tiles, s):
    t, d = x.shape
    n = w_in.shape[1]
    row = lambda i: (i, 0)
    per_seq = s // TM
    nb = n_prompt_tiles // per_seq
    return pl.pallas_call(
        functools.partial(_inproj_even_kernel, tiles_per_seq=per_seq, n_prompt_tiles=n_prompt_tiles),
        grid=(t // TM,),
        in_specs=[pl.BlockSpec((TM, d), row), _const_spec((1, d)), _const_spec((d, n))],
        out_specs=[pl.BlockSpec((TM, A_Q_W), row), pl.BlockSpec((TM, 2 * A_KV_W), row),
                   pl.BlockSpec((TM, 2 * SG_WIDTH), row),
                   pl.BlockSpec((None, WINDOW, 2 * A_KV_W), lambda i: (jnp.minimum(i // per_seq, nb - 1), 0, 0))],
        out_shape=[jax.ShapeDtypeStruct((t, A_Q_W), BF16), jax.ShapeDtypeStruct((t, 2 * A_KV_W), F32),
                   jax.ShapeDtypeStruct((t, 2 * SG_WIDTH), F32),
                   jax.ShapeDtypeStruct((nb, WINDOW, 2 * A_KV_W), F32)],
        compiler_params=_cp(("arbitrary",)),
        name="inproj_even",
    )(x, norm_g.reshape(1, d), w_in)


def _head_variants(x):
    lo = lax.broadcasted_iota(jnp.int32, x.shape, 1) < HEAD_DIM
    xr = pltpu.roll(x, HEAD_DIM, 1)
    z = jnp.zeros_like(x)
    return [[jnp.where(lo, x, z).astype(BF16), jnp.where(lo, z, xr).astype(BF16)],
            [jnp.where(lo, xr, z).astype(BF16), jnp.where(lo, z, x).astype(BF16)]]


def _sink_softmax_pv(s, sink, v):
    mx = jnp.maximum(jnp.max(s, axis=-1, keepdims=True), sink)
    p = jnp.exp(s - mx)
    den = jnp.sum(p, axis=-1, keepdims=True) + jnp.exp(sink - mx)
    return _dot(p.astype(BF16), v) * (1.0 / den)


def _layer_norm(x, g, b):
    xc = x - jnp.mean(x, axis=-1, keepdims=True)
    y = xc * lax.rsqrt(jnp.mean(xc * xc, axis=-1, keepdims=True) + EPS)
    return y * g + b


def _even_prompt_kernel(q_ref, kv_ref, kvp_ref, ug_ref, bias_ref, sink_ref, lng_ref, lnb_ref, ws_ref,
                        bst_ref, o_ref, *, tq):
    nblk = tq // WINDOW
    step = pl.program_id(1)
    kv_all = jnp.concatenate([kvp_ref[...], kv_ref[...]], axis=0)
    kvar = _head_variants(kv_all[:, :A_KV_W])
    vvar = _head_variants(kv_all[:, A_KV_W:])
    qi = lax.broadcasted_iota(jnp.int32, (WINDOW, 2 * WINDOW), 0)
    kj = lax.broadcasted_iota(jnp.int32, (WINDOW, 2 * WINDOW), 1)
    dist = qi - kj + WINDOW
    band = (dist >= 0) & (dist <= WINDOW)
    cur = kj >= WINDOW
    ri = lax.broadcasted_iota(jnp.int32, (CHUNK, CHUNK), 0)
    ci = lax.broadcasted_iota(jnp.int32, (CHUNK, CHUNK), 1)
    wt = [jnp.where(ri >= ci, ws_ref[g], 0.0).astype(BF16) for g in range(SG_GROUPS)]
    lng, lnb = lng_ref[...], lnb_ref[...]
    for n in range(nblk):
        r0 = n * WINDOW
        has_prev = (step * nblk + n) > 0
        mask = band & (cur | has_prev)
        for m in range(A_HEADS // 2):
            kvh = (2 * m) // (A_HEADS // A_KV_HEADS)
            qp = q_ref[r0:r0 + WINDOW, LANE * m:LANE * (m + 1)]
            o = None
            for half in range(2):
                hd = 2 * m + half
                s = _dot_nt(qp, kvar[kvh][half][r0:r0 + 2 * WINDOW])
                s = jnp.where(mask, s + bias_ref[hd], NEG)
                oh = _sink_softmax_pv(s, sink_ref[hd], vvar[kvh][half][r0:r0 + 2 * WINDOW])
                o = oh if o is None else o + oh
            o_ref[r0:r0 + WINDOW, LANE * m:LANE * (m + 1)] = o.astype(BF16)
        u = _gelu(ug_ref[r0:r0 + WINDOW, :SG_WIDTH])
        v = _layer_norm(_gelu(ug_ref[r0:r0 + WINDOW, SG_WIDTH:]), lng, lnb)
        for g in range(SG_GROUPS):
            sl = slice(SG_GROUP_DIM * g, SG_GROUP_DIM * (g + 1))
            y = _dot(wt[g], v[:, sl].astype(BF16)) + bst_ref[:, g:g + 1]
            o_ref[r0:r0 + WINDOW, A_Q_W + SG_GROUP_DIM * g:A_Q_W + SG_GROUP_DIM * (g + 1)] = (
                u[:, sl] * y).astype(BF16)


def _mix_even_prompt(q, kv, ug, bias_p, sinks, ln_g, ln_b, w_s, b_s, nb, s):
    tq = TQ_EVEN
    nq = s // tq
    blk = lambda b, i: (b * nq + i, 0)
    prev = lambda b, i: (jnp.maximum((b * nq + i) * (tq // WINDOW) - 1, 0), 0)
    return pl.pallas_call(
        functools.partial(_even_prompt_kernel, tq=tq),
        grid=(nb, nq),
        in_specs=[pl.BlockSpec((tq, A_Q_W), blk), pl.BlockSpec((tq, 2 * A_KV_W), blk),
                  pl.BlockSpec((WINDOW, 2 * A_KV_W), prev), pl.BlockSpec((tq, 2 * SG_WIDTH), blk),
                  _const_spec(bias_p.shape), _smem_spec(), _const_spec((1, SG_WIDTH)),
                  _const_spec((1, SG_WIDTH)), _const_spec(w_s.shape), _const_spec((CHUNK, SG_GROUPS))],
        out_specs=pl.BlockSpec((tq, A_Q_W + SG_WIDTH), blk),
        out_shape=jax.ShapeDtypeStruct((nb * s, A_Q_W + SG_WIDTH), BF16),
        compiler_params=_cp(("arbitrary", "arbitrary")),
        name="mix_even_prompt",
    )(q, kv, kv, ug, bias_p, sinks, ln_g.reshape(1, -1), ln_b.reshape(1, -1), w_s, b_s.T)


def _even_sample_kernel(q_ref, kvn_ref, ck_ref, cv_ref, bias_ref, sink_ref, a_ref, nk_ref, nv_ref, *, ts):
    bb = ck_ref.shape[0]
    gpb = 8 // ts
    wb = ck_ref.shape[1]
    n_cache = gpb * wb
    hpk = A_HEADS // A_KV_HEADS
    zpad = jnp.zeros((8, A_KV_W), F32)

    def lo_half(x, kvh):
        lo = lax.broadcasted_iota(jnp.int32, x.shape, 1) < HEAD_DIM
        return jnp.where(lo, x if kvh == 0 else pltpu.roll(x, HEAD_DIM, 1), 0.0).astype(BF16)

    hrow = lax.broadcasted_iota(jnp.int32, (8 * hpk, 1), 0) // 8
    bias_c, bias_n, sink = [], [], []
    for kvh in range(A_KV_HEADS):
        heads = [hpk * kvh + i for i in range(hpk)]
        bias_c.append(jnp.concatenate([bias_ref[h][:, :n_cache] for h in heads], axis=0))
        bias_n.append(jnp.concatenate([bias_ref[h][:, n_cache:] for h in heads], axis=0))
        sk = jnp.full((8 * hpk, 1), sink_ref[heads[0]], F32)
        for i in range(1, hpk):
            sk = jnp.where(hrow == i, sink_ref[heads[i]], sk)
        sink.append(sk)
    for g in range(bb // gpb):
        r0 = 8 * g
        qg = q_ref[r0:r0 + 8, :]
        kn = kvn_ref[r0:r0 + 8, :A_KV_W]
        vn = kvn_ref[r0:r0 + 8, A_KV_W:]
        kc = jnp.concatenate([ck_ref[gpb * g + i] for i in range(gpb)], axis=0)
        vc = jnp.concatenate([cv_ref[gpb * g + i] for i in range(gpb)], axis=0)
        knp = jnp.concatenate([kn, zpad], axis=0)
        vnp = jnp.concatenate([vn, zpad], axis=0)
        for kvh in range(A_KV_HEADS):
            c0 = 2 * LANE * kvh
            qs = jnp.concatenate([lo_half(qg[:, c0 + LANE * (i // 2):c0 + LANE * (i // 2 + 1)], i % 2)
                                  for i in range(hpk)], axis=0)
            sc = _dot_nt(qs, lo_half(kc, kvh))
            sn = _dot_nt(qs, lo_half(knp, kvh))
            sc = jnp.where(bias_c[kvh] > 0.5 * NEG, sc + bias_c[kvh], NEG)
            sn = jnp.where(bias_n[kvh] > 0.5 * NEG, sn + bias_n[kvh], NEG)
            mx = jnp.maximum(jnp.maximum(jnp.max(sc, axis=-1, keepdims=True), jnp.max(sn, axis=-1, keepdims=True)),
                             sink[kvh])
            pc = jnp.exp(sc - mx)
            pn = jnp.exp(sn - mx)
            den = (jnp.sum(pc, axis=-1, keepdims=True) + jnp.sum(pn, axis=-1, keepdims=True)
                   + jnp.exp(sink[kvh] - mx))
            o = (_dot(pc.astype(BF16), lo_half(vc, kvh)) + _dot(pn.astype(BF16), lo_half(vnp, kvh))) * (1.0 / den)
            for i in range(hpk // 2):
                pair = o[16 * i:16 * i + 8] + pltpu.roll(o[16 * i + 8:16 * i + 16], HEAD_DIM, 1)
                a_ref[r0:r0 + 8, c0 + LANE * i:c0 + LANE * (i + 1)] = pair
        for i in range(gpb):
            b = gpb * g + i
            nk_ref[b, 0:wb - ts, :] = ck_ref[b, ts:wb, :]
            nk_ref[b, wb - ts:wb, :] = kn[ts * i:ts * (i + 1)]
            nv_ref[b, 0:wb - ts, :] = cv_ref[b, ts:wb, :]
            nv_ref[b, wb - ts:wb, :] = vn[ts * i:ts * (i + 1)]


def _mix_even_sample(q_b, kvn_b, cache_k, cache_v, bias_s, sinks, ts):
    bd, wb, _ = cache_k.shape
    bb = BB_EVEN
    rows = bb * ts
    blk2 = lambda i: (i, 0)
    blk3 = lambda i: (i, 0, 0)
    return pl.pallas_call(
        functools.partial(_even_sample_kernel, ts=ts),
        grid=(bd // bb,),
        in_specs=[pl.BlockSpec((rows, A_Q_W), blk2), pl.BlockSpec((rows, 2 * A_KV_W), blk2),
                  pl.BlockSpec((bb, wb, A_KV_W), blk3), pl.BlockSpec((bb, wb, A_KV_W), blk3),
                  _const_spec(bias_s.shape), _smem_spec()],
        out_specs=[pl.BlockSpec((rows, A_Q_W), blk2), pl.BlockSpec((bb, wb, A_KV_W), blk3),
                   pl.BlockSpec((bb, wb, A_KV_W), blk3)],
        out_shape=[jax.ShapeDtypeStruct((bd * ts, A_Q_W), F32),
                   jax.ShapeDtypeStruct(cache_k.shape, F32), jax.ShapeDtypeStruct(cache_v.shape, F32)],
        compiler_params=_cp(("arbitrary",)),
        name="mix_even_sample",
    )(q_b, kvn_b, cache_k, cache_v, bias_s, sinks)


def _sgu_sample_kernel(ug_ref, lng_ref, lnb_ref, w_ref, b_ref, sg_ref, v_ref, *, ts, bd):
    u = _gelu(ug_ref[:, :SG_WIDTH])
    v = _layer_norm(_gelu(ug_ref[:, SG_WIDTH:]), lng_ref[...], lnb_ref[...])
    v_ref[...] = v
    for i in range(ts):
        for g in range(SG_GROUPS):
            sl = slice(SG_GROUP_DIM * g, SG_GROUP_DIM * (g + 1))
            y = jnp.full((bd, SG_GROUP_DIM), b_ref[g, i], F32)
            for j in range(i + 1):
                y = y + w_ref[g, i * ts + j] * v[bd * j:bd * (j + 1), sl]
            sg_ref[bd * i:bd * (i + 1), sl] = (u[bd * i:bd * (i + 1), sl] * y).astype(BF16)


def _sgu_sample(ug, ln_g, ln_b, w_s, b_s, n_prompt_rows, ts, bd):
    rows = ts * bd
    off = n_prompt_rows // rows
    w_small = w_s[:, :ts, :ts].reshape(SG_GROUPS, ts * ts)
    b_small = b_s[:, :ts]
    return pl.pallas_call(
        functools.partial(_sgu_sample_kernel, ts=ts, bd=bd),
        grid=(1,),
        in_specs=[pl.BlockSpec((rows, 2 * SG_WIDTH), lambda i: (off, 0)), _const_spec((1, SG_WIDTH)),
                  _const_spec((1, SG_WIDTH)), _smem_spec(), _smem_spec()],
        out_specs=[pl.BlockSpec((rows, SG_WIDTH), lambda i: (0, 0)),
                   pl.BlockSpec((rows, SG_WIDTH), lambda i: (0, 0))],
        out_shape=[jax.ShapeDtypeStruct((rows, SG_WIDTH), BF16), jax.ShapeDtypeStruct((rows, SG_WIDTH), F32)],
        compiler_params=_cp(("arbitrary",)),
        name="sgu_sample",
    )(ug, ln_g.reshape(1, -1), ln_b.reshape(1, -1), w_small, b_small)


def _rope_tables_np(s, past, ts, bd):
    half = ROPE_DIM // 2
    inv = np.exp(-math.log(ROPE_THETA) * np.arange(half, dtype=np.float32) / np.float32(half)).astype(np.float32)
    pos = np.concatenate([np.arange(s), past + np.repeat(np.arange(ts), bd)]).astype(np.float32)
    ang = (pos[:, None] * inv[None, :]).astype(np.float32).astype(np.float64)
    cos = np.cos(ang)
    sin = np.sin(ang)
    cos_t = np.tile(np.concatenate([cos, cos], axis=1), (1, MLA_HEADS)).astype(np.float32)
    sin_t = np.tile(np.concatenate([-sin, sin], axis=1), (1, MLA_HEADS)).astype(np.float32)
    return cos_t, sin_t


def _rope(x, cos_t, sin_t):
    n = x.shape[1]
    half = ROPE_DIM // 2
    first = (lax.broadcasted_iota(jnp.int32, x.shape, 1) % ROPE_DIM) < half
    partner = jnp.where(first, pltpu.roll(x, n - half, 1), pltpu.roll(x, half, 1))
    return x * cos_t + partner * sin_t


def _inproj_odd_kernel(x_ref, g_ref, w_ref, qg_ref, wuq_ref, kvg_ref, cos_ref, sin_ref,
                       ci_ref, qn_ref, qp_ref, kc_ref, ckvp_ref, kpep_ref, ckvs_ref, kpes_ref, *, n_prompt_tiles):
    h = _rms(x_ref[...], g_ref[...]).astype(BF16)
    p = _dot(h, w_ref[...])
    o_q = 3 * C_WIDTH
    o_kv = o_q + Q_LORA
    o_pe = o_kv + KV_LORA
    ci_ref[...] = p[:, :o_q].astype(BF16)
    scale = np.float32((NOPE_DIM + ROPE_DIM) ** -0.5)
    q = _dot(_rms(p[:, o_q:o_kv], qg_ref[...]).astype(BF16), wuq_ref[...])
    n_nope = MLA_HEADS * NOPE_DIM
    cos_t, sin_t = cos_ref[...], sin_ref[...]
    qn_ref[...] = (q[:, :n_nope] * scale).astype(BF16)
    qp_ref[...] = (_rope(q[:, n_nope:], cos_t, sin_t) * scale).astype(BF16)
    ckv = _rms(p[:, o_kv:o_pe], kvg_ref[...])
    kpe = _rope(p[:, o_pe:], cos_t[:, :LANE], sin_t[:, :LANE])
    kc_ref[:, :KV_LORA] = ckv.astype(BF16)
    kc_ref[:, KV_LORA:] = kpe.astype(BF16)
    is_sample = pl.program_id(0) >= n_prompt_tiles

    @pl.when(jnp.logical_not(is_sample))
    def _():
        ckvp_ref[...] = ckv
        kpep_ref[...] = kpe[:, :ROPE_DIM]

    @pl.when(is_sample)
    def _():
        ckvs_ref[...] = ckv
        kpes_ref[...] = kpe[:, :ROPE_DIM]


def _inproj_odd(x, norm_g, w_in, q_norm, w_uq, kv_norm, cos_t, sin_t, n_prompt_tiles, s):
    t, d = x.shape
    n = w_in.shape[1]
    row = lambda i: (i, 0)
    last_p = n_prompt_tiles - 1
    prompt_row = lambda i: (jnp.minimum(i, last_p), 0)
    sample_row = lambda i: (0, 0)
    per_seq = s // TM
    tab = lambda i: (jnp.where(i < n_prompt_tiles, i % per_seq, per_seq), 0)
    n_pe = MLA_HEADS * ROPE_DIM
    n_p = n_prompt_tiles * TM
    widths = [(3 * C_WIDTH, BF16), (MLA_HEADS * NOPE_DIM, BF16), (n_pe, BF16), (KC_W, BF16)]
    return pl.pallas_call(
        functools.partial(_inproj_odd_kernel, n_prompt_tiles=n_prompt_tiles),
        grid=(t // TM,),
        in_specs=[pl.BlockSpec((TM, d), row), _const_spec((1, d)), _const_spec((d, n)),
                  _const_spec((1, Q_LORA)), _const_spec(w_uq.shape), _const_spec((1, KV_LORA)),
                  pl.BlockSpec((TM, n_pe), tab), pl.BlockSpec((TM, n_pe), tab)],
        out_specs=[pl.BlockSpec((TM, w), row) for w, _ in widths] + [
            pl.BlockSpec((TM, KV_LORA), prompt_row), pl.BlockSpec((TM, ROPE_DIM), prompt_row),
            pl.BlockSpec((TM, KV_LORA), sample_row), pl.BlockSpec((TM, ROPE_DIM), sample_row)],
        out_shape=[jax.ShapeDtypeStruct((t, w), dt) for w, dt in widths] + [
            jax.ShapeDtypeStruct((n_p, KV_LORA), F32), jax.ShapeDtypeStruct((n_p, ROPE_DIM), F32),
            jax.ShapeDtypeStruct((TM, KV_LORA), F32), jax.ShapeDtypeStruct((TM, ROPE_DIM), F32)],
        compiler_params=_cp(("arbitrary",)),
        name="inproj_odd",
    )(x, norm_g.reshape(1, d), w_in, q_norm.reshape(1, -1), w_uq, kv_norm.reshape(1, -1), cos_t, sin_t)


def _build_qx(qn_ref, qp_ref, wuk_ref, qx_ref, rows):
    qp = qp_ref[...].astype(F32)
    keep = lax.broadcasted_iota(jnp.int32, (rows, LANE), 1) < ROPE_DIM
    n_pe = MLA_HEADS * ROPE_DIM
    for h in range(MLA_HEADS):
        pair = qn_ref[:, LANE * (h // 2):LANE * (h // 2 + 1)]
        qx_ref[h * rows:(h + 1) * rows, :KV_LORA] = _dot(pair, wuk_ref[h]).astype(BF16)
        pe = qp if h == 0 else pltpu.roll(qp, n_pe - ROPE_DIM * h, 1)
        qx_ref[h * rows:(h + 1) * rows, KV_LORA:] = jnp.where(keep, pe[:, :LANE], 0.0).astype(BF16)


def _odd_prompt_kernel(ci_ref, cih_ref, qn_ref, qp_ref, kc_ref, wuk_ref, wuv_ref, cw_ref,
                       o_ref, cst_ref, qx_ref, m_ref, l_ref, acc_ref, z_ref, *, tq, kb):
    step = pl.program_id(1)
    halo = 16
    zh = cih_ref[:, C_WIDTH:2 * C_WIDTH].astype(F32) * cih_ref[:, 2 * C_WIDTH:].astype(F32)
    z_ref[0:halo, :] = jnp.where(step > 0, zh, 0.0)
    z = ci_ref[:, C_WIDTH:2 * C_WIDTH].astype(F32) * ci_ref[:, 2 * C_WIDTH:].astype(F32)
    z_ref[halo:halo + tq, :] = z
    cw = cw_ref[...]
    y = cw[0:1] * z_ref[halo - 2:halo - 2 + tq, :] + cw[1:2] * z_ref[halo - 1:halo - 1 + tq, :] + cw[2:3] * z
    o_ref[:, :C_WIDTH] = (ci_ref[:, :C_WIDTH].astype(F32) * y).astype(BF16)
    cst_ref[0] = z[tq - 8:tq]

    _build_qx(qn_ref, qp_ref, wuk_ref, qx_ref, tq)
    rows = MLA_HEADS * tq
    m_ref[...] = jnp.full((rows, LANE), NEG, F32)
    l_ref[...] = jnp.zeros((rows, LANE), F32)
    acc_ref[...] = jnp.zeros((rows, KV_LORA), F32)
    n_full = (step * tq) // kb
    hpc = MLA_HEADS // CHAINS_ODD

    def kv_step(j, width, masked):
        k0 = pl.multiple_of(j * kb, kb)
        kc = kc_ref[pl.ds(k0, width), :]
        n_grp = width // LANE
        if masked:
            qpos = step * tq + lax.broadcasted_iota(jnp.int32, (tq, LANE), 0)
            kpos = k0 + lax.broadcasted_iota(jnp.int32, (tq, LANE), 1)
            visible = [kpos + LANE * g <= qpos for g in range(n_grp)]
        for c in range(CHAINS_ODD):
            rc = slice(c * hpc * tq, (c + 1) * hpc * tq)
            s_all = _dot_nt(qx_ref[rc, :], kc)
            ps, alphas = [], []
            for hh in range(hpc):
                r = slice((c * hpc + hh) * tq, (c * hpc + hh + 1) * tq)
                sg = [s_all[hh * tq:(hh + 1) * tq, LANE * g:LANE * (g + 1)] for g in range(n_grp)]
                if masked:
                    sg = [jnp.where(visible[g], sg[g], NEG) for g in range(n_grp)]
                m_old = m_ref[r, :]
                m_new = jnp.maximum(m_old, jnp.max(functools.reduce(jnp.maximum, sg), axis=-1, keepdims=True))
                alpha = jnp.exp(m_old - m_new)
                pg = [jnp.exp(x - m_new) for x in sg]
                l_ref[r, :] = alpha * l_ref[r, :] + jnp.sum(functools.reduce(jnp.add, pg), axis=-1, keepdims=True)
                m_ref[r, :] = m_new
                ps.append(jnp.concatenate([x.astype(BF16) for x in pg], axis=1))
                alphas.append(jnp.concatenate([alpha] * (KV_LORA // LANE), axis=1))
            pv = _dot(jnp.concatenate(ps, axis=0), kc[:, :KV_LORA])
            acc_ref[rc, :] = jnp.concatenate(alphas, axis=0) * acc_ref[rc, :] + pv

    def full_step(j, carry):
        kv_step(j, kb, False)
        return carry

    lax.fori_loop(0, n_full, full_step, 0)
    for i, width in enumerate(range(tq, kb + 1, tq)):
        @pl.when(step % (kb // tq) == i)
        def _(width=width):
            kv_step(n_full, width, True)

    for m in range(MLA_HEADS // 2):
        o = None
        for half in range(2):
            h = 2 * m + half
            inv_l = 1.0 / l_ref[h * tq:(h + 1) * tq, :]
            ol = (acc_ref[h * tq:(h + 1) * tq, :] * jnp.concatenate([inv_l] * (KV_LORA // LANE), axis=1)).astype(BF16)
            oh = _dot(ol, wuv_ref[h])
            o = oh if o is None else o + oh
        o_ref[:, C_WIDTH + LANE * m:C_WIDTH + LANE * (m + 1)] = o.astype(BF16)


def _mix_odd_prompt(ci, qn, qp, kc, wuk_ext, wuv_ext, conv_w, nb, s):
    tq, kb = TQ_ODD, KB_ODD
    nq = s // tq
    blk = lambda b, i: (b * nq + i, 0)
    halo = lambda b, i: (jnp.maximum((b * nq + i) * (tq // 16) - 1, 0), 0)
    rows = MLA_HEADS * tq
    n_out = C_WIDTH + MLA_HEADS * V_DIM
    return pl.pallas_call(
        functools.partial(_odd_prompt_kernel, tq=tq, kb=kb),
        grid=(nb, nq),
        in_specs=[pl.BlockSpec((tq, 3 * C_WIDTH), blk), pl.BlockSpec((16, 3 * C_WIDTH), halo),
                  pl.BlockSpec((tq, MLA_HEADS * NOPE_DIM), blk), pl.BlockSpec((tq, MLA_HEADS * ROPE_DIM), blk),
                  pl.BlockSpec((s, KC_W), lambda b, i: (b, 0)),
                  _const_spec(wuk_ext.shape), _const_spec(wuv_ext.shape), _const_spec(conv_w.shape)],
        out_specs=[pl.BlockSpec((tq, n_out), blk), pl.BlockSpec((1, 8, C_WIDTH), lambda b, i: (b, 0, 0))],
        out_shape=[jax.ShapeDtypeStruct((nb * s, n_out), BF16), jax.ShapeDtypeStruct((nb, 8, C_WIDTH), F32)],
        scratch_shapes=[pltpu.VMEM((rows, KC_W), BF16),
                        pltpu.VMEM((rows, LANE), F32), pltpu.VMEM((rows, LANE), F32), pltpu.VMEM((rows, KV_LORA), F32),
                        pltpu.VMEM((tq + 16, C_WIDTH), F32)],
        compiler_params=_cp(("arbitrary", "arbitrary")),
        name="mix_odd_prompt",
    )(ci, ci, qn, qp, kc, wuk_ext, wuv_ext, conv_w)


def _qx_sample_kernel(qn_ref, qp_ref, wuk_ref, qx_ref, *, rows):
    _build_qx(qn_ref, qp_ref, wuk_ref, qx_ref, rows)


def _qx_sample(qn, qp, wuk_ext, n_prompt_rows, rows):
    off = n_prompt_rows // rows
    return pl.pallas_call(
        functools.partial(_qx_sample_kernel, rows=rows),
        grid=(1,),
        in_specs=[pl.BlockSpec((rows, MLA_HEADS * NOPE_DIM), lambda i: (off, 0)),
                  pl.BlockSpec((rows, MLA_HEADS * ROPE_DIM), lambda i: (off, 0)), _const_spec(wuk_ext.shape)],
        out_specs=pl.BlockSpec((MLA_HEADS * rows, KC_W), lambda i: (0, 0)),
        out_shape=jax.ShapeDtypeStruct((MLA_HEADS * rows, KC_W), BF16),
        compiler_params=_cp(("arbitrary",)),
        name="qx_sample",
    )(qn, qp, wuk_ext)


def _page_copies(pt_ref, pool_ckv, pool_kpet, ckv_buf, kpe_buf, sem, layer, n_pages, step, slot):
    copies = []
    for i in range(MLA_SAMPLE_BATCHES):
        b = step * MLA_SAMPLE_BATCHES + i
        for p in range(n_pages):
            page = pt_ref[b, p]
            keys = pl.ds(p * PAGE_SIZE, PAGE_SIZE)
            copies.append(pltpu.make_async_copy(pool_ckv.at[layer, page], ckv_buf.at[slot, i, keys],
                                                sem.at[0, slot]))
            copies.append(pltpu.make_async_copy(pool_kpet.at[layer, page], kpe_buf.at[slot, i, :, keys],
                                                sem.at[1, slot]))
    return copies


def _mla_sample_kernel(pt_ref, qx_ref, kn_ref, pool_ckv, pool_kpet, o_ref, ckv_buf, kpe_buf, sem, *,
                       ts, n_pages, layer, chunk):
    step = pl.program_id(0)
    n_steps = pl.num_programs(0)
    slot = step % 2
    copies = functools.partial(_page_copies, pt_ref, pool_ckv, pool_kpet, ckv_buf, kpe_buf, sem, layer, n_pages)

    @pl.when(step == 0)
    def _():
        for c in copies(0, 0):
            c.start()

    @pl.when(step + 1 < n_steps)
    def _():
        for c in copies(step + 1, 1 - slot):
            c.start()

    for c in copies(step, slot):
        c.wait()

    rows = qx_ref.shape[1]
    t_q = lax.broadcasted_iota(jnp.int32, (rows, 8), 0) % ts
    t_k = lax.broadcasted_iota(jnp.int32, (rows, 8), 1)
    past = n_pages * PAGE_SIZE
    starts = range(0, past, chunk)
    for i in range(MLA_SAMPLE_BATCHES):
        qx = qx_ref[i]
        q_lat = qx[:, :KV_LORA]
        q_pe = qx[:, KV_LORA:KV_LORA + ROPE_DIM]
        kn = kn_ref[i].astype(BF16)
        sn = jnp.where(t_k <= t_q, _dot_nt(qx, kn), NEG)
        kcs = [ckv_buf[slot, i, k0:k0 + chunk, :].astype(BF16) for k0 in starts]
        ss = [_dot_nt(q_lat, kc) + _dot(q_pe, kpe_buf[slot, i, :, k0:k0 + chunk].astype(BF16))
              for k0, kc in zip(starts, kcs)]
        m = functools.reduce(jnp.maximum, [jnp.max(s, axis=-1, keepdims=True) for s in ss],
                             jnp.max(sn, axis=-1, keepdims=True))
        pn = jnp.exp(sn - m)
        den = jnp.sum(pn, axis=-1, keepdims=True)
        acc = _dot(pn.astype(BF16), kn[:, :KV_LORA])
        for s, kc in zip(ss, kcs):
            p = jnp.exp(s - m)
            den = den + jnp.sum(p, axis=-1, keepdims=True)
            acc = acc + _dot(p.astype(BF16), kc)
        o_ref[i] = acc * (1.0 / den)


def _mla_sample(page_table, qx_b, kn_b, pool_ckv, pool_kpet, layer, ts):
    bd, n_pages = page_table.shape
    past = n_pages * PAGE_SIZE
    rows = MLA_HEADS * ts
    nbs = MLA_SAMPLE_BATCHES
    blk = lambda g, pt: (g, 0, 0)
    grid_spec = pltpu.PrefetchScalarGridSpec(
        num_scalar_prefetch=1,
        grid=(bd // nbs,),
        in_specs=[pl.BlockSpec((nbs, rows, KC_W), blk), pl.BlockSpec((nbs, 8, KC_W), blk),
                  pl.BlockSpec(memory_space=pl.ANY), pl.BlockSpec(memory_space=pl.ANY)],
        out_specs=pl.BlockSpec((nbs, rows, KV_LORA), blk),
        scratch_shapes=[pltpu.VMEM((2, nbs, past, KV_LORA), F32), pltpu.VMEM((2, nbs, ROPE_DIM, past), F32),
                        pltpu.SemaphoreType.DMA((2, 2))],
    )
    return pl.pallas_call(
        functools.partial(_mla_sample_kernel, ts=ts, n_pages=n_pages, layer=layer,
                          chunk=min(MLA_SAMPLE_CHUNK, past)),
        grid_spec=grid_spec,
        out_shape=jax.ShapeDtypeStruct((bd, rows, KV_LORA), F32),
        compiler_params=_cp(("arbitrary",), vmem=MLA_SAMPLE_VMEM),
        name="mla_sample",
    )(page_table, qx_b, kn_b, pool_ckv, pool_kpet)


def _odd_sample_tail_kernel(ci_ref, st_ref, cw_ref, ol_ref, wuv_ref, o_ref, nst_ref, *, ts, bd):
    cw = cw_ref[...]
    zz = [st_ref[0], st_ref[1]]
    for t in range(ts):
        r = slice(bd * t, bd * (t + 1))
        zz.append(ci_ref[r, C_WIDTH:2 * C_WIDTH].astype(F32) * ci_ref[r, 2 * C_WIDTH:].astype(F32))
    for t in range(ts):
        r = slice(bd * t, bd * (t + 1))
        y = cw[0:1] * zz[t] + cw[1:2] * zz[t + 1] + cw[2:3] * zz[t + 2]
        o_ref[r, :C_WIDTH] = (ci_ref[r, :C_WIDTH].astype(F32) * y).astype(BF16)
    nst_ref[0] = zz[ts]
    nst_ref[1] = zz[ts + 1]
    for m in range(MLA_HEADS // 2):
        o = _dot(ol_ref[2 * m], wuv_ref[2 * m]) + _dot(ol_ref[2 * m + 1], wuv_ref[2 * m + 1])
        o_ref[:, C_WIDTH + LANE * m:C_WIDTH + LANE * (m + 1)] = o.astype(BF16)


def _odd_sample_tail(ci, state_t, conv_w, o_lat_t, wuv_ext, n_prompt_rows, ts, bd):
    rows = ts * bd
    off = n_prompt_rows // rows
    n_out = C_WIDTH + MLA_HEADS * V_DIM
    return pl.pallas_call(
        functools.partial(_odd_sample_tail_kernel, ts=ts, bd=bd),
        grid=(1,),
        in_specs=[pl.BlockSpec((rows, 3 * C_WIDTH), lambda i: (off, 0)), _const_spec(state_t.shape),
                  _const_spec(conv_w.shape), _const_spec(o_lat_t.shape), _const_spec(wuv_ext.shape)],
        out_specs=[pl.BlockSpec((rows, n_out), lambda i: (0, 0)),
                   pl.BlockSpec(state_t.shape, lambda i: (0, 0, 0))],
        out_shape=[jax.ShapeDtypeStruct((rows, n_out), BF16), jax.ShapeDtypeStruct(state_t.shape, F32)],
        compiler_params=_cp(("arbitrary",)),
        name="odd_sample_tail",
    )(ci, state_t, conv_w, o_lat_t, wuv_ext)


def _prep_odd_weights(w_in, w_uq, w_uk, w_uv):
    d, n = w_in.shape
    n_pad = -n % LANE
    w_in_p = jnp.pad(w_in, ((0, 0), (0, n_pad))).astype(BF16)
    per = NOPE_DIM + ROPE_DIM
    wq = w_uq.reshape(Q_LORA, MLA_HEADS, per)
    w_uq_r = jnp.concatenate([wq[:, :, :NOPE_DIM].reshape(Q_LORA, -1), wq[:, :, NOPE_DIM:].reshape(Q_LORA, -1)],
                             axis=1).astype(BF16)
    ukt = jnp.transpose(w_uk, (1, 2, 0))
    uvt = jnp.transpose(w_uv, (1, 0, 2))
    wuk_ext, wuv_ext = [], []
    for h in range(MLA_HEADS):
        lo = (h % 2) * NOPE_DIM
        wuk_ext.append(jnp.pad(ukt[h], ((lo, LANE - NOPE_DIM - lo), (0, 0))))
        wuv_ext.append(jnp.pad(uvt[h], ((0, 0), (lo, LANE - V_DIM - lo))))
    return w_in_p, w_uq_r, jnp.stack(wuk_ext).astype(BF16), jnp.stack(wuv_ext).astype(BF16)


def kernel(x_prompt, x_sample, cache_swa_k, cache_swa_v, state_conv, cache_mla_ckv, cache_mla_kpe, page_table,
           rel_bias, ffn1_norm, ffn1_w_gate, ffn1_w_up, ffn1_w_down, mix_norm, ffn2_norm, ffn2_w_gate,
           ffn2_w_up, ffn2_w_down, even_w_in, even_w_out, attn_sinks, sgu_ln_g, sgu_ln_b, sgu_w, sgu_b,
           odd_w_in, odd_w_out, conv_w, mla_q_norm, mla_w_uq, mla_kv_norm, mla_w_uk, mla_w_uv, final_norm):
    nb, s, d = x_prompt.shape
    bd, ts, _ = x_sample.shape
    depth = ffn1_norm.shape[0]
    n_p, n_s = nb * s, bd * ts
    past = page_table.shape[1] * PAGE_SIZE
    wb = cache_swa_k.shape[2]
    assert n_s == TM and n_p % TM == 0 and s % TM == 0, "sample group must fill exactly one token tile"
    assert 8 % ts == 0 and bd % BB_EVEN == 0 and bd % 8 == 0 and wb == WINDOW
    assert s % KB_ODD == 0 and s % TQ_EVEN == 0 and KB_ODD % TQ_ODD == 0
    n_pt = n_p // TM

    def to_b_major(a):
        return a.reshape(ts, bd, -1).transpose(1, 0, 2)

    x = (x_prompt.reshape(n_p, d), x_sample.transpose(1, 0, 2).reshape(n_s, d))
    bias_p, bias_s = _bias_tables(rel_bias, ts)
    cos_np, sin_np = _rope_tables_np(s, past, ts, bd)
    cos_t, sin_t = jnp.asarray(cos_np), jnp.asarray(sin_np)

    w1 = tuple(w.astype(BF16) for w in (ffn1_w_gate, ffn1_w_up, ffn1_w_down))
    w2 = tuple(w.astype(BF16) for w in (ffn2_w_gate, ffn2_w_up, ffn2_w_down))
    w_out_even, w_out_odd = even_w_out.astype(BF16), odd_w_out.astype(BF16)
    pool_kpet = jnp.swapaxes(cache_mla_kpe, 2, 3)

    outs = {k: [] for k in ("kp", "vp", "ks", "vs", "sv", "cp", "cs", "ckp", "kpp", "cks", "kps")}
    for l in range(depth):
        x = _ffn(x, ffn1_norm[l], *w1, l, n_pt)
        if l % 2 == 0:
            e = l // 2
            q, kv, ug, kv_p = _inproj_even(x, mix_norm[l], even_w_in[e].astype(BF16), n_pt, s)
            mix_p = _mix_even_prompt(q, kv, ug, bias_p, attn_sinks[e], sgu_ln_g[e], sgu_ln_b[e], sgu_w[e],
                                     sgu_b[e], nb, s)
            q_b = to_b_major(q[n_p:].astype(F32)).reshape(n_s, A_Q_W)
            kvn_b = to_b_major(kv[n_p:]).reshape(n_s, 2 * A_KV_W)
            a_b, new_k, new_v = _mix_even_sample(q_b, kvn_b, cache_swa_k[e].reshape(bd, wb, A_KV_W),
                                                 cache_swa_v[e].reshape(bd, wb, A_KV_W), bias_s, attn_sinks[e], ts)
            a_t = a_b.reshape(bd, ts, A_Q_W).transpose(1, 0, 2).reshape(n_s, A_Q_W).astype(BF16)
            sg_s, v_rows = _sgu_sample(ug, sgu_ln_g[e], sgu_ln_b[e], sgu_w[e], sgu_b[e], n_p, ts, bd)
            mix_s = jnp.concatenate([a_t, sg_s], axis=1)
            w_out = (w_out_even, e)
            outs["kp"].append(kv_p[..., :A_KV_W].reshape(nb, wb, A_KV_HEADS, HEAD_DIM))
            outs["vp"].append(kv_p[..., A_KV_W:].reshape(nb, wb, A_KV_HEADS, HEAD_DIM))
            outs["ks"].append(new_k.reshape(bd, wb, A_KV_HEADS, HEAD_DIM))
            outs["vs"].append(new_v.reshape(bd, wb, A_KV_HEADS, HEAD_DIM))
            outs["sv"].append(to_b_major(v_rows))
        else:
            j = l // 2
            w_in_p, w_uq_r, wuk_ext, wuv_ext = _prep_odd_weights(odd_w_in[j], mla_w_uq[j], mla_w_uk[j], mla_w_uv[j])
            ci, qn, qp, kc, ckv_p, kpe_p, ckv_s, kpe_s = _inproj_odd(
                x, mix_norm[l], w_in_p, mla_q_norm[j], w_uq_r, mla_kv_norm[j], cos_t, sin_t, n_pt, s)
            mix_p, cst_p = _mix_odd_prompt(ci, qn, qp, kc, wuk_ext, wuv_ext, conv_w[j], nb, s)
            qx = _qx_sample(qn, qp, wuk_ext, n_p, n_s)
            qx_b = qx.reshape(MLA_HEADS * ts, bd, KC_W).transpose(1, 0, 2)
            kn_b = jnp.pad(to_b_major(kc[n_p:].astype(F32)), ((0, 0), (0, 8 - ts), (0, 0)))
            o_lat = _mla_sample(page_table, qx_b, kn_b, cache_mla_ckv, pool_kpet, j, ts)
            o_lat_t = o_lat.reshape(bd, MLA_HEADS, ts, KV_LORA).transpose(1, 2, 0, 3).reshape(
                MLA_HEADS, n_s, KV_LORA).astype(BF16)
            mix_s, cst_s = _odd_sample_tail(ci, state_conv[j].transpose(1, 0, 2), conv_w[j], o_lat_t, wuv_ext,
                                            n_p, ts, bd)
            w_out = (w_out_odd, j)
            outs["cp"].append(cst_p[:, 8 - (CONV_WIDTH - 1):])
            outs["cs"].append(cst_s.transpose(1, 0, 2))
            outs["ckp"].append(ckv_p.reshape(nb, s, KV_LORA))
            outs["kpp"].append(kpe_p.reshape(nb, s, ROPE_DIM))
            outs["cks"].append(to_b_major(ckv_s))
            outs["kps"].append(to_b_major(kpe_s))
        x = _ffn(x, ffn2_norm[l], *w2, l, n_pt, mix=(mix_p, mix_s) + w_out,
                 final_g=final_norm if l == depth - 1 else None)
    y_prompt = x[0].reshape(nb, s, d)
    y_sample = x[1].reshape(ts, bd, d).transpose(1, 0, 2)
    st = lambda k: jnp.stack(outs[k])
    return (y_prompt, y_sample, st("kp"), st("vp"), st("ks"), st("vs"), st("sv"), st("cp"), st("cs"),
            st("ckp"), st("kpp"), st("cks"), st("kps"))
```

```python
import functools
import math

import numpy as np
import jax
import jax.numpy as jnp
from jax import lax
from jax.experimental import pallas as pl
from jax.experimental.pallas import tpu as pltpu

F32 = jnp.float32
BF16 = jnp.bfloat16

HEAD_DIM = 64
A_HEADS = 8
A_KV_HEADS = 2
A_Q_W = A_HEADS * HEAD_DIM
A_KV_W = A_KV_HEADS * HEAD_DIM
WINDOW = 128
N_BUCKETS = 32
MAX_DISTANCE = 128
SG_GROUPS = 4
SG_GROUP_DIM = 128
SG_WIDTH = SG_GROUPS * SG_GROUP_DIM
CHUNK = 128
C_WIDTH = 512
CONV_WIDTH = 3
MLA_HEADS = 8
NOPE_DIM = 64
ROPE_DIM = 32
V_DIM = 64
Q_LORA = 768
KV_LORA = 256
ROPE_THETA = 10000.0
PAGE_SIZE = 128
EPS = 1e-6
NEG = -1e30

LANE = 128
KC_W = KV_LORA + LANE
TM = 512
FC = 512
TQ_EVEN = 512
TQ_ODD = 256
KB_ODD = 512
CHAINS_ODD = 2
BB_EVEN = 16
MLA_SAMPLE_CHUNK = 2048
MLA_SAMPLE_BATCHES = 2
MLA_SAMPLE_VMEM = 60 * 1024 * 1024
VMEM_LIMIT = 56 * 1024 * 1024


def _cp(sem, vmem=VMEM_LIMIT):
    return pltpu.CompilerParams(dimension_semantics=sem, vmem_limit_bytes=vmem)


def _dot(a, b):
    return jnp.dot(a, b, preferred_element_type=F32)


def _dot_nt(a, b):
    return lax.dot_general(a, b, (((1,), (1,)), ((), ())), preferred_element_type=F32)


def _rms(x, g):
    ms = jnp.mean(x * x, axis=-1, keepdims=True)
    return x * lax.rsqrt(ms + EPS) * g


def _gelu(x):
    return 0.5 * x * (1.0 + lax.erf(x * np.float32(math.sqrt(0.5))))


def _const_spec(shape):
    n = len(shape)
    return pl.BlockSpec(shape, lambda *_: (0,) * n, pipeline_mode=pl.Buffered(1))


def _smem_spec():
    return pl.BlockSpec(memory_space=pltpu.SMEM)


def _ffn_chunks(f):
    chunks = [(c, FC) for c in range(0, f - f % FC, FC)]
    if f % FC:
        chunks.append((f - f % FC, f % FC))
    return chunks


def _ffn_kernel(*refs, split_in, has_mix, has_final, n_prompt_tiles, chunks):
    it = iter(refs)
    x_ref = next(it)
    if split_in:
        xs_ref = next(it)
    if has_mix:
        mixp_ref, mixs_ref, wout_ref = next(it), next(it), next(it)
    g_ref, wg_ref, wu_ref, wd_ref = next(it), next(it), next(it), next(it)
    if has_final:
        fg_ref = next(it)
    o_ref = next(it)
    if has_final:
        os_ref = next(it)
    acc_ref = next(it)

    is_sample = pl.program_id(0) >= n_prompt_tiles
    x = x_ref[...]
    if split_in:
        x = jnp.where(is_sample, xs_ref[...], x)
    if has_mix:
        mix = jnp.where(is_sample, mixs_ref[...], mixp_ref[...])
        x = x + _dot(mix, wout_ref[...])
    h = _rms(x, g_ref[...]).astype(BF16)
    for k, (c0, cw) in enumerate(chunks):
        g = _dot(h, wg_ref[:, c0:c0 + cw])
        u = _dot(h, wu_ref[:, c0:c0 + cw])
        a = (g * jax.nn.sigmoid(g) * u).astype(BF16)
        d = _dot(a, wd_ref[c0:c0 + cw, :])
        if k == 0:
            acc_ref[...] = d
        else:
            acc_ref[...] += d
    y = x + 0.5 * acc_ref[...]
    if has_final:
        y = _rms(y, fg_ref[...])

        @pl.when(jnp.logical_not(is_sample))
        def _():
            o_ref[...] = y

        @pl.when(is_sample)
        def _():
            os_ref[...] = y
    else:
        o_ref[...] = y


def _layer_spec(shape, layer):
    n = len(shape)
    return pl.BlockSpec((None,) + tuple(shape), lambda *_: (layer,) + (0,) * n, pipeline_mode=pl.Buffered(1))


def _ffn(x, norm_g, w_gate, w_up, w_down, layer, n_prompt_tiles, mix=None, final_g=None):
    n_tiles = n_prompt_tiles + 1
    last_p = n_prompt_tiles - 1
    row = lambda i: (i, 0)
    prompt_row = lambda i: (jnp.minimum(i, last_p), 0)
    sample_row = lambda i: (0, 0)
    split_in = isinstance(x, tuple)
    d = x[0].shape[1] if split_in else x.shape[1]
    t = n_tiles * TM
    f = w_down.shape[1]
    if split_in:
        args, specs = list(x), [pl.BlockSpec((TM, d), prompt_row), pl.BlockSpec((TM, d), sample_row)]
    else:
        args, specs = [x], [pl.BlockSpec((TM, d), row)]
    if mix is not None:
        mix_p, mix_s, w_out, e = mix
        dm = w_out.shape[1]
        args += [mix_p, mix_s, w_out]
        specs += [pl.BlockSpec((TM, dm), prompt_row), pl.BlockSpec((TM, dm), sample_row),
                  _layer_spec((dm, d), e)]
    args += [norm_g.reshape(1, d), w_gate, w_up, w_down]
    specs += [_const_spec((1, d)), _layer_spec((d, f), layer), _layer_spec((d, f), layer),
              _layer_spec((f, d), layer)]
    if final_g is not None:
        args.append(final_g.reshape(1, d))
        specs.append(_const_spec((1, d)))
    body = functools.partial(_ffn_kernel, split_in=split_in, has_mix=mix is not None,
                             has_final=final_g is not None, n_prompt_tiles=n_prompt_tiles, chunks=_ffn_chunks(f))
    if final_g is not None:
        out_specs = [pl.BlockSpec((TM, d), prompt_row), pl.BlockSpec((TM, d), sample_row)]
        out_shape = [jax.ShapeDtypeStruct((n_prompt_tiles * TM, d), F32), jax.ShapeDtypeStruct((TM, d), F32)]
    else:
        out_specs = pl.BlockSpec((TM, d), row)
        out_shape = jax.ShapeDtypeStruct((t, d), F32)
    return pl.pallas_call(
        body,
        grid=(n_tiles,),
        in_specs=specs,
        out_specs=out_specs,
        out_shape=out_shape,
        scratch_shapes=[pltpu.VMEM((TM, d), F32)],
        compiler_params=_cp(("arbitrary",)),
        name="ffn",
    )(*args)


def _t5_bucket_np(dist):
    n = np.maximum(dist, 0)
    max_exact = N_BUCKETS // 2
    nf = np.maximum(n, 1).astype(np.float32)
    large = max_exact + (np.log(nf / np.float32(max_exact)) / np.float32(math.log(MAX_DISTANCE / max_exact))
                         * np.float32(N_BUCKETS - max_exact)).astype(np.int32)
    return np.where(n < max_exact, n, np.minimum(large, N_BUCKETS - 1)).astype(np.int32)


def _bias_tables_np(ts):
    q = np.arange(WINDOW)[:, None]
    j = np.arange(2 * WINDOW)[None, :]
    bucket_p = _t5_bucket_np(q - j + WINDOW)
    nb = 8 // ts
    cols = nb * WINDOW + 16
    bucket_s = -np.ones((8, cols), np.int32)
    for r in range(8):
        bi, t = divmod(r, ts)
        for c in range(cols):
            if c < nb * WINDOW:
                bj, jj = divmod(c, WINDOW)
                dist = t - (jj - WINDOW)
            elif c < nb * WINDOW + 8:
                bj, tt = divmod(c - nb * WINDOW, ts)
                dist = t - tt
            else:
                continue
            if bj == bi and 0 <= dist <= WINDOW:
                bucket_s[r, c] = _t5_bucket_np(np.array(dist))
    return bucket_p, bucket_s


def _bias_kernel(rb_ref, bp_ref, bs_ref, op_ref, os_ref):
    h = pl.program_id(0)
    bp = bp_ref[...]
    bs = bs_ref[...]
    accp = jnp.zeros(bp.shape, F32)
    accs = jnp.full(bs.shape, NEG, F32)
    for k in range(N_BUCKETS):
        val = rb_ref[k, h]
        accp = jnp.where(bp == k, val, accp)
        accs = jnp.where(bs == k, val, accs)
    op_ref[0] = accp
    os_ref[0] = accs


def _bias_tables(rel_bias, ts):
    bucket_p, bucket_s = _bias_tables_np(ts)
    return pl.pallas_call(
        _bias_kernel,
        grid=(A_HEADS,),
        in_specs=[_smem_spec(),
                  pl.BlockSpec(bucket_p.shape, lambda h: (0, 0)),
                  pl.BlockSpec(bucket_s.shape, lambda h: (0, 0))],
        out_specs=[pl.BlockSpec((1,) + bucket_p.shape, lambda h: (h, 0, 0)),
                   pl.BlockSpec((1,) + bucket_s.shape, lambda h: (h, 0, 0))],
        out_shape=[jax.ShapeDtypeStruct((A_HEADS,) + bucket_p.shape, F32),
                   jax.ShapeDtypeStruct((A_HEADS,) + bucket_s.shape, F32)],
        compiler_params=_cp(("arbitrary",)),
        name="t5_bias",
    )(rel_bias, jnp.asarray(bucket_p), jnp.asarray(bucket_s))


def _inproj_even_kernel(x_ref, g_ref, w_ref, q_ref, kv_ref, ug_ref, last_ref, *, tiles_per_seq, n_prompt_tiles):
    h = _rms(x_ref[...], g_ref[...]).astype(BF16)
    p = _dot(h, w_ref[...])
    q_ref[...] = (p[:, :A_Q_W] * np.float32(HEAD_DIM ** -0.5)).astype(BF16)
    kv = p[:, A_Q_W:A_Q_W + 2 * A_KV_W]
    kv_ref[...] = kv
    ug_ref[...] = p[:, A_Q_W + 2 * A_KV_W:]
    i = pl.program_id(0)

    @pl.when(((i + 1) % tiles_per_seq == 0) & (i < n_prompt_tiles))
    def _():
        last_ref[...] = kv[TM - WINDOW:]


def _inproj_even(x, norm_g, w_in, n_prompt_tiles, s):
    t, d = x.shape
    n = w_in.shape[1]
    row = lambda i: (i, 0)
    per_seq = s // TM
    nb = n_prompt_tiles // per_seq
    return pl.pallas_call(
        functools.partial(_inproj_even_kernel, tiles_per_seq=per_seq, n_prompt_tiles=n_prompt_tiles),
        grid=(t // TM,),
        in_specs=[pl.BlockSpec((TM, d), row), _const_spec((1, d)), _const_spec((d, n))],
        out_specs=[pl.BlockSpec((TM, A_Q_W), row), pl.BlockSpec((TM, 2 * A_KV_W), row),
                   pl.BlockSpec((TM, 2 * SG_WIDTH), row),
                   pl.BlockSpec((None, WINDOW, 2 * A_KV_W), lambda i: (jnp.minimum(i // per_seq, nb - 1), 0, 0))],
        out_shape=[jax.ShapeDtypeStruct((t, A_Q_W), BF16), jax.ShapeDtypeStruct((t, 2 * A_KV_W), F32),
                   jax.ShapeDtypeStruct((t, 2 * SG_WIDTH), F32),
                   jax.ShapeDtypeStruct((nb, WINDOW, 2 * A_KV_W), F32)],
        compiler_params=_cp(("arbitrary",)),
        name="inproj_even",
    )(x, norm_g.reshape(1, d), w_in)


def _head_variants(x):
    lo = lax.broadcasted_iota(jnp.int32, x.shape, 1) < HEAD_DIM
    xr = pltpu.roll(x, HEAD_DIM, 1)
    z = jnp.zeros_like(x)
    return [[jnp.where(lo, x, z).astype(BF16), jnp.where(lo, z, xr).astype(BF16)],
            [jnp.where(lo, xr, z).astype(BF16), jnp.where(lo, z, x).astype(BF16)]]


def _sink_softmax_pv(s, sink, v):
    mx = jnp.maximum(jnp.max(s, axis=-1, keepdims=True), sink)
    p = jnp.exp(s - mx)
    den = jnp.sum(p, axis=-1, keepdims=True) + jnp.exp(sink - mx)
    return _dot(p.astype(BF16), v) * (1.0 / den)


def _layer_norm(x, g, b):
    xc = x - jnp.mean(x, axis=-1, keepdims=True)
    y = xc * lax.rsqrt(jnp.mean(xc * xc, axis=-1, keepdims=True) + EPS)
    return y * g + b


def _even_prompt_kernel(q_ref, kv_ref, kvp_ref, ug_ref, bias_ref, sink_ref, lng_ref, lnb_ref, ws_ref,
                        bst_ref, o_ref, *, tq):
    nblk = tq // WINDOW
    step = pl.program_id(1)
    kv_all = jnp.concatenate([kvp_ref[...], kv_ref[...]], axis=0)
    kvar = _head_variants(kv_all[:, :A_KV_W])
    vvar = _head_variants(kv_all[:, A_KV_W:])
    qi = lax.broadcasted_iota(jnp.int32, (WINDOW, 2 * WINDOW), 0)
    kj = lax.broadcasted_iota(jnp.int32, (WINDOW, 2 * WINDOW), 1)
    dist = qi - kj + WINDOW
    band = (dist >= 0) & (dist <= WINDOW)
    cur = kj >= WINDOW
    ri = lax.broadcasted_iota(jnp.int32, (CHUNK, CHUNK), 0)
    ci = lax.broadcasted_iota(jnp.int32, (CHUNK, CHUNK), 1)
    wt = [jnp.where(ri >= ci, ws_ref[g], 0.0).astype(BF16) for g in range(SG_GROUPS)]
    lng, lnb = lng_ref[...], lnb_ref[...]
    for n in range(nblk):
        r0 = n * WINDOW
        has_prev = (step * nblk + n) > 0
        mask = band & (cur | has_prev)
        for m in range(A_HEADS // 2):
            kvh = (2 * m) // (A_HEADS // A_KV_HEADS)
            qp = q_ref[r0:r0 + WINDOW, LANE * m:LANE * (m + 1)]
            o = None
            for half in range(2):
                hd = 2 * m + half
                s = _dot_nt(qp, kvar[kvh][half][r0:r0 + 2 * WINDOW])
                s = jnp.where(mask, s + bias_ref[hd], NEG)
                oh = _sink_softmax_pv(s, sink_ref[hd], vvar[kvh][half][r0:r0 + 2 * WINDOW])
                o = oh if o is None else o + oh
            o_ref[r0:r0 + WINDOW, LANE * m:LANE * (m + 1)] = o.astype(BF16)
        u = _gelu(ug_ref[r0:r0 + WINDOW, :SG_WIDTH])
        v = _layer_norm(_gelu(ug_ref[r0:r0 + WINDOW, SG_WIDTH:]), lng, lnb)
        for g in range(SG_GROUPS):
            sl = slice(SG_GROUP_DIM * g, SG_GROUP_DIM * (g + 1))
            y = _dot(wt[g], v[:, sl].astype(BF16)) + bst_ref[:, g:g + 1]
            o_ref[r0:r0 + WINDOW, A_Q_W + SG_GROUP_DIM * g:A_Q_W + SG_GROUP_DIM * (g + 1)] = (
                u[:, sl] * y).astype(BF16)


def _mix_even_prompt(q, kv, ug, bias_p, sinks, ln_g, ln_b, w_s, b_s, nb, s):
    tq = TQ_EVEN
    nq = s // tq
    blk = lambda b, i: (b * nq + i, 0)
    prev = lambda b, i: (jnp.maximum((b * nq + i) * (tq // WINDOW) - 1, 0), 0)
    return pl.pallas_call(
        functools.partial(_even_prompt_kernel, tq=tq),
        grid=(nb, nq),
        in_specs=[pl.BlockSpec((tq, A_Q_W), blk), pl.BlockSpec((tq, 2 * A_KV_W), blk),
                  pl.BlockSpec((WINDOW, 2 * A_KV_W), prev), pl.BlockSpec((tq, 2 * SG_WIDTH), blk),
                  _const_spec(bias_p.shape), _smem_spec(), _const_spec((1, SG_WIDTH)),
                  _const_spec((1, SG_WIDTH)), _const_spec(w_s.shape), _const_spec((CHUNK, SG_GROUPS))],
        out_specs=pl.BlockSpec((tq, A_Q_W + SG_WIDTH), blk),
        out_shape=jax.ShapeDtypeStruct((nb * s, A_Q_W + SG_WIDTH), BF16),
        compiler_params=_cp(("arbitrary", "arbitrary")),
        name="mix_even_prompt",
    )(q, kv, kv, ug, bias_p, sinks, ln_g.reshape(1, -1), ln_b.reshape(1, -1), w_s, b_s.T)


def _even_sample_kernel(q_ref, kvn_ref, ck_ref, cv_ref, bias_ref, sink_ref, a_ref, nk_ref, nv_ref, *, ts):
    bb = ck_ref.shape[0]
    gpb = 8 // ts
    wb = ck_ref.shape[1]
    n_cache = gpb * wb
    hpk = A_HEADS // A_KV_HEADS
    zpad = jnp.zeros((8, A_KV_W), F32)

    def lo_half(x, kvh):
        lo = lax.broadcasted_iota(jnp.int32, x.shape, 1) < HEAD_DIM
        return jnp.where(lo, x if kvh == 0 else pltpu.roll(x, HEAD_DIM, 1), 0.0).astype(BF16)

    hrow = lax.broadcasted_iota(jnp.int32, (8 * hpk, 1), 0) // 8
    bias_c, bias_n, sink = [], [], []
    for kvh in range(A_KV_HEADS):
        heads = [hpk * kvh + i for i in range(hpk)]
        bias_c.append(jnp.concatenate([bias_ref[h][:, :n_cache] for h in heads], axis=0))
        bias_n.append(jnp.concatenate([bias_ref[h][:, n_cache:] for h in heads], axis=0))
        sk = jnp.full((8 * hpk, 1), sink_ref[heads[0]], F32)
        for i in range(1, hpk):
            sk = jnp.where(hrow == i, sink_ref[heads[i]], sk)
        sink.append(sk)
    for g in range(bb // gpb):
        r0 = 8 * g
        qg = q_ref[r0:r0 + 8, :]
        kn = kvn_ref[r0:r0 + 8, :A_KV_W]
        vn = kvn_ref[r0:r0 + 8, A_KV_W:]
        kc = jnp.concatenate([ck_ref[gpb * g + i] for i in range(gpb)], axis=0)
        vc = jnp.concatenate([cv_ref[gpb * g + i] for i in range(gpb)], axis=0)
        knp = jnp.concatenate([kn, zpad], axis=0)
        vnp = jnp.concatenate([vn, zpad], axis=0)
        for kvh in range(A_KV_HEADS):
            c0 = 2 * LANE * kvh
            qs = jnp.concatenate([lo_half(qg[:, c0 + LANE * (i // 2):c0 + LANE * (i // 2 + 1)], i % 2)
                                  for i in range(hpk)], axis=0)
            sc = _dot_nt(qs, lo_half(kc, kvh))
            sn = _dot_nt(qs, lo_half(knp, kvh))
            sc = jnp.where(bias_c[kvh] > 0.5 * NEG, sc + bias_c[kvh], NEG)
            sn = jnp.where(bias_n[kvh] > 0.5 * NEG, sn + bias_n[kvh], NEG)
            mx = jnp.maximum(jnp.maximum(jnp.max(sc, axis=-1, keepdims=True), jnp.max(sn, axis=-1, keepdims=True)),
                             sink[kvh])
            pc = jnp.exp(sc - mx)
            pn = jnp.exp(sn - mx)
            den = (jnp.sum(pc, axis=-1, keepdims=True) + jnp.sum(pn, axis=-1, keepdims=True)
                   + jnp.exp(sink[kvh] - mx))
            o = (_dot(pc.astype(BF16), lo_half(vc, kvh)) + _dot(pn.astype(BF16), lo_half(vnp, kvh))) * (1.0 / den)
            for i in range(hpk // 2):
                pair = o[16 * i:16 * i + 8] + pltpu.roll(o[16 * i + 8:16 * i + 16], HEAD_DIM, 1)
                a_ref[r0:r0 + 8, c0 + LANE * i:c0 + LANE * (i + 1)] = pair
        for i in range(gpb):
            b = gpb * g + i
            nk_ref[b, 0:wb - ts, :] = ck_ref[b, ts:wb, :]
            nk_ref[b, wb - ts:wb, :] = kn[ts * i:ts * (i + 1)]
            nv_ref[b, 0:wb - ts, :] = cv_ref[b, ts:wb, :]
            nv_ref[b, wb - ts:wb, :] = vn[ts * i:ts * (i + 1)]


def _mix_even_sample(q_b, kvn_b, cache_k, cache_v, bias_s, sinks, ts):
    bd, wb, _ = cache_k.shape
    bb = BB_EVEN
    rows = bb * ts
    blk2 = lambda i: (i, 0)
    blk3 = lambda i: (i, 0, 0)
    return pl.pallas_call(
        functools.partial(_even_sample_kernel, ts=ts),
        grid=(bd // bb,),
        in_specs=[pl.BlockSpec((rows, A_Q_W), blk2), pl.BlockSpec((rows, 2 * A_KV_W), blk2),
                  pl.BlockSpec((bb, wb, A_KV_W), blk3), pl.BlockSpec((bb, wb, A_KV_W), blk3),
                  _const_spec(bias_s.shape), _smem_spec()],
        out_specs=[pl.BlockSpec((rows, A_Q_W), blk2), pl.BlockSpec((bb, wb, A_KV_W), blk3),
                   pl.BlockSpec((bb, wb, A_KV_W), blk3)],
        out_shape=[jax.ShapeDtypeStruct((bd * ts, A_Q_W), F32),
                   jax.ShapeDtypeStruct(cache_k.shape, F32), jax.ShapeDtypeStruct(cache_v.shape, F32)],
        compiler_params=_cp(("arbitrary",)),
        name="mix_even_sample",
    )(q_b, kvn_b, cache_k, cache_v, bias_s, sinks)


def _sgu_sample_kernel(ug_ref, lng_ref, lnb_ref, w_ref, b_ref, sg_ref, v_ref, *, ts, bd):
    u = _gelu(ug_ref[:, :SG_WIDTH])
    v = _layer_norm(_gelu(ug_ref[:, SG_WIDTH:]), lng_ref[...], lnb_ref[...])
    v_ref[...] = v
    for i in range(ts):
        for g in range(SG_GROUPS):
            sl = slice(SG_GROUP_DIM * g, SG_GROUP_DIM * (g + 1))
            y = jnp.full((bd, SG_GROUP_DIM), b_ref[g, i], F32)
            for j in range(i + 1):
                y = y + w_ref[g, i * ts + j] * v[bd * j:bd * (j + 1), sl]
            sg_ref[bd * i:bd * (i + 1), sl] = (u[bd * i:bd * (i + 1), sl] * y).astype(BF16)


def _sgu_sample(ug, ln_g, ln_b, w_s, b_s, n_prompt_rows, ts, bd):
    rows = ts * bd
    off = n_prompt_rows // rows
    w_small = w_s[:, :ts, :ts].reshape(SG_GROUPS, ts * ts)
    b_small = b_s[:, :ts]
    return pl.pallas_call(
        functools.partial(_sgu_sample_kernel, ts=ts, bd=bd),
        grid=(1,),
        in_specs=[pl.BlockSpec((rows, 2 * SG_WIDTH), lambda i: (off, 0)), _const_spec((1, SG_WIDTH)),
                  _const_spec((1, SG_WIDTH)), _smem_spec(), _smem_spec()],
        out_specs=[pl.BlockSpec((rows, SG_WIDTH), lambda i: (0, 0)),
                   pl.BlockSpec((rows, SG_WIDTH), lambda i: (0, 0))],
        out_shape=[jax.ShapeDtypeStruct((rows, SG_WIDTH), BF16), jax.ShapeDtypeStruct((rows, SG_WIDTH), F32)],
        compiler_params=_cp(("arbitrary",)),
        name="sgu_sample",
    )(ug, ln_g.reshape(1, -1), ln_b.reshape(1, -1), w_small, b_small)


def _rope_tables_np(s, past, ts, bd):
    half = ROPE_DIM // 2
    inv = np.exp(-math.log(ROPE_THETA) * np.arange(half, dtype=np.float32) / np.float32(half)).astype(np.float32)
    pos = np.concatenate([np.arange(s), past + np.repeat(np.arange(ts), bd)]).astype(np.float32)
    ang = (pos[:, None] * inv[None, :]).astype(np.float32).astype(np.float64)
    cos = np.cos(ang)
    sin = np.sin(ang)
    cos_t = np.tile(np.concatenate([cos, cos], axis=1), (1, MLA_HEADS)).astype(np.float32)
    sin_t = np.tile(np.concatenate([-sin, sin], axis=1), (1, MLA_HEADS)).astype(np.float32)
    return cos_t, sin_t


def _rope(x, cos_t, sin_t):
    n = x.shape[1]
    half = ROPE_DIM // 2
    first = (lax.broadcasted_iota(jnp.int32, x.shape, 1) % ROPE_DIM) < half
    partner = jnp.where(first, pltpu.roll(x, n - half, 1), pltpu.roll(x, half, 1))
    return x * cos_t + partner * sin_t


def _inproj_odd_kernel(x_ref, g_ref, w_ref, qg_ref, wuq_ref, kvg_ref, cos_ref, sin_ref,
                       ci_ref, qn_ref, qp_ref, kc_ref, ckvp_ref, kpep_ref, ckvs_ref, kpes_ref, *, n_prompt_tiles):
    h = _rms(x_ref[...], g_ref[...]).astype(BF16)
    p = _dot(h, w_ref[...])
    o_q = 3 * C_WIDTH
    o_kv = o_q + Q_LORA
    o_pe = o_kv + KV_LORA
    ci_ref[...] = p[:, :o_q].astype(BF16)
    scale = np.float32((NOPE_DIM + ROPE_DIM) ** -0.5)
    q = _dot(_rms(p[:, o_q:o_kv], qg_ref[...]).astype(BF16), wuq_ref[...])
    n_nope = MLA_HEADS * NOPE_DIM
    cos_t, sin_t = cos_ref[...], sin_ref[...]
    qn_ref[...] = (q[:, :n_nope] * scale).astype(BF16)
    qp_ref[...] = (_rope(q[:, n_nope:], cos_t, sin_t) * scale).astype(BF16)
    ckv = _rms(p[:, o_kv:o_pe], kvg_ref[...])
    kpe = _rope(p[:, o_pe:], cos_t[:, :LANE], sin_t[:, :LANE])
    kc_ref[:, :KV_LORA] = ckv.astype(BF16)
    kc_ref[:, KV_LORA:] = kpe.astype(BF16)
    is_sample = pl.program_id(0) >= n_prompt_tiles

    @pl.when(jnp.logical_not(is_sample))
    def _():
        ckvp_ref[...] = ckv
        kpep_ref[...] = kpe[:, :ROPE_DIM]

    @pl.when(is_sample)
    def _():
        ckvs_ref[...] = ckv
        kpes_ref[...] = kpe[:, :ROPE_DIM]


def _inproj_odd(x, norm_g, w_in, q_norm, w_uq, kv_norm, cos_t, sin_t, n_prompt_tiles, s):
    t, d = x.shape
    n = w_in.shape[1]
    row = lambda i: (i, 0)
    last_p = n_prompt_tiles - 1
    prompt_row = lambda i: (jnp.minimum(i, last_p), 0)
    sample_row = lambda i: (0, 0)
    per_seq = s // TM
    tab = lambda i: (jnp.where(i < n_prompt_tiles, i % per_seq, per_seq), 0)
    n_pe = MLA_HEADS * ROPE_DIM
    n_p = n_prompt_tiles * TM
    widths = [(3 * C_WIDTH, BF16), (MLA_HEADS * NOPE_DIM, BF16), (n_pe, BF16), (KC_W, BF16)]
    return pl.pallas_call(
        functools.partial(_inproj_odd_kernel, n_prompt_tiles=n_prompt_tiles),
        grid=(t // TM,),
        in_specs=[pl.BlockSpec((TM, d), row), _const_spec((1, d)), _const_spec((d, n)),
                  _const_spec((1, Q_LORA)), _const_spec(w_uq.shape), _const_spec((1, KV_LORA)),
                  pl.BlockSpec((TM, n_pe), tab), pl.BlockSpec((TM, n_pe), tab)],
        out_specs=[pl.BlockSpec((TM, w), row) for w, _ in widths] + [
            pl.BlockSpec((TM, KV_LORA), prompt_row), pl.BlockSpec((TM, ROPE_DIM), prompt_row),
            pl.BlockSpec((TM, KV_LORA), sample_row), pl.BlockSpec((TM, ROPE_DIM), sample_row)],
        out_shape=[jax.ShapeDtypeStruct((t, w), dt) for w, dt in widths] + [
            jax.ShapeDtypeStruct((n_p, KV_LORA), F32), jax.ShapeDtypeStruct((n_p, ROPE_DIM), F32),
            jax.ShapeDtypeStruct((TM, KV_LORA), F32), jax.ShapeDtypeStruct((TM, ROPE_DIM), F32)],
        compiler_params=_cp(("arbitrary",)),
        name="inproj_odd",
    )(x, norm_g.reshape(1, d), w_in, q_norm.reshape(1, -1), w_uq, kv_norm.reshape(1, -1), cos_t, sin_t)


def _build_qx(qn_ref, qp_ref, wuk_ref, qx_ref, rows):
    qp = qp_ref[...].astype(F32)
    keep = lax.broadcasted_iota(jnp.int32, (rows, LANE), 1) < ROPE_DIM
    n_pe = MLA_HEADS * ROPE_DIM
    for h in range(MLA_HEADS):
        pair = qn_ref[:, LANE * (h // 2):LANE * (h // 2 + 1)]
        qx_ref[h * rows:(h + 1) * rows, :KV_LORA] = _dot(pair, wuk_ref[h]).astype(BF16)
        pe = qp if h == 0 else pltpu.roll(qp, n_pe - ROPE_DIM * h, 1)
        qx_ref[h * rows:(h + 1) * rows, KV_LORA:] = jnp.where(keep, pe[:, :LANE], 0.0).astype(BF16)


def _odd_prompt_kernel(ci_ref, cih_ref, qn_ref, qp_ref, kc_ref, wuk_ref, wuv_ref, cw_ref,
                       o_ref, cst_ref, qx_ref, m_ref, l_ref, acc_ref, z_ref, *, tq, kb):
    step = pl.program_id(1)
    halo = 16
    zh = cih_ref[:, C_WIDTH:2 * C_WIDTH].astype(F32) * cih_ref[:, 2 * C_WIDTH:].astype(F32)
    z_ref[0:halo, :] = jnp.where(step > 0, zh, 0.0)
    z = ci_ref[:, C_WIDTH:2 * C_WIDTH].astype(F32) * ci_ref[:, 2 * C_WIDTH:].astype(F32)
    z_ref[halo:halo + tq, :] = z
    cw = cw_ref[...]
    y = cw[0:1] * z_ref[halo - 2:halo - 2 + tq, :] + cw[1:2] * z_ref[halo - 1:halo - 1 + tq, :] + cw[2:3] * z
    o_ref[:, :C_WIDTH] = (ci_ref[:, :C_WIDTH].astype(F32) * y).astype(BF16)
    cst_ref[0] = z[tq - 8:tq]

    _build_qx(qn_ref, qp_ref, wuk_ref, qx_ref, tq)
    rows = MLA_HEADS * tq
    m_ref[...] = jnp.full((rows, LANE), NEG, F32)
    l_ref[...] = jnp.zeros((rows, LANE), F32)
    acc_ref[...] = jnp.zeros((rows, KV_LORA), F32)
    n_full = (step * tq) // kb
    hpc = MLA_HEADS // CHAINS_ODD

    def kv_step(j, width, masked):
        k0 = pl.multiple_of(j * kb, kb)
        kc = kc_ref[pl.ds(k0, width), :]
        n_grp = width // LANE
        if masked:
            qpos = step * tq + lax.broadcasted_iota(jnp.int32, (tq, LANE), 0)
            kpos = k0 + lax.broadcasted_iota(jnp.int32, (tq, LANE), 1)
            visible = [kpos + LANE * g <= qpos for g in range(n_grp)]
        for c in range(CHAINS_ODD):
            rc = slice(c * hpc * tq, (c + 1) * hpc * tq)
            s_all = _dot_nt(qx_ref[rc, :], kc)
            ps, alphas = [], []
            for hh in range(hpc):
                r = slice((c * hpc + hh) * tq, (c * hpc + hh + 1) * tq)
                sg = [s_all[hh * tq:(hh + 1) * tq, LANE * g:LANE * (g + 1)] for g in range(n_grp)]
                if masked:
                    sg = [jnp.where(visible[g], sg[g], NEG) for g in range(n_grp)]
                m_old = m_ref[r, :]
                m_new = jnp.maximum(m_old, jnp.max(functools.reduce(jnp.maximum, sg), axis=-1, keepdims=True))
                alpha = jnp.exp(m_old - m_new)
                pg = [jnp.exp(x - m_new) for x in sg]
                l_ref[r, :] = alpha * l_ref[r, :] + functools.reduce(jnp.add, pg)
                m_ref[r, :] = m_new
                ps.append(jnp.concatenate([x.astype(BF16) for x in pg], axis=1))
                alphas.append(jnp.concatenate([alpha] * (KV_LORA // LANE), axis=1))
            pv = _dot(jnp.concatenate(ps, axis=0), kc[:, :KV_LORA])
            acc_ref[rc, :] = jnp.concatenate(alphas, axis=0) * acc_ref[rc, :] + pv

    def full_step(j, carry):
        kv_step(j, kb, False)
        return carry

    lax.fori_loop(0, n_full, full_step, 0)
    for i, width in enumerate(range(tq, kb + 1, tq)):
        @pl.when(step % (kb // tq) == i)
        def _(width=width):
            kv_step(n_full, width, True)

    for m in range(MLA_HEADS // 2):
        o = None
        for half in range(2):
            h = 2 * m + half
            inv_l = jnp.broadcast_to(1.0 / jnp.sum(l_ref[h * tq:(h + 1) * tq, :], axis=-1, keepdims=True), (tq, LANE))
            ol = (acc_ref[h * tq:(h + 1) * tq, :] * jnp.concatenate([inv_l] * (KV_LORA // LANE), axis=1)).astype(BF16)
            oh = _dot(ol, wuv_ref[h])
            o = oh if o is None else o + oh
        o_ref[:, C_WIDTH + LANE * m:C_WIDTH + LANE * (m + 1)] = o.astype(BF16)


def _mix_odd_prompt(ci, qn, qp, kc, wuk_ext, wuv_ext, conv_w, nb, s):
    tq, kb = TQ_ODD, KB_ODD
    nq = s // tq
    blk = lambda b, i: (b * nq + i, 0)
    halo = lambda b, i: (jnp.maximum((b * nq + i) * (tq // 16) - 1, 0), 0)
    rows = MLA_HEADS * tq
    n_out = C_WIDTH + MLA_HEADS * V_DIM
    return pl.pallas_call(
        functools.partial(_odd_prompt_kernel, tq=tq, kb=kb),
        grid=(nb, nq),
        in_specs=[pl.BlockSpec((tq, 3 * C_WIDTH), blk), pl.BlockSpec((16, 3 * C_WIDTH), halo),
                  pl.BlockSpec((tq, MLA_HEADS * NOPE_DIM), blk), pl.BlockSpec((tq, MLA_HEADS * ROPE_DIM), blk),
                  pl.BlockSpec((s, KC_W), lambda b, i: (b, 0)),
                  _const_spec(wuk_ext.shape), _const_spec(wuv_ext.shape), _const_spec(conv_w.shape)],
        out_specs=[pl.BlockSpec((tq, n_out), blk), pl.BlockSpec((1, 8, C_WIDTH), lambda b, i: (b, 0, 0))],
        out_shape=[jax.ShapeDtypeStruct((nb * s, n_out), BF16), jax.ShapeDtypeStruct((nb, 8, C_WIDTH), F32)],
        scratch_shapes=[pltpu.VMEM((rows, KC_W), BF16),
                        pltpu.VMEM((rows, LANE), F32), pltpu.VMEM((rows, LANE), F32), pltpu.VMEM((rows, KV_LORA), F32),
                        pltpu.VMEM((tq + 16, C_WIDTH), F32)],
        compiler_params=_cp(("arbitrary", "arbitrary")),
        name="mix_odd_prompt",
    )(ci, ci, qn, qp, kc, wuk_ext, wuv_ext, conv_w)


def _qx_sample_kernel(qn_ref, qp_ref, wuk_ref, qx_ref, *, rows):
    _build_qx(qn_ref, qp_ref, wuk_ref, qx_ref, rows)


def _qx_sample(qn, qp, wuk_ext, n_prompt_rows, rows):
    off = n_prompt_rows // rows
    return pl.pallas_call(
        functools.partial(_qx_sample_kernel, rows=rows),
        grid=(1,),
        in_specs=[pl.BlockSpec((rows, MLA_HEADS * NOPE_DIM), lambda i: (off, 0)),
                  pl.BlockSpec((rows, MLA_HEADS * ROPE_DIM), lambda i: (off, 0)), _const_spec(wuk_ext.shape)],
        out_specs=pl.BlockSpec((MLA_HEADS * rows, KC_W), lambda i: (0, 0)),
        out_shape=jax.ShapeDtypeStruct((MLA_HEADS * rows, KC_W), BF16),
        compiler_params=_cp(("arbitrary",)),
        name="qx_sample",
    )(qn, qp, wuk_ext)


def _page_copies(pt_ref, pool_ckv, pool_kpet, ckv_buf, kpe_buf, sem, layer, n_pages, step, slot):
    copies = []
    for i in range(MLA_SAMPLE_BATCHES):
        b = step * MLA_SAMPLE_BATCHES + i
        for p in range(n_pages):
            page = pt_ref[b, p]
            keys = pl.ds(p * PAGE_SIZE, PAGE_SIZE)
            copies.append(pltpu.make_async_copy(pool_ckv.at[layer, page], ckv_buf.at[slot, i, keys],
                                                sem.at[0, slot]))
            copies.append(pltpu.make_async_copy(pool_kpet.at[layer, page], kpe_buf.at[slot, i, :, keys],
                                                sem.at[1, slot]))
    return copies


def _mla_sample_kernel(pt_ref, qx_ref, kn_ref, pool_ckv, pool_kpet, o_ref, ckv_buf, kpe_buf, sem, *,
                       ts, n_pages, layer, chunk):
    step = pl.program_id(0)
    n_steps = pl.num_programs(0)
    slot = step % 2
    copies = functools.partial(_page_copies, pt_ref, pool_ckv, pool_kpet, ckv_buf, kpe_buf, sem, layer, n_pages)

    @pl.when(step == 0)
    def _():
        for c in copies(0, 0):
            c.start()

    @pl.when(step + 1 < n_steps)
    def _():
        for c in copies(step + 1, 1 - slot):
            c.start()

    for c in copies(step, slot):
        c.wait()

    rows = qx_ref.shape[1]
    t_q = lax.broadcasted_iota(jnp.int32, (rows, 8), 0) % ts
    t_k = lax.broadcasted_iota(jnp.int32, (rows, 8), 1)
    past = n_pages * PAGE_SIZE
    starts = range(0, past, chunk)
    for i in range(MLA_SAMPLE_BATCHES):
        qx = qx_ref[i]
        q_lat = qx[:, :KV_LORA]
        q_pe = qx[:, KV_LORA:KV_LORA + ROPE_DIM]
        kn = kn_ref[i].astype(BF16)
        sn = jnp.where(t_k <= t_q, _dot_nt(qx, kn), NEG)
        kcs = [ckv_buf[slot, i, k0:k0 + chunk, :].astype(BF16) for k0 in starts]
        ss = [_dot_nt(q_lat, kc) + _dot(q_pe, kpe_buf[slot, i, :, k0:k0 + chunk].astype(BF16))
              for k0, kc in zip(starts, kcs)]
        m = functools.reduce(jnp.maximum, [jnp.max(s, axis=-1, keepdims=True) for s in ss],
                             jnp.max(sn, axis=-1, keepdims=True))
        pn = jnp.exp(sn - m)
        den = jnp.sum(pn, axis=-1, keepdims=True)
        pnb = pn.astype(BF16)
        accs = [_dot(pnb, kn[:, LANE * g:LANE * (g + 1)]) for g in range(KV_LORA // LANE)]
        for s, kc in zip(ss, kcs):
            p = jnp.exp(s - m)
            den = den + jnp.sum(p, axis=-1, keepdims=True)
            pb = p.astype(BF16)
            accs = [a + _dot(pb, kc[:, LANE * g:LANE * (g + 1)]) for g, a in enumerate(accs)]
        o_ref[i] = jnp.concatenate(accs, axis=1) * (1.0 / den)


def _mla_sample(page_table, qx_b, kn_b, pool_ckv, pool_kpet, layer, ts):
    bd, n_pages = page_table.shape
    past = n_pages * PAGE_SIZE
    rows = MLA_HEADS * ts
    nbs = MLA_SAMPLE_BATCHES
    blk = lambda g, pt: (g, 0, 0)
    grid_spec = pltpu.PrefetchScalarGridSpec(
        num_scalar_prefetch=1,
        grid=(bd // nbs,),
        in_specs=[pl.BlockSpec((nbs, rows, KC_W), blk), pl.BlockSpec((nbs, 8, KC_W), blk),
                  pl.BlockSpec(memory_space=pl.ANY), pl.BlockSpec(memory_space=pl.ANY)],
        out_specs=pl.BlockSpec((nbs, rows, KV_LORA), blk),
        scratch_shapes=[pltpu.VMEM((2, nbs, past, KV_LORA), F32), pltpu.VMEM((2, nbs, ROPE_DIM, past), F32),
                        pltpu.SemaphoreType.DMA((2, 2))],
    )
    return pl.pallas_call(
        functools.partial(_mla_sample_kernel, ts=ts, n_pages=n_pages, layer=layer,
                          chunk=min(MLA_SAMPLE_CHUNK, past)),
        grid_spec=grid_spec,
        out_shape=jax.ShapeDtypeStruct((bd, rows, KV_LORA), F32),
        compiler_params=_cp(("arbitrary",), vmem=MLA_SAMPLE_VMEM),
        name="mla_sample",
    )(page_table, qx_b, kn_b, pool_ckv, pool_kpet)


def _odd_sample_tail_kernel(ci_ref, st_ref, cw_ref, ol_ref, wuv_ref, o_ref, nst_ref, *, ts, bd):
    cw = cw_ref[...]
    zz = [st_ref[0], st_ref[1]]
    for t in range(ts):
        r = slice(bd * t, bd * (t + 1))
        zz.append(ci_ref[r, C_WIDTH:2 * C_WIDTH].astype(F32) * ci_ref[r, 2 * C_WIDTH:].astype(F32))
    for t in range(ts):
        r = slice(bd * t, bd * (t + 1))
        y = cw[0:1] * zz[t] + cw[1:2] * zz[t + 1] + cw[2:3] * zz[t + 2]
        o_ref[r, :C_WIDTH] = (ci_ref[r, :C_WIDTH].astype(F32) * y).astype(BF16)
    nst_ref[0] = zz[ts]
    nst_ref[1] = zz[ts + 1]
    for m in range(MLA_HEADS // 2):
        o = _dot(ol_ref[2 * m], wuv_ref[2 * m]) + _dot(ol_ref[2 * m + 1], wuv_ref[2 * m + 1])
        o_ref[:, C_WIDTH + LANE * m:C_WIDTH + LANE * (m + 1)] = o.astype(BF16)


def _odd_sample_tail(ci, state_t, conv_w, o_lat_t, wuv_ext, n_prompt_rows, ts, bd):
    rows = ts * bd
    off = n_prompt_rows // rows
    n_out = C_WIDTH + MLA_HEADS * V_DIM
    return pl.pallas_call(
        functools.partial(_odd_sample_tail_kernel, ts=ts, bd=bd),
        grid=(1,),
        in_specs=[pl.BlockSpec((rows, 3 * C_WIDTH), lambda i: (off, 0)), _const_spec(state_t.shape),
                  _const_spec(conv_w.shape), _const_spec(o_lat_t.shape), _const_spec(wuv_ext.shape)],
        out_specs=[pl.BlockSpec((rows, n_out), lambda i: (0, 0)),
                   pl.BlockSpec(state_t.shape, lambda i: (0, 0, 0))],
        out_shape=[jax.ShapeDtypeStruct((rows, n_out), BF16), jax.ShapeDtypeStruct(state_t.shape, F32)],
        compiler_params=_cp(("arbitrary",)),
        name="odd_sample_tail",
    )(ci, state_t, conv_w, o_lat_t, wuv_ext)


def _prep_odd_weights(w_in, w_uq, w_uk, w_uv):
    d, n = w_in.shape
    n_pad = -n % LANE
    w_in_p = jnp.pad(w_in, ((0, 0), (0, n_pad))).astype(BF16)
    per = NOPE_DIM + ROPE_DIM
    wq = w_uq.reshape(Q_LORA, MLA_HEADS, per)
    w_uq_r = jnp.concatenate([wq[:, :, :NOPE_DIM].reshape(Q_LORA, -1), wq[:, :, NOPE_DIM:].reshape(Q_LORA, -1)],
                             axis=1).astype(BF16)
    ukt = jnp.transpose(w_uk, (1, 2, 0))
    uvt = jnp.transpose(w_uv, (1, 0, 2))
    wuk_ext, wuv_ext = [], []
    for h in range(MLA_HEADS):
        lo = (h % 2) * NOPE_DIM
        wuk_ext.append(jnp.pad(ukt[h], ((lo, LANE - NOPE_DIM - lo), (0, 0))))
        wuv_ext.append(jnp.pad(uvt[h], ((0, 0), (lo, LANE - V_DIM - lo))))
    return w_in_p, w_uq_r, jnp.stack(wuk_ext).astype(BF16), jnp.stack(wuv_ext).astype(BF16)


def kernel(x_prompt, x_sample, cache_swa_k, cache_swa_v, state_conv, cache_mla_ckv, cache_mla_kpe, page_table,
           rel_bias, ffn1_norm, ffn1_w_gate, ffn1_w_up, ffn1_w_down, mix_norm, ffn2_norm, ffn2_w_gate,
           ffn2_w_up, ffn2_w_down, even_w_in, even_w_out, attn_sinks, sgu_ln_g, sgu_ln_b, sgu_w, sgu_b,
           odd_w_in, odd_w_out, conv_w, mla_q_norm, mla_w_uq, mla_kv_norm, mla_w_uk, mla_w_uv, final_norm):
    nb, s, d = x_prompt.shape
    bd, ts, _ = x_sample.shape
    depth = ffn1_norm.shape[0]
    n_p, n_s = nb * s, bd * ts
    past = page_table.shape[1] * PAGE_SIZE
    wb = cache_swa_k.shape[2]
    assert n_s == TM and n_p % TM == 0 and s % TM == 0, "sample group must fill exactly one token tile"
    assert 8 % ts == 0 and bd % BB_EVEN == 0 and bd % 8 == 0 and wb == WINDOW
    assert s % KB_ODD == 0 and s % TQ_EVEN == 0 and KB_ODD % TQ_ODD == 0
    n_pt = n_p // TM

    def to_b_major(a):
        return a.reshape(ts, bd, -1).transpose(1, 0, 2)

    x = (x_prompt.reshape(n_p, d), x_sample.transpose(1, 0, 2).reshape(n_s, d))
    bias_p, bias_s = _bias_tables(rel_bias, ts)
    cos_np, sin_np = _rope_tables_np(s, past, ts, bd)
    cos_t, sin_t = jnp.asarray(cos_np), jnp.asarray(sin_np)

    w1 = tuple(w.astype(BF16) for w in (ffn1_w_gate, ffn1_w_up, ffn1_w_down))
    w2 = tuple(w.astype(BF16) for w in (ffn2_w_gate, ffn2_w_up, ffn2_w_down))
    w_out_even, w_out_odd = even_w_out.astype(BF16), odd_w_out.astype(BF16)
    pool_kpet = jnp.swapaxes(cache_mla_kpe, 2, 3)

    outs = {k: [] for k in ("kp", "vp", "ks", "vs", "sv", "cp", "cs", "ckp", "kpp", "cks", "kps")}
    for l in range(depth):
        x = _ffn(x, ffn1_norm[l], *w1, l, n_pt)
        if l % 2 == 0:
            e = l // 2
            q, kv, ug, kv_p = _inproj_even(x, mix_norm[l], even_w_in[e].astype(BF16), n_pt, s)
            mix_p = _mix_even_prompt(q, kv, ug, bias_p, attn_sinks[e], sgu_ln_g[e], sgu_ln_b[e], sgu_w[e],
                                     sgu_b[e], nb, s)
            q_b = to_b_major(q[n_p:].astype(F32)).reshape(n_s, A_Q_W)
            kvn_b = to_b_major(kv[n_p:]).reshape(n_s, 2 * A_KV_W)
            a_b, new_k, new_v = _mix_even_sample(q_b, kvn_b, cache_swa_k[e].reshape(bd, wb, A_KV_W),
                                                 cache_swa_v[e].reshape(bd, wb, A_KV_W), bias_s, attn_sinks[e], ts)
            a_t = a_b.reshape(bd, ts, A_Q_W).transpose(1, 0, 2).reshape(n_s, A_Q_W).astype(BF16)
            sg_s, v_rows = _sgu_sample(ug, sgu_ln_g[e], sgu_ln_b[e], sgu_w[e], sgu_b[e], n_p, ts, bd)
            mix_s = jnp.concatenate([a_t, sg_s], axis=1)
            w_out = (w_out_even, e)
            outs["kp"].append(kv_p[..., :A_KV_W].reshape(nb, wb, A_KV_HEADS, HEAD_DIM))
            outs["vp"].append(kv_p[..., A_KV_W:].reshape(nb, wb, A_KV_HEADS, HEAD_DIM))
            outs["ks"].append(new_k.reshape(bd, wb, A_KV_HEADS, HEAD_DIM))
            outs["vs"].append(new_v.reshape(bd, wb, A_KV_HEADS, HEAD_DIM))
            outs["sv"].append(to_b_major(v_rows))
        else:
            j = l // 2
            w_in_p, w_uq_r, wuk_ext, wuv_ext = _prep_odd_weights(odd_w_in[j], mla_w_uq[j], mla_w_uk[j], mla_w_uv[j])
            ci, qn, qp, kc, ckv_p, kpe_p, ckv_s, kpe_s = _inproj_odd(
                x, mix_norm[l], w_in_p, mla_q_norm[j], w_uq_r, mla_kv_norm[j], cos_t, sin_t, n_pt, s)
            mix_p, cst_p = _mix_odd_prompt(ci, qn, qp, kc, wuk_ext, wuv_ext, conv_w[j], nb, s)
            qx = _qx_sample(qn, qp, wuk_ext, n_p, n_s)
            qx_b = qx.reshape(MLA_HEADS * ts, bd, KC_W).transpose(1, 0, 2)
            kn_b = jnp.pad(to_b_major(kc[n_p:].astype(F32)), ((0, 0), (0, 8 - ts), (0, 0)))
            o_lat = _mla_sample(page_table, qx_b, kn_b, cache_mla_ckv, pool_kpet, j, ts)
            o_lat_t = o_lat.reshape(bd, MLA_HEADS, ts, KV_LORA).transpose(1, 2, 0, 3).reshape(
                MLA_HEADS, n_s, KV_LORA).astype(BF16)
            mix_s, cst_s = _odd_sample_tail(ci, state_conv[j].transpose(1, 0, 2), conv_w[j], o_lat_t, wuv_ext,
                                            n_p, ts, bd)
            w_out = (w_out_odd, j)
            outs["cp"].append(cst_p[:, 8 - (CONV_WIDTH - 1):])
            outs["cs"].append(cst_s.transpose(1, 0, 2))
            outs["ckp"].append(ckv_p.reshape(nb, s, KV_LORA))
            outs["kpp"].append(kpe_p.reshape(nb, s, ROPE_DIM))
            outs["cks"].append(to_b_major(ckv_s))
            outs["kps"].append(to_b_major(kpe_s))
        x = _ffn(x, ffn2_norm[l], *w2, l, n_pt, mix=(mix_p, mix_s) + w_out,
                 final_g=final_norm if l == depth - 1 else None)
    y_prompt = x[0].reshape(nb, s, d)
    y_sample = x[1].reshape(ts, bd, d).transpose(1, 0, 2)
    st = lambda k: jnp.stack(outs[k])
    return (y_prompt, y_sample, st("kp"), st("vp"), st("ks"), st("vs"), st("sv"), st("cp"), st("cs"),
            st("ckp"), st("kpp"), st("cks"), st("kps"))
```

```python
import functools
import math

import numpy as np
import jax
import jax.numpy as jnp
from jax import lax
from jax.experimental import pallas as pl
from jax.experimental.pallas import tpu as pltpu

F32 = jnp.float32
BF16 = jnp.bfloat16

HEAD_DIM = 64
A_HEADS = 8
A_KV_HEADS = 2
A_Q_W = A_HEADS * HEAD_DIM
A_KV_W = A_KV_HEADS * HEAD_DIM
WINDOW = 128
N_BUCKETS = 32
MAX_DISTANCE = 128
SG_GROUPS = 4
SG_GROUP_DIM = 128
SG_WIDTH = SG_GROUPS * SG_GROUP_DIM
CHUNK = 128
C_WIDTH = 512
CONV_WIDTH = 3
MLA_HEADS = 8
NOPE_DIM = 64
ROPE_DIM = 32
V_DIM = 64
Q_LORA = 768
KV_LORA = 256
ROPE_THETA = 10000.0
PAGE_SIZE = 128
EPS = 1e-6
NEG = -1e30

LANE = 128
KC_W = KV_LORA + LANE
TM = 512
FC = 512
TQ_EVEN = 512
TQ_ODD = 256
KB_ODD = 512
CHAINS_ODD = 2
BB_EVEN = 16
MLA_SAMPLE_CHUNK = 2048
MLA_SAMPLE_BATCHES = 2
MLA_SAMPLE_VMEM = 60 * 1024 * 1024
VMEM_LIMIT = 56 * 1024 * 1024


def _cp(sem, vmem=VMEM_LIMIT):
    return pltpu.CompilerParams(dimension_semantics=sem, vmem_limit_bytes=vmem)


def _dot(a, b):
    return jnp.dot(a, b, preferred_element_type=F32)


def _dot_nt(a, b):
    return lax.dot_general(a, b, (((1,), (1,)), ((), ())), preferred_element_type=F32)


def _rms(x, g):
    ms = jnp.mean(x * x, axis=-1, keepdims=True)
    return x * lax.rsqrt(ms + EPS) * g


def _gelu(x):
    return 0.5 * x * (1.0 + lax.erf(x * np.float32(math.sqrt(0.5))))


def _const_spec(shape):
    n = len(shape)
    return pl.BlockSpec(shape, lambda *_: (0,) * n, pipeline_mode=pl.Buffered(1))


def _smem_spec():
    return pl.BlockSpec(memory_space=pltpu.SMEM)


def _ffn_chunks(f):
    chunks = [(c, FC) for c in range(0, f - f % FC, FC)]
    if f % FC:
        chunks.append((f - f % FC, f % FC))
    return chunks


def _ffn_kernel(*refs, split_in, has_mix, has_final, n_prompt_tiles, chunks):
    it = iter(refs)
    x_ref = next(it)
    if split_in:
        xs_ref = next(it)
    if has_mix:
        mixp_ref, mixs_ref, wout_ref = next(it), next(it), next(it)
    g_ref, wg_ref, wu_ref, wd_ref = next(it), next(it), next(it), next(it)
    if has_final:
        fg_ref = next(it)
    o_ref = next(it)
    if has_final:
        os_ref = next(it)
    acc_ref = next(it)

    is_sample = pl.program_id(0) >= n_prompt_tiles
    x = x_ref[...]
    if split_in:
        x = jnp.where(is_sample, xs_ref[...], x)
    if has_mix:
        mix = jnp.where(is_sample, mixs_ref[...], mixp_ref[...])
        x = x + _dot(mix, wout_ref[...])
    h = _rms(x, g_ref[...]).astype(BF16)
    for k, (c0, cw) in enumerate(chunks):
        g = _dot(h, wg_ref[:, c0:c0 + cw])
        u = _dot(h, wu_ref[:, c0:c0 + cw])
        a = (g * jax.nn.sigmoid(g) * u).astype(BF16)
        d = _dot(a, wd_ref[c0:c0 + cw, :])
        if k == 0:
            acc_ref[...] = d
        else:
            acc_ref[...] += d
    y = x + 0.5 * acc_ref[...]
    if has_final:
        y = _rms(y, fg_ref[...])

        @pl.when(jnp.logical_not(is_sample))
        def _():
            o_ref[...] = y

        @pl.when(is_sample)
        def _():
            os_ref[...] = y
    else:
        o_ref[...] = y


def _layer_spec(shape, layer):
    n = len(shape)
    return pl.BlockSpec((None,) + tuple(shape), lambda *_: (layer,) + (0,) * n, pipeline_mode=pl.Buffered(1))


def _ffn(x, norm_g, w_gate, w_up, w_down, layer, n_prompt_tiles, mix=None, final_g=None):
    n_tiles = n_prompt_tiles + 1
    last_p = n_prompt_tiles - 1
    row = lambda i: (i, 0)
    prompt_row = lambda i: (jnp.minimum(i, last_p), 0)
    sample_row = lambda i: (0, 0)
    split_in = isinstance(x, tuple)
    d = x[0].shape[1] if split_in else x.shape[1]
    t = n_tiles * TM
    f = w_down.shape[1]
    if split_in:
        args, specs = list(x), [pl.BlockSpec((TM, d), prompt_row), pl.BlockSpec((TM, d), sample_row)]
    else:
        args, specs = [x], [pl.BlockSpec((TM, d), row)]
    if mix is not None:
        mix_p, mix_s, w_out, e = mix
        dm = w_out.shape[1]
        args += [mix_p, mix_s, w_out]
        specs += [pl.BlockSpec((TM, dm), prompt_row), pl.BlockSpec((TM, dm), sample_row),
                  _layer_spec((dm, d), e)]
    args += [norm_g.reshape(1, d), w_gate, w_up, w_down]
    specs += [_const_spec((1, d)), _layer_spec((d, f), layer), _layer_spec((d, f), layer),
              _layer_spec((f, d), layer)]
    if final_g is not None:
        args.append(final_g.reshape(1, d))
        specs.append(_const_spec((1, d)))
    body = functools.partial(_ffn_kernel, split_in=split_in, has_mix=mix is not None,
                             has_final=final_g is not None, n_prompt_tiles=n_prompt_tiles, chunks=_ffn_chunks(f))
    if final_g is not None:
        out_specs = [pl.BlockSpec((TM, d), prompt_row), pl.BlockSpec((TM, d), sample_row)]
        out_shape = [jax.ShapeDtypeStruct((n_prompt_tiles * TM, d), F32), jax.ShapeDtypeStruct((TM, d), F32)]
    else:
        out_specs = pl.BlockSpec((TM, d), row)
        out_shape = jax.ShapeDtypeStruct((t, d), F32)
    return pl.pallas_call(
        body,
        grid=(n_tiles,),
        in_specs=specs,
        out_specs=out_specs,
        out_shape=out_shape,
        scratch_shapes=[pltpu.VMEM((TM, d), F32)],
        compiler_params=_cp(("arbitrary",)),
        name="ffn",
    )(*args)


def _t5_bucket_np(dist):
    n = np.maximum(dist, 0)
    max_exact = N_BUCKETS // 2
    nf = np.maximum(n, 1).astype(np.float32)
    large = max_exact + (np.log(nf / np.float32(max_exact)) / np.float32(math.log(MAX_DISTANCE / max_exact))
                         * np.float32(N_BUCKETS - max_exact)).astype(np.int32)
    return np.where(n < max_exact, n, np.minimum(large, N_BUCKETS - 1)).astype(np.int32)


def _bias_tables_np(ts):
    q = np.arange(WINDOW)[:, None]
    j = np.arange(2 * WINDOW)[None, :]
    bucket_p = _t5_bucket_np(q - j + WINDOW)
    nb = 8 // ts
    cols = nb * WINDOW + 16
    bucket_s = -np.ones((8, cols), np.int32)
    for r in range(8):
        bi, t = divmod(r, ts)
        for c in range(cols):
            if c < nb * WINDOW:
                bj, jj = divmod(c, WINDOW)
                dist = t - (jj - WINDOW)
            elif c < nb * WINDOW + 8:
                bj, tt = divmod(c - nb * WINDOW, ts)
                dist = t - tt
            else:
                continue
            if bj == bi and 0 <= dist <= WINDOW:
                bucket_s[r, c] = _t5_bucket_np(np.array(dist))
    return bucket_p, bucket_s


def _bias_kernel(rb_ref, bp_ref, bs_ref, op_ref, os_ref):
    h = pl.program_id(0)
    bp = bp_ref[...]
    bs = bs_ref[...]
    accp = jnp.zeros(bp.shape, F32)
    accs = jnp.full(bs.shape, NEG, F32)
    for k in range(N_BUCKETS):
        val = rb_ref[k, h]
        accp = jnp.where(bp == k, val, accp)
        accs = jnp.where(bs == k, val, accs)
    op_ref[0] = accp
    os_ref[0] = accs


def _bias_tables(rel_bias, ts):
    bucket_p, bucket_s = _bias_tables_np(ts)
    return pl.pallas_call(
        _bias_kernel,
        grid=(A_HEADS,),
        in_specs=[_smem_spec(),
                  pl.BlockSpec(bucket_p.shape, lambda h: (0, 0)),
                  pl.BlockSpec(bucket_s.shape, lambda h: (0, 0))],
        out_specs=[pl.BlockSpec((1,) + bucket_p.shape, lambda h: (h, 0, 0)),
                   pl.BlockSpec((1,) + bucket_s.shape, lambda h: (h, 0, 0))],
        out_shape=[jax.ShapeDtypeStruct((A_HEADS,) + bucket_p.shape, F32),
                   jax.ShapeDtypeStruct((A_HEADS,) + bucket_s.shape, F32)],
        compiler_params=_cp(("arbitrary",)),
        name="t5_bias",
    )(rel_bias, jnp.asarray(bucket_p), jnp.asarray(bucket_s))


def _inproj_even_kernel(x_ref, g_ref, w_ref, q_ref, kv_ref, ug_ref, last_ref, *, tiles_per_seq, n_prompt_tiles):
    h = _rms(x_ref[...], g_ref[...]).astype(BF16)
    p = _dot(h, w_ref[...])
    q_ref[...] = (p[:, :A_Q_W] * np.float32(HEAD_DIM ** -0.5)).astype(BF16)
    kv = p[:, A_Q_W:A_Q_W + 2 * A_KV_W]
    kv_ref[...] = kv
    ug_ref[...] = p[:, A_Q_W + 2 * A_KV_W:]
    i = pl.program_id(0)

    @pl.when(((i + 1) % tiles_per_seq == 0) & (i < n_prompt_tiles))
    def _():
        last_ref[...] = kv[TM - WINDOW:]


def _inproj_even(x, norm_g, w_in, n_prompt_tiles, s):
    t, d = x.shape
    n = w_in.shape[1]
    row = lambda i: (i, 0)
    per_seq = s // TM
    nb = n_prompt_tiles // per_seq
    return pl.pallas_call(
        functools.partial(_inproj_even_kernel, tiles_per_seq=per_seq, n_prompt_tiles=n_prompt_tiles),
        grid=(t // TM,),
        in_specs=[pl.BlockSpec((TM, d), row), _const_spec((1, d)), _const_spec((d, n))],
        out_specs=[pl.BlockSpec((TM, A_Q_W), row), pl.BlockSpec((TM, 2 * A_KV_W), row),
                   pl.BlockSpec((TM, 2 * SG_WIDTH), row),
                   pl.BlockSpec((None, WINDOW, 2 * A_KV_W), lambda i: (jnp.minimum(i // per_seq, nb - 1), 0, 0))],
        out_shape=[jax.ShapeDtypeStruct((t, A_Q_W), BF16), jax.ShapeDtypeStruct((t, 2 * A_KV_W), F32),
                   jax.ShapeDtypeStruct((t, 2 * SG_WIDTH), F32),
                   jax.ShapeDtypeStruct((nb, WINDOW, 2 * A_KV_W), F32)],
        compiler_params=_cp(("arbitrary",)),
        name="inproj_even",
    )(x, norm_g.reshape(1, d), w_in)


def _head_variants(x):
    lo = lax.broadcasted_iota(jnp.int32, x.shape, 1) < HEAD_DIM
    xr = pltpu.roll(x, HEAD_DIM, 1)
    z = jnp.zeros_like(x)
    return [[jnp.where(lo, x, z).astype(BF16), jnp.where(lo, z, xr).astype(BF16)],
            [jnp.where(lo, xr, z).astype(BF16), jnp.where(lo, z, x).astype(BF16)]]


def _sink_softmax_pv(s, sink, v):
    mx = jnp.maximum(jnp.max(s, axis=-1, keepdims=True), sink)
    p = jnp.exp(s - mx)
    den = jnp.sum(p, axis=-1, keepdims=True) + jnp.exp(sink - mx)
    return _dot(p.astype(BF16), v) * (1.0 / den)


def _layer_norm(x, g, b):
    xc = x - jnp.mean(x, axis=-1, keepdims=True)
    y = xc * lax.rsqrt(jnp.mean(xc * xc, axis=-1, keepdims=True) + EPS)
    return y * g + b


def _even_prompt_kernel(q_ref, kv_ref, kvp_ref, ug_ref, bias_ref, sink_ref, lng_ref, lnb_ref, ws_ref,
                        bst_ref, o_ref, *, tq):
    nblk = tq // WINDOW
    step = pl.program_id(1)
    kv_all = jnp.concatenate([kvp_ref[...], kv_ref[...]], axis=0)
    kvar = _head_variants(kv_all[:, :A_KV_W])
    vvar = _head_variants(kv_all[:, A_KV_W:])
    qi = lax.broadcasted_iota(jnp.int32, (WINDOW, 2 * WINDOW), 0)
    kj = lax.broadcasted_iota(jnp.int32, (WINDOW, 2 * WINDOW), 1)
    dist = qi - kj + WINDOW
    band = (dist >= 0) & (dist <= WINDOW)
    cur = kj >= WINDOW
    ri = lax.broadcasted_iota(jnp.int32, (CHUNK, CHUNK), 0)
    ci = lax.broadcasted_iota(jnp.int32, (CHUNK, CHUNK), 1)
    wt = [jnp.where(ri >= ci, ws_ref[g], 0.0).astype(BF16) for g in range(SG_GROUPS)]
    lng, lnb = lng_ref[...], lnb_ref[...]
    for n in range(nblk):
        r0 = n * WINDOW
        has_prev = (step * nblk + n) > 0
        mask = band & (cur | has_prev)
        for m in range(A_HEADS // 2):
            kvh = (2 * m) // (A_HEADS // A_KV_HEADS)
            qp = q_ref[r0:r0 + WINDOW, LANE * m:LANE * (m + 1)]
            o = None
            for half in range(2):
                hd = 2 * m + half
                s = _dot_nt(qp, kvar[kvh][half][r0:r0 + 2 * WINDOW])
                s = jnp.where(mask, s + bias_ref[hd], NEG)
                oh = _sink_softmax_pv(s, sink_ref[hd], vvar[kvh][half][r0:r0 + 2 * WINDOW])
                o = oh if o is None else o + oh
            o_ref[r0:r0 + WINDOW, LANE * m:LANE * (m + 1)] = o.astype(BF16)
        u = _gelu(ug_ref[r0:r0 + WINDOW, :SG_WIDTH])
        v = _layer_norm(_gelu(ug_ref[r0:r0 + WINDOW, SG_WIDTH:]), lng, lnb)
        for g in range(SG_GROUPS):
            sl = slice(SG_GROUP_DIM * g, SG_GROUP_DIM * (g + 1))
            y = _dot(wt[g], v[:, sl].astype(BF16)) + bst_ref[:, g:g + 1]
            o_ref[r0:r0 + WINDOW, A_Q_W + SG_GROUP_DIM * g:A_Q_W + SG_GROUP_DIM * (g + 1)] = (
                u[:, sl] * y).astype(BF16)


def _mix_even_prompt(q, kv, ug, bias_p, sinks, ln_g, ln_b, w_s, b_s, nb, s):
    tq = TQ_EVEN
    nq = s // tq
    blk = lambda b, i: (b * nq + i, 0)
    prev = lambda b, i: (jnp.maximum((b * nq + i) * (tq // WINDOW) - 1, 0), 0)
    return pl.pallas_call(
        functools.partial(_even_prompt_kernel, tq=tq),
        grid=(nb, nq),
        in_specs=[pl.BlockSpec((tq, A_Q_W), blk), pl.BlockSpec((tq, 2 * A_KV_W), blk),
                  pl.BlockSpec((WINDOW, 2 * A_KV_W), prev), pl.BlockSpec((tq, 2 * SG_WIDTH), blk),
                  _const_spec(bias_p.shape), _smem_spec(), _const_spec((1, SG_WIDTH)),
                  _const_spec((1, SG_WIDTH)), _const_spec(w_s.shape), _const_spec((CHUNK, SG_GROUPS))],
        out_specs=pl.BlockSpec((tq, A_Q_W + SG_WIDTH), blk),
        out_shape=jax.ShapeDtypeStruct((nb * s, A_Q_W + SG_WIDTH), BF16),
        compiler_params=_cp(("arbitrary", "arbitrary")),
        name="mix_even_prompt",
    )(q, kv, kv, ug, bias_p, sinks, ln_g.reshape(1, -1), ln_b.reshape(1, -1), w_s, b_s.T)


def _even_sample_kernel(q_ref, kvn_ref, ck_ref, cv_ref, bias_ref, sink_ref, a_ref, nk_ref, nv_ref, *, ts):
    bb = ck_ref.shape[0]
    gpb = 8 // ts
    wb = ck_ref.shape[1]
    n_cache = gpb * wb
    hpk = A_HEADS // A_KV_HEADS
    zpad = jnp.zeros((8, A_KV_W), F32)

    def lo_half(x, kvh):
        lo = lax.broadcasted_iota(jnp.int32, x.shape, 1) < HEAD_DIM
        return jnp.where(lo, x if kvh == 0 else pltpu.roll(x, HEAD_DIM, 1), 0.0).astype(BF16)

    hrow = lax.broadcasted_iota(jnp.int32, (8 * hpk, 1), 0) // 8
    bias_c, bias_n, sink = [], [], []
    for kvh in range(A_KV_HEADS):
        heads = [hpk * kvh + i for i in range(hpk)]
        bias_c.append(jnp.concatenate([bias_ref[h][:, :n_cache] for h in heads], axis=0))
        bias_n.append(jnp.concatenate([bias_ref[h][:, n_cache:] for h in heads], axis=0))
        sk = jnp.full((8 * hpk, 1), sink_ref[heads[0]], F32)
        for i in range(1, hpk):
            sk = jnp.where(hrow == i, sink_ref[heads[i]], sk)
        sink.append(sk)
    for g in range(bb // gpb):
        r0 = 8 * g
        qg = q_ref[r0:r0 + 8, :]
        kn = kvn_ref[r0:r0 + 8, :A_KV_W]
        vn = kvn_ref[r0:r0 + 8, A_KV_W:]
        kc = jnp.concatenate([ck_ref[gpb * g + i] for i in range(gpb)], axis=0)
        vc = jnp.concatenate([cv_ref[gpb * g + i] for i in range(gpb)], axis=0)
        knp = jnp.concatenate([kn, zpad], axis=0)
        vnp = jnp.concatenate([vn, zpad], axis=0)
        for kvh in range(A_KV_HEADS):
            c0 = 2 * LANE * kvh
            qs = jnp.concatenate([lo_half(qg[:, c0 + LANE * (i // 2):c0 + LANE * (i // 2 + 1)], i % 2)
                                  for i in range(hpk)], axis=0)
            sc = _dot_nt(qs, lo_half(kc, kvh))
            sn = _dot_nt(qs, lo_half(knp, kvh))
            sc = jnp.where(bias_c[kvh] > 0.5 * NEG, sc + bias_c[kvh], NEG)
            sn = jnp.where(bias_n[kvh] > 0.5 * NEG, sn + bias_n[kvh], NEG)
            mx = jnp.maximum(jnp.maximum(jnp.max(sc, axis=-1, keepdims=True), jnp.max(sn, axis=-1, keepdims=True)),
                             sink[kvh])
            pc = jnp.exp(sc - mx)
            pn = jnp.exp(sn - mx)
            den = (jnp.sum(pc, axis=-1, keepdims=True) + jnp.sum(pn, axis=-1, keepdims=True)
                   + jnp.exp(sink[kvh] - mx))
            o = (_dot(pc.astype(BF16), lo_half(vc, kvh)) + _dot(pn.astype(BF16), lo_half(vnp, kvh))) * (1.0 / den)
            for i in range(hpk // 2):
                pair = o[16 * i:16 * i + 8] + pltpu.roll(o[16 * i + 8:16 * i + 16], HEAD_DIM, 1)
                a_ref[r0:r0 + 8, c0 + LANE * i:c0 + LANE * (i + 1)] = pair
        for i in range(gpb):
            b = gpb * g + i
            nk_ref[b, 0:wb - ts, :] = ck_ref[b, ts:wb, :]
            nk_ref[b, wb - ts:wb, :] = kn[ts * i:ts * (i + 1)]
            nv_ref[b, 0:wb - ts, :] = cv_ref[b, ts:wb, :]
            nv_ref[b, wb - ts:wb, :] = vn[ts * i:ts * (i + 1)]


def _mix_even_sample(q_b, kvn_b, cache_k, cache_v, bias_s, sinks, ts):
    bd, wb, _ = cache_k.shape
    bb = BB_EVEN
    rows = bb * ts
    blk2 = lambda i: (i, 0)
    blk3 = lambda i: (i, 0, 0)
    return pl.pallas_call(
        functools.partial(_even_sample_kernel, ts=ts),
        grid=(bd // bb,),
        in_specs=[pl.BlockSpec((rows, A_Q_W), blk2), pl.BlockSpec((rows, 2 * A_KV_W), blk2),
                  pl.BlockSpec((bb, wb, A_KV_W), blk3), pl.BlockSpec((bb, wb, A_KV_W), blk3),
                  _const_spec(bias_s.shape), _smem_spec()],
        out_specs=[pl.BlockSpec((rows, A_Q_W), blk2), pl.BlockSpec((bb, wb, A_KV_W), blk3),
                   pl.BlockSpec((bb, wb, A_KV_W), blk3)],
        out_shape=[jax.ShapeDtypeStruct((bd * ts, A_Q_W), F32),
                   jax.ShapeDtypeStruct(cache_k.shape, F32), jax.ShapeDtypeStruct(cache_v.shape, F32)],
        compiler_params=_cp(("arbitrary",)),
        name="mix_even_sample",
    )(q_b, kvn_b, cache_k, cache_v, bias_s, sinks)


def _sgu_sample_kernel(ug_ref, lng_ref, lnb_ref, w_ref, b_ref, sg_ref, v_ref, *, ts, bd):
    u = _gelu(ug_ref[:, :SG_WIDTH])
    v = _layer_norm(_gelu(ug_ref[:, SG_WIDTH:]), lng_ref[...], lnb_ref[...])
    v_ref[...] = v
    for i in range(ts):
        for g in range(SG_GROUPS):
            sl = slice(SG_GROUP_DIM * g, SG_GROUP_DIM * (g + 1))
            y = jnp.full((bd, SG_GROUP_DIM), b_ref[g, i], F32)
            for j in range(i + 1):
                y = y + w_ref[g, i * ts + j] * v[bd * j:bd * (j + 1), sl]
            sg_ref[bd * i:bd * (i + 1), sl] = (u[bd * i:bd * (i + 1), sl] * y).astype(BF16)


def _sgu_sample(ug, ln_g, ln_b, w_s, b_s, n_prompt_rows, ts, bd):
    rows = ts * bd
    off = n_prompt_rows // rows
    w_small = w_s[:, :ts, :ts].reshape(SG_GROUPS, ts * ts)
    b_small = b_s[:, :ts]
    return pl.pallas_call(
        functools.partial(_sgu_sample_kernel, ts=ts, bd=bd),
        grid=(1,),
        in_specs=[pl.BlockSpec((rows, 2 * SG_WIDTH), lambda i: (off, 0)), _const_spec((1, SG_WIDTH)),
                  _const_spec((1, SG_WIDTH)), _smem_spec(), _smem_spec()],
        out_specs=[pl.BlockSpec((rows, SG_WIDTH), lambda i: (0, 0)),
                   pl.BlockSpec((rows, SG_WIDTH), lambda i: (0, 0))],
        out_shape=[jax.ShapeDtypeStruct((rows, SG_WIDTH), BF16), jax.ShapeDtypeStruct((rows, SG_WIDTH), F32)],
        compiler_params=_cp(("arbitrary",)),
        name="sgu_sample",
    )(ug, ln_g.reshape(1, -1), ln_b.reshape(1, -1), w_small, b_small)


def _rope_tables_np(s, past, ts, bd):
    half = ROPE_DIM // 2
    inv = np.exp(-math.log(ROPE_THETA) * np.arange(half, dtype=np.float32) / np.float32(half)).astype(np.float32)
    pos = np.concatenate([np.arange(s), past + np.repeat(np.arange(ts), bd)]).astype(np.float32)
    ang = (pos[:, None] * inv[None, :]).astype(np.float32).astype(np.float64)
    cos = np.cos(ang)
    sin = np.sin(ang)
    cos_t = np.tile(np.concatenate([cos, cos], axis=1), (1, MLA_HEADS)).astype(np.float32)
    sin_t = np.tile(np.concatenate([-sin, sin], axis=1), (1, MLA_HEADS)).astype(np.float32)
    return cos_t, sin_t


def _rope(x, cos_t, sin_t):
    n = x.shape[1]
    half = ROPE_DIM // 2
    first = (lax.broadcasted_iota(jnp.int32, x.shape, 1) % ROPE_DIM) < half
    partner = jnp.where(first, pltpu.roll(x, n - half, 1), pltpu.roll(x, half, 1))
    return x * cos_t + partner * sin_t


def _inproj_odd_kernel(x_ref, g_ref, w_ref, qg_ref, wuq_ref, kvg_ref, cos_ref, sin_ref,
                       ci_ref, qn_ref, qp_ref, kc_ref, ckvp_ref, kpep_ref, ckvs_ref, kpes_ref, *, n_prompt_tiles):
    h = _rms(x_ref[...], g_ref[...]).astype(BF16)
    p = _dot(h, w_ref[...])
    o_q = 3 * C_WIDTH
    o_kv = o_q + Q_LORA
    o_pe = o_kv + KV_LORA
    ci_ref[...] = p[:, :o_q].astype(BF16)
    scale = np.float32((NOPE_DIM + ROPE_DIM) ** -0.5 * math.log2(math.e))
    q = _dot(_rms(p[:, o_q:o_kv], qg_ref[...]).astype(BF16), wuq_ref[...])
    n_nope = MLA_HEADS * NOPE_DIM
    cos_t, sin_t = cos_ref[...], sin_ref[...]
    qn_ref[...] = (q[:, :n_nope] * scale).astype(BF16)
    qp_ref[...] = (_rope(q[:, n_nope:], cos_t, sin_t) * scale).astype(BF16)
    ckv = _rms(p[:, o_kv:o_pe], kvg_ref[...])
    kpe = _rope(p[:, o_pe:], cos_t[:, :LANE], sin_t[:, :LANE])
    kc_ref[:, :KV_LORA] = ckv.astype(BF16)
    kc_ref[:, KV_LORA:] = kpe.astype(BF16)
    is_sample = pl.program_id(0) >= n_prompt_tiles

    @pl.when(jnp.logical_not(is_sample))
    def _():
        ckvp_ref[...] = ckv
        kpep_ref[...] = kpe[:, :ROPE_DIM]

    @pl.when(is_sample)
    def _():
        ckvs_ref[...] = ckv
        kpes_ref[...] = kpe[:, :ROPE_DIM]


def _inproj_odd(x, norm_g, w_in, q_norm, w_uq, kv_norm, cos_t, sin_t, n_prompt_tiles, s):
    t, d = x.shape
    n = w_in.shape[1]
    row = lambda i: (i, 0)
    last_p = n_prompt_tiles - 1
    prompt_row = lambda i: (jnp.minimum(i, last_p), 0)
    sample_row = lambda i: (0, 0)
    per_seq = s // TM
    tab = lambda i: (jnp.where(i < n_prompt_tiles, i % per_seq, per_seq), 0)
    n_pe = MLA_HEADS * ROPE_DIM
    n_p = n_prompt_tiles * TM
    widths = [(3 * C_WIDTH, BF16), (MLA_HEADS * NOPE_DIM, BF16), (n_pe, BF16), (KC_W, BF16)]
    return pl.pallas_call(
        functools.partial(_inproj_odd_kernel, n_prompt_tiles=n_prompt_tiles),
        grid=(t // TM,),
        in_specs=[pl.BlockSpec((TM, d), row), _const_spec((1, d)), _const_spec((d, n)),
                  _const_spec((1, Q_LORA)), _const_spec(w_uq.shape), _const_spec((1, KV_LORA)),
                  pl.BlockSpec((TM, n_pe), tab), pl.BlockSpec((TM, n_pe), tab)],
        out_specs=[pl.BlockSpec((TM, w), row) for w, _ in widths] + [
            pl.BlockSpec((TM, KV_LORA), prompt_row), pl.BlockSpec((TM, ROPE_DIM), prompt_row),
            pl.BlockSpec((TM, KV_LORA), sample_row), pl.BlockSpec((TM, ROPE_DIM), sample_row)],
        out_shape=[jax.ShapeDtypeStruct((t, w), dt) for w, dt in widths] + [
            jax.ShapeDtypeStruct((n_p, KV_LORA), F32), jax.ShapeDtypeStruct((n_p, ROPE_DIM), F32),
            jax.ShapeDtypeStruct((TM, KV_LORA), F32), jax.ShapeDtypeStruct((TM, ROPE_DIM), F32)],
        compiler_params=_cp(("arbitrary",)),
        name="inproj_odd",
    )(x, norm_g.reshape(1, d), w_in, q_norm.reshape(1, -1), w_uq, kv_norm.reshape(1, -1), cos_t, sin_t)


def _build_qx(qn_ref, qp_ref, wuk_ref, qx_ref, rows):
    qp = qp_ref[...].astype(F32)
    keep = lax.broadcasted_iota(jnp.int32, (rows, LANE), 1) < ROPE_DIM
    n_pe = MLA_HEADS * ROPE_DIM
    for h in range(MLA_HEADS):
        pair = qn_ref[:, LANE * (h // 2):LANE * (h // 2 + 1)]
        qx_ref[h * rows:(h + 1) * rows, :KV_LORA] = _dot(pair, wuk_ref[h]).astype(BF16)
        pe = qp if h == 0 else pltpu.roll(qp, n_pe - ROPE_DIM * h, 1)
        qx_ref[h * rows:(h + 1) * rows, KV_LORA:] = jnp.where(keep, pe[:, :LANE], 0.0).astype(BF16)


def _odd_prompt_kernel(ci_ref, cih_ref, qn_ref, qp_ref, kc_ref, wuk_ref, wuv_ref, cw_ref,
                       o_ref, cst_ref, qx_ref, m_ref, l_ref, acc_ref, z_ref, *, tq, kb):
    step = pl.program_id(1)
    halo = 16
    zh = cih_ref[:, C_WIDTH:2 * C_WIDTH].astype(F32) * cih_ref[:, 2 * C_WIDTH:].astype(F32)
    z_ref[0:halo, :] = jnp.where(step > 0, zh, 0.0)
    z = ci_ref[:, C_WIDTH:2 * C_WIDTH].astype(F32) * ci_ref[:, 2 * C_WIDTH:].astype(F32)
    z_ref[halo:halo + tq, :] = z
    cw = cw_ref[...]
    y = cw[0:1] * z_ref[halo - 2:halo - 2 + tq, :] + cw[1:2] * z_ref[halo - 1:halo - 1 + tq, :] + cw[2:3] * z
    o_ref[:, :C_WIDTH] = (ci_ref[:, :C_WIDTH].astype(F32) * y).astype(BF16)
    cst_ref[0] = z[tq - 8:tq]

    _build_qx(qn_ref, qp_ref, wuk_ref, qx_ref, tq)
    rows = MLA_HEADS * tq
    n_full = (step * tq) // kb
    hpc = MLA_HEADS // CHAINS_ODD

    def kv_step(j, width, mask_from, first):
        k0 = j * kb if isinstance(j, int) else pl.multiple_of(j * kb, kb)
        kc = kc_ref[pl.ds(k0, width), :]
        n_grp = width // LANE
        if mask_from < n_grp:
            qpos = step * tq + lax.broadcasted_iota(jnp.int32, (tq, LANE), 0)
            kpos = k0 + lax.broadcasted_iota(jnp.int32, (tq, LANE), 1)
            visible = {g: kpos + LANE * g <= qpos for g in range(mask_from, n_grp)}
        for c in range(CHAINS_ODD):
            rc = slice(c * hpc * tq, (c + 1) * hpc * tq)
            s_all = _dot_nt(qx_ref[rc, :], kc)
            ps, alphas = [], []
            for hh in range(hpc):
                r = slice((c * hpc + hh) * tq, (c * hpc + hh + 1) * tq)
                sg = [s_all[hh * tq:(hh + 1) * tq, LANE * g:LANE * (g + 1)] for g in range(n_grp)]
                sg = [jnp.where(visible[g], x, NEG) if g >= mask_from else x for g, x in enumerate(sg)]
                m_blk = jnp.max(functools.reduce(jnp.maximum, sg), axis=-1, keepdims=True)
                if first:
                    m_new = jnp.broadcast_to(m_blk, (tq, LANE))
                else:
                    m_old = m_ref[r, :]
                    m_new = jnp.maximum(m_old, m_blk)
                    alpha = jnp.exp2(m_old - m_new)
                    alphas.append(jnp.concatenate([alpha] * (KV_LORA // LANE), axis=1))
                pg = [jnp.exp2(x - m_new) for x in sg]
                psum = functools.reduce(jnp.add, pg)
                l_ref[r, :] = psum if first else alpha * l_ref[r, :] + psum
                m_ref[r, :] = m_new
                ps.append(jnp.concatenate([x.astype(BF16) for x in pg], axis=1))
            pv = _dot(jnp.concatenate(ps, axis=0), kc[:, :KV_LORA])
            acc_ref[rc, :] = pv if first else jnp.concatenate(alphas, axis=0) * acc_ref[rc, :] + pv

    @pl.when(n_full >= 2)
    def _():
        kv_step(0, kb, kb // LANE, True)

    def full_step(j, carry):
        kv_step(j, kb, kb // LANE, False)
        return carry

    lax.fori_loop(1, n_full - 1, full_step, 0)
    for i, width in enumerate(range(tq, kb + 1, tq)):
        on_diag = step % (kb // tq) == i

        @pl.when(on_diag & (n_full == 0))
        def _(width=width):
            kv_step(0, width, 0, True)

        @pl.when(on_diag & (n_full == 1))
        def _(width=width):
            kv_step(0, kb + width, kb // LANE, True)

        @pl.when(on_diag & (n_full >= 2))
        def _(width=width):
            kv_step(n_full - 1, kb + width, kb // LANE, False)

    for m in range(MLA_HEADS // 2):
        o = None
        for half in range(2):
            h = 2 * m + half
            inv_l = jnp.broadcast_to(1.0 / jnp.sum(l_ref[h * tq:(h + 1) * tq, :], axis=-1, keepdims=True), (tq, LANE))
            ol = (acc_ref[h * tq:(h + 1) * tq, :] * jnp.concatenate([inv_l] * (KV_LORA // LANE), axis=1)).astype(BF16)
            oh = _dot(ol, wuv_ref[h])
            o = oh if o is None else o + oh
        o_ref[:, C_WIDTH + LANE * m:C_WIDTH + LANE * (m + 1)] = o.astype(BF16)


def _mix_odd_prompt(ci, qn, qp, kc, wuk_ext, wuv_ext, conv_w, nb, s):
    tq, kb = TQ_ODD, KB_ODD
    nq = s // tq
    blk = lambda b, i: (b * nq + i, 0)
    halo = lambda b, i: (jnp.maximum((b * nq + i) * (tq // 16) - 1, 0), 0)
    rows = MLA_HEADS * tq
    n_out = C_WIDTH + MLA_HEADS * V_DIM
    return pl.pallas_call(
        functools.partial(_odd_prompt_kernel, tq=tq, kb=kb),
        grid=(nb, nq),
        in_specs=[pl.BlockSpec((tq, 3 * C_WIDTH), blk), pl.BlockSpec((16, 3 * C_WIDTH), halo),
                  pl.BlockSpec((tq, MLA_HEADS * NOPE_DIM), blk), pl.BlockSpec((tq, MLA_HEADS * ROPE_DIM), blk),
                  pl.BlockSpec((s, KC_W), lambda b, i: (b, 0)),
                  _const_spec(wuk_ext.shape), _const_spec(wuv_ext.shape), _const_spec(conv_w.shape)],
        out_specs=[pl.BlockSpec((tq, n_out), blk), pl.BlockSpec((1, 8, C_WIDTH), lambda b, i: (b, 0, 0))],
        out_shape=[jax.ShapeDtypeStruct((nb * s, n_out), BF16), jax.ShapeDtypeStruct((nb, 8, C_WIDTH), F32)],
        scratch_shapes=[pltpu.VMEM((rows, KC_W), BF16),
                        pltpu.VMEM((rows, LANE), F32), pltpu.VMEM((rows, LANE), F32), pltpu.VMEM((rows, KV_LORA), F32),
                        pltpu.VMEM((tq + 16, C_WIDTH), F32)],
        compiler_params=_cp(("arbitrary", "arbitrary")),
        name="mix_odd_prompt",
    )(ci, ci, qn, qp, kc, wuk_ext, wuv_ext, conv_w)


def _qx_sample_kernel(qn_ref, qp_ref, wuk_ref, qx_ref, *, rows):
    _build_qx(qn_ref, qp_ref, wuk_ref, qx_ref, rows)


def _qx_sample(qn, qp, wuk_ext, n_prompt_rows, rows):
    off = n_prompt_rows // rows
    return pl.pallas_call(
        functools.partial(_qx_sample_kernel, rows=rows),
        grid=(1,),
        in_specs=[pl.BlockSpec((rows, MLA_HEADS * NOPE_DIM), lambda i: (off, 0)),
                  pl.BlockSpec((rows, MLA_HEADS * ROPE_DIM), lambda i: (off, 0)), _const_spec(wuk_ext.shape)],
        out_specs=pl.BlockSpec((MLA_HEADS * rows, KC_W), lambda i: (0, 0)),
        out_shape=jax.ShapeDtypeStruct((MLA_HEADS * rows, KC_W), BF16),
        compiler_params=_cp(("arbitrary",)),
        name="qx_sample",
    )(qn, qp, wuk_ext)


def _page_copies(pt_ref, pool_ckv, pool_kpet, ckv_buf, kpe_buf, sem, layer, n_pages, step, slot):
    copies = []
    for i in range(MLA_SAMPLE_BATCHES):
        b = step * MLA_SAMPLE_BATCHES + i
        for p in range(n_pages):
            page = pt_ref[b, p]
            keys = pl.ds(p * PAGE_SIZE, PAGE_SIZE)
            copies.append(pltpu.make_async_copy(pool_ckv.at[layer, page], ckv_buf.at[slot, i, keys],
                                                sem.at[0, slot]))
            copies.append(pltpu.make_async_copy(pool_kpet.at[layer, page], kpe_buf.at[slot, i, :, keys],
                                                sem.at[1, slot]))
    return copies


def _mla_sample_kernel(pt_ref, qx_ref, kn_ref, pool_ckv, pool_kpet, o_ref, ckv_buf, kpe_buf, sem, *,
                       ts, n_pages, layer, chunk):
    step = pl.program_id(0)
    n_steps = pl.num_programs(0)
    slot = step % 2
    copies = functools.partial(_page_copies, pt_ref, pool_ckv, pool_kpet, ckv_buf, kpe_buf, sem, layer, n_pages)

    @pl.when(step == 0)
    def _():
        for c in copies(0, 0):
            c.start()

    @pl.when(step + 1 < n_steps)
    def _():
        for c in copies(step + 1, 1 - slot):
            c.start()

    for c in copies(step, slot):
        c.wait()

    rows = qx_ref.shape[1]
    t_q = lax.broadcasted_iota(jnp.int32, (rows, 8), 0) % ts
    t_k = lax.broadcasted_iota(jnp.int32, (rows, 8), 1)
    past = n_pages * PAGE_SIZE
    starts = range(0, past, chunk)
    for i in range(MLA_SAMPLE_BATCHES):
        qx = qx_ref[i]
        q_lat = qx[:, :KV_LORA]
        q_pe = qx[:, KV_LORA:KV_LORA + ROPE_DIM]
        kn = kn_ref[i].astype(BF16)
        sn = jnp.where(t_k <= t_q, _dot_nt(qx, kn), NEG)
        kcs = [ckv_buf[slot, i, k0:k0 + chunk, :].astype(BF16) for k0 in starts]
        ss = [_dot_nt(q_lat, kc) + _dot(q_pe, kpe_buf[slot, i, :, k0:k0 + chunk].astype(BF16))
              for k0, kc in zip(starts, kcs)]
        m = functools.reduce(jnp.maximum, [jnp.max(s, axis=-1, keepdims=True) for s in ss],
                             jnp.max(sn, axis=-1, keepdims=True))
        pn = jnp.exp2(sn - m)
        den = jnp.sum(pn, axis=-1, keepdims=True)
        pnb = pn.astype(BF16)
        accs = [_dot(pnb, kn[:, LANE * g:LANE * (g + 1)]) for g in range(KV_LORA // LANE)]
        for s, kc in zip(ss, kcs):
            p = jnp.exp2(s - m)
            den = den + jnp.sum(p, axis=-1, keepdims=True)
            pb = p.astype(BF16)
            accs = [a + _dot(pb, kc[:, LANE * g:LANE * (g + 1)]) for g, a in enumerate(accs)]
        o_ref[i] = jnp.concatenate(accs, axis=1) * (1.0 / den)


def _mla_sample(page_table, qx_b, kn_b, pool_ckv, pool_kpet, layer, ts):
    bd, n_pages = page_table.shape
    past = n_pages * PAGE_SIZE
    rows = MLA_HEADS * ts
    nbs = MLA_SAMPLE_BATCHES
    blk = lambda g, pt: (g, 0, 0)
    grid_spec = pltpu.PrefetchScalarGridSpec(
        num_scalar_prefetch=1,
        grid=(bd // nbs,),
        in_specs=[pl.BlockSpec((nbs, rows, KC_W), blk), pl.BlockSpec((nbs, 8, KC_W), blk),
                  pl.BlockSpec(memory_space=pl.ANY), pl.BlockSpec(memory_space=pl.ANY)],
        out_specs=pl.BlockSpec((nbs, rows, KV_LORA), blk),
        scratch_shapes=[pltpu.VMEM((2, nbs, past, KV_LORA), F32), pltpu.VMEM((2, nbs, ROPE_DIM, past), F32),
                        pltpu.SemaphoreType.DMA((2, 2))],
    )
    return pl.pallas_call(
        functools.partial(_mla_sample_kernel, ts=ts, n_pages=n_pages, layer=layer,
                          chunk=min(MLA_SAMPLE_CHUNK, past)),
        grid_spec=grid_spec,
        out_shape=jax.ShapeDtypeStruct((bd, rows, KV_LORA), F32),
        compiler_params=_cp(("arbitrary",), vmem=MLA_SAMPLE_VMEM),
        name="mla_sample",
    )(page_table, qx_b, kn_b, pool_ckv, pool_kpet)


def _odd_sample_tail_kernel(ci_ref, st_ref, cw_ref, ol_ref, wuv_ref, o_ref, nst_ref, *, ts, bd):
    cw = cw_ref[...]
    zz = [st_ref[0], st_ref[1]]
    for t in range(ts):
        r = slice(bd * t, bd * (t + 1))
        zz.append(ci_ref[r, C_WIDTH:2 * C_WIDTH].astype(F32) * ci_ref[r, 2 * C_WIDTH:].astype(F32))
    for t in range(ts):
        r = slice(bd * t, bd * (t + 1))
        y = cw[0:1] * zz[t] + cw[1:2] * zz[t + 1] + cw[2:3] * zz[t + 2]
        o_ref[r, :C_WIDTH] = (ci_ref[r, :C_WIDTH].astype(F32) * y).astype(BF16)
    nst_ref[0] = zz[ts]
    nst_ref[1] = zz[ts + 1]
    for m in range(MLA_HEADS // 2):
        o = _dot(ol_ref[2 * m], wuv_ref[2 * m]) + _dot(ol_ref[2 * m + 1], wuv_ref[2 * m + 1])
        o_ref[:, C_WIDTH + LANE * m:C_WIDTH + LANE * (m + 1)] = o.astype(BF16)


def _odd_sample_tail(ci, state_t, conv_w, o_lat_t, wuv_ext, n_prompt_rows, ts, bd):
    rows = ts * bd
    off = n_prompt_rows // rows
    n_out = C_WIDTH + MLA_HEADS * V_DIM
    return pl.pallas_call(
        functools.partial(_odd_sample_tail_kernel, ts=ts, bd=bd),
        grid=(1,),
        in_specs=[pl.BlockSpec((rows, 3 * C_WIDTH), lambda i: (off, 0)), _const_spec(state_t.shape),
                  _const_spec(conv_w.shape), _const_spec(o_lat_t.shape), _const_spec(wuv_ext.shape)],
        out_specs=[pl.BlockSpec((rows, n_out), lambda i: (0, 0)),
                   pl.BlockSpec(state_t.shape, lambda i: (0, 0, 0))],
        out_shape=[jax.ShapeDtypeStruct((rows, n_out), BF16), jax.ShapeDtypeStruct(state_t.shape, F32)],
        compiler_params=_cp(("arbitrary",)),
        name="odd_sample_tail",
    )(ci, state_t, conv_w, o_lat_t, wuv_ext)


def _prep_odd_weights(w_in, w_uq, w_uk, w_uv):
    d, n = w_in.shape
    n_pad = -n % LANE
    w_in_p = jnp.pad(w_in, ((0, 0), (0, n_pad))).astype(BF16)
    per = NOPE_DIM + ROPE_DIM
    wq = w_uq.reshape(Q_LORA, MLA_HEADS, per)
    w_uq_r = jnp.concatenate([wq[:, :, :NOPE_DIM].reshape(Q_LORA, -1), wq[:, :, NOPE_DIM:].reshape(Q_LORA, -1)],
                             axis=1).astype(BF16)
    ukt = jnp.transpose(w_uk, (1, 2, 0))
    uvt = jnp.transpose(w_uv, (1, 0, 2))
    wuk_ext, wuv_ext = [], []
    for h in range(MLA_HEADS):
        lo = (h % 2) * NOPE_DIM
        wuk_ext.append(jnp.pad(ukt[h], ((lo, LANE - NOPE_DIM - lo), (0, 0))))
        wuv_ext.append(jnp.pad(uvt[h], ((0, 0), (lo, LANE - V_DIM - lo))))
    return w_in_p, w_uq_r, jnp.stack(wuk_ext).astype(BF16), jnp.stack(wuv_ext).astype(BF16)


def kernel(x_prompt, x_sample, cache_swa_k, cache_swa_v, state_conv, cache_mla_ckv, cache_mla_kpe, page_table,
           rel_bias, ffn1_norm, ffn1_w_gate, ffn1_w_up, ffn1_w_down, mix_norm, ffn2_norm, ffn2_w_gate,
           ffn2_w_up, ffn2_w_down, even_w_in, even_w_out, attn_sinks, sgu_ln_g, sgu_ln_b, sgu_w, sgu_b,
           odd_w_in, odd_w_out, conv_w, mla_q_norm, mla_w_uq, mla_kv_norm, mla_w_uk, mla_w_uv, final_norm):
    nb, s, d = x_prompt.shape
    bd, ts, _ = x_sample.shape
    depth = ffn1_norm.shape[0]
    n_p, n_s = nb * s, bd * ts
    past = page_table.shape[1] * PAGE_SIZE
    wb = cache_swa_k.shape[2]
    assert n_s == TM and n_p % TM == 0 and s % TM == 0, "sample group must fill exactly one token tile"
    assert 8 % ts == 0 and bd % BB_EVEN == 0 and bd % 8 == 0 and wb == WINDOW
    assert s % KB_ODD == 0 and s % TQ_EVEN == 0 and KB_ODD % TQ_ODD == 0
    n_pt = n_p // TM

    def to_b_major(a):
        return a.reshape(ts, bd, -1).transpose(1, 0, 2)

    x = (x_prompt.reshape(n_p, d), x_sample.transpose(1, 0, 2).reshape(n_s, d))
    bias_p, bias_s = _bias_tables(rel_bias, ts)
    cos_np, sin_np = _rope_tables_np(s, past, ts, bd)
    cos_t, sin_t = jnp.asarray(cos_np), jnp.asarray(sin_np)

    w1 = tuple(w.astype(BF16) for w in (ffn1_w_gate, ffn1_w_up, ffn1_w_down))
    w2 = tuple(w.astype(BF16) for w in (ffn2_w_gate, ffn2_w_up, ffn2_w_down))
    w_out_even, w_out_odd = even_w_out.astype(BF16), odd_w_out.astype(BF16)
    pool_kpet = jnp.swapaxes(cache_mla_kpe, 2, 3)

    outs = {k: [] for k in ("kp", "vp", "ks", "vs", "sv", "cp", "cs", "ckp", "kpp", "cks", "kps")}
    for l in range(depth):
        x = _ffn(x, ffn1_norm[l], *w1, l, n_pt)
        if l % 2 == 0:
            e = l // 2
            q, kv, ug, kv_p = _inproj_even(x, mix_norm[l], even_w_in[e].astype(BF16), n_pt, s)
            mix_p = _mix_even_prompt(q, kv, ug, bias_p, attn_sinks[e], sgu_ln_g[e], sgu_ln_b[e], sgu_w[e],
                                     sgu_b[e], nb, s)
            q_b = to_b_major(q[n_p:].astype(F32)).reshape(n_s, A_Q_W)
            kvn_b = to_b_major(kv[n_p:]).reshape(n_s, 2 * A_KV_W)
            a_b, new_k, new_v = _mix_even_sample(q_b, kvn_b, cache_swa_k[e].reshape(bd, wb, A_KV_W),
                                                 cache_swa_v[e].reshape(bd, wb, A_KV_W), bias_s, attn_sinks[e], ts)
            a_t = a_b.reshape(bd, ts, A_Q_W).transpose(1, 0, 2).reshape(n_s, A_Q_W).astype(BF16)
            sg_s, v_rows = _sgu_sample(ug, sgu_ln_g[e], sgu_ln_b[e], sgu_w[e], sgu_b[e], n_p, ts, bd)
            mix_s = jnp.concatenate([a_t, sg_s], axis=1)
            w_out = (w_out_even, e)
            outs["kp"].append(kv_p[..., :A_KV_W].reshape(nb, wb, A_KV_HEADS, HEAD_DIM))
            outs["vp"].append(kv_p[..., A_KV_W:].reshape(nb, wb, A_KV_HEADS, HEAD_DIM))
            outs["ks"].append(new_k.reshape(bd, wb, A_KV_HEADS, HEAD_DIM))
            outs["vs"].append(new_v.reshape(bd, wb, A_KV_HEADS, HEAD_DIM))
            outs["sv"].append(to_b_major(v_rows))
        else:
            j = l // 2
            w_in_p, w_uq_r, wuk_ext, wuv_ext = _prep_odd_weights(odd_w_in[j], mla_w_uq[j], mla_w_uk[j], mla_w_uv[j])
            ci, qn, qp, kc, ckv_p, kpe_p, ckv_s, kpe_s = _inproj_odd(
                x, mix_norm[l], w_in_p, mla_q_norm[j], w_uq_r, mla_kv_norm[j], cos_t, sin_t, n_pt, s)
            mix_p, cst_p = _mix_odd_prompt(ci, qn, qp, kc, wuk_ext, wuv_ext, conv_w[j], nb, s)
            qx = _qx_sample(qn, qp, wuk_ext, n_p, n_s)
            qx_b = qx.reshape(MLA_HEADS * ts, bd, KC_W).transpose(1, 0, 2)
            kn_b = jnp.pad(to_b_major(kc[n_p:].astype(F32)), ((0, 0), (0, 8 - ts), (0, 0)))
            o_lat = _mla_sample(page_table, qx_b, kn_b, cache_mla_ckv, pool_kpet, j, ts)
            o_lat_t = o_lat.reshape(bd, MLA_HEADS, ts, KV_LORA).transpose(1, 2, 0, 3).reshape(
                MLA_HEADS, n_s, KV_LORA).astype(BF16)
            mix_s, cst_s = _odd_sample_tail(ci, state_conv[j].transpose(1, 0, 2), conv_w[j], o_lat_t, wuv_ext,
                                            n_p, ts, bd)
            w_out = (w_out_odd, j)
            outs["cp"].append(cst_p[:, 8 - (CONV_WIDTH - 1):])
            outs["cs"].append(cst_s.transpose(1, 0, 2))
            outs["ckp"].append(ckv_p.reshape(nb, s, KV_LORA))
            outs["kpp"].append(kpe_p.reshape(nb, s, ROPE_DIM))
            outs["cks"].append(to_b_major(ckv_s))
            outs["kps"].append(to_b_major(kpe_s))
        x = _ffn(x, ffn2_norm[l], *w2, l, n_pt, mix=(mix_p, mix_s) + w_out,
                 final_g=final_norm if l == depth - 1 else None)
    y_prompt = x[0].reshape(nb, s, d)
    y_sample = x[1].reshape(ts, bd, d).transpose(1, 0, 2)
    st = lambda k: jnp.stack(outs[k])
    return (y_prompt, y_sample, st("kp"), st("vp"), st("ks"), st("vs"), st("sv"), st("cp"), st("cs"),
            st("ckp"), st("kpp"), st("cks"), st("kps"))
```

```python
import functools
import math

import numpy as np
import jax
import jax.numpy as jnp
from jax import lax
from jax.experimental import pallas as pl
from jax.experimental.pallas import tpu as pltpu

F32 = jnp.float32
BF16 = jnp.bfloat16

HEAD_DIM = 64
A_HEADS = 8
A_KV_HEADS = 2
A_Q_W = A_HEADS * HEAD_DIM
A_KV_W = A_KV_HEADS * HEAD_DIM
WINDOW = 128
N_BUCKETS = 32
MAX_DISTANCE = 128
SG_GROUPS = 4
SG_GROUP_DIM = 128
SG_WIDTH = SG_GROUPS * SG_GROUP_DIM
CHUNK = 128
C_WIDTH = 512
CONV_WIDTH = 3
MLA_HEADS = 8
NOPE_DIM = 64
ROPE_DIM = 32
V_DIM = 64
Q_LORA = 768
KV_LORA = 256
ROPE_THETA = 10000.0
PAGE_SIZE = 128
EPS = 1e-6
NEG = -1e30

LANE = 128
KC_W = KV_LORA + LANE
TM = 512
FC = 512
TQ_EVEN = 512
TQ_ODD = 256
KB_ODD = 512
CHAINS_ODD = 2
BB_EVEN = 16
MLA_SAMPLE_CHUNK = 2048
MLA_SAMPLE_BATCHES = 2
MLA_SAMPLE_VMEM = 60 * 1024 * 1024
VMEM_LIMIT = 56 * 1024 * 1024


def _cp(sem, vmem=VMEM_LIMIT):
    return pltpu.CompilerParams(dimension_semantics=sem, vmem_limit_bytes=vmem)


def _dot(a, b):
    return jnp.dot(a, b, preferred_element_type=F32)


def _dot_nt(a, b):
    return lax.dot_general(a, b, (((1,), (1,)), ((), ())), preferred_element_type=F32)


def _rms(x, g):
    ms = jnp.mean(x * x, axis=-1, keepdims=True)
    return x * lax.rsqrt(ms + EPS) * g


def _gelu(x):
    return 0.5 * x * (1.0 + lax.erf(x * np.float32(math.sqrt(0.5))))


def _const_spec(shape):
    n = len(shape)
    return pl.BlockSpec(shape, lambda *_: (0,) * n, pipeline_mode=pl.Buffered(1))


def _smem_spec():
    return pl.BlockSpec(memory_space=pltpu.SMEM)


def _ffn_chunks(f):
    chunks = [(c, FC) for c in range(0, f - f % FC, FC)]
    if f % FC:
        chunks.append((f - f % FC, f % FC))
    return chunks


def _ffn_kernel(*refs, split_in, has_mix, has_final, n_prompt_tiles, chunks):
    it = iter(refs)
    x_ref = next(it)
    if split_in:
        xs_ref = next(it)
    if has_mix:
        mixp_ref, mixs_ref, wout_ref = next(it), next(it), next(it)
    g_ref, wg_ref, wu_ref, wd_ref = next(it), next(it), next(it), next(it)
    if has_final:
        fg_ref = next(it)
    o_ref = next(it)
    if has_final:
        os_ref = next(it)
    acc_ref = next(it)

    is_sample = pl.program_id(0) >= n_prompt_tiles
    x = x_ref[...]
    if split_in:
        x = jnp.where(is_sample, xs_ref[...], x)
    if has_mix:
        mix = jnp.where(is_sample, mixs_ref[...], mixp_ref[...])
        x = x + _dot(mix, wout_ref[...])
    h = _rms(x, g_ref[...]).astype(BF16)
    for k, (c0, cw) in enumerate(chunks):
        g = _dot(h, wg_ref[:, c0:c0 + cw])
        u = _dot(h, wu_ref[:, c0:c0 + cw])
        a = (g * jax.nn.sigmoid(g) * u).astype(BF16)
        d = _dot(a, wd_ref[c0:c0 + cw, :])
        if k == 0:
            acc_ref[...] = d
        else:
            acc_ref[...] += d
    y = x + 0.5 * acc_ref[...]
    if has_final:
        y = _rms(y, fg_ref[...])

        @pl.when(jnp.logical_not(is_sample))
        def _():
            o_ref[...] = y

        @pl.when(is_sample)
        def _():
            os_ref[...] = y
    else:
        o_ref[...] = y


def _layer_spec(shape, layer):
    n = len(shape)
    return pl.BlockSpec((None,) + tuple(shape), lambda *_: (layer,) + (0,) * n, pipeline_mode=pl.Buffered(1))


def _ffn(x, norm_g, w_gate, w_up, w_down, layer, n_prompt_tiles, mix=None, final_g=None):
    n_tiles = n_prompt_tiles + 1
    last_p = n_prompt_tiles - 1
    row = lambda i: (i, 0)
    prompt_row = lambda i: (jnp.minimum(i, last_p), 0)
    sample_row = lambda i: (0, 0)
    split_in = isinstance(x, tuple)
    d = x[0].shape[1] if split_in else x.shape[1]
    t = n_tiles * TM
    f = w_down.shape[1]
    if split_in:
        args, specs = list(x), [pl.BlockSpec((TM, d), prompt_row), pl.BlockSpec((TM, d), sample_row)]
    else:
        args, specs = [x], [pl.BlockSpec((TM, d), row)]
    if mix is not None:
        mix_p, mix_s, w_out, e = mix
        dm = w_out.shape[1]
        args += [mix_p, mix_s, w_out]
        specs += [pl.BlockSpec((TM, dm), prompt_row), pl.BlockSpec((TM, dm), sample_row),
                  _layer_spec((dm, d), e)]
    args += [norm_g.reshape(1, d), w_gate, w_up, w_down]
    specs += [_const_spec((1, d)), _layer_spec((d, f), layer), _layer_spec((d, f), layer),
              _layer_spec((f, d), layer)]
    if final_g is not None:
        args.append(final_g.reshape(1, d))
        specs.append(_const_spec((1, d)))
    body = functools.partial(_ffn_kernel, split_in=split_in, has_mix=mix is not None,
                             has_final=final_g is not None, n_prompt_tiles=n_prompt_tiles, chunks=_ffn_chunks(f))
    if final_g is not None:
        out_specs = [pl.BlockSpec((TM, d), prompt_row), pl.BlockSpec((TM, d), sample_row)]
        out_shape = [jax.ShapeDtypeStruct((n_prompt_tiles * TM, d), F32), jax.ShapeDtypeStruct((TM, d), F32)]
    else:
        out_specs = pl.BlockSpec((TM, d), row)
        out_shape = jax.ShapeDtypeStruct((t, d), F32)
    return pl.pallas_call(
        body,
        grid=(n_tiles,),
        in_specs=specs,
        out_specs=out_specs,
        out_shape=out_shape,
        scratch_shapes=[pltpu.VMEM((TM, d), F32)],
        compiler_params=_cp(("arbitrary",)),
        name="ffn",
    )(*args)


def _t5_bucket_np(dist):
    n = np.maximum(dist, 0)
    max_exact = N_BUCKETS // 2
    nf = np.maximum(n, 1).astype(np.float32)
    large = max_exact + (np.log(nf / np.float32(max_exact)) / np.float32(math.log(MAX_DISTANCE / max_exact))
                         * np.float32(N_BUCKETS - max_exact)).astype(np.int32)
    return np.where(n < max_exact, n, np.minimum(large, N_BUCKETS - 1)).astype(np.int32)


def _bias_tables_np(ts):
    q = np.arange(WINDOW)[:, None]
    j = np.arange(2 * WINDOW)[None, :]
    bucket_p = _t5_bucket_np(q - j + WINDOW)
    nb = 8 // ts
    cols = nb * WINDOW + 16
    bucket_s = -np.ones((8, cols), np.int32)
    for r in range(8):
        bi, t = divmod(r, ts)
        for c in range(cols):
            if c < nb * WINDOW:
                bj, jj = divmod(c, WINDOW)
                dist = t - (jj - WINDOW)
            elif c < nb * WINDOW + 8:
                bj, tt = divmod(c - nb * WINDOW, ts)
                dist = t - tt
            else:
                continue
            if bj == bi and 0 <= dist <= WINDOW:
                bucket_s[r, c] = _t5_bucket_np(np.array(dist))
    return bucket_p, bucket_s


def _bias_kernel(rb_ref, bp_ref, bs_ref, op_ref, os_ref):
    h = pl.program_id(0)
    bp = bp_ref[...]
    bs = bs_ref[...]
    accp = jnp.zeros(bp.shape, F32)
    accs = jnp.full(bs.shape, NEG, F32)
    for k in range(N_BUCKETS):
        val = rb_ref[k, h]
        accp = jnp.where(bp == k, val, accp)
        accs = jnp.where(bs == k, val, accs)
    op_ref[0] = accp
    os_ref[0] = accs


def _bias_tables(rel_bias, ts):
    bucket_p, bucket_s = _bias_tables_np(ts)
    return pl.pallas_call(
        _bias_kernel,
        grid=(A_HEADS,),
        in_specs=[_smem_spec(),
                  pl.BlockSpec(bucket_p.shape, lambda h: (0, 0)),
                  pl.BlockSpec(bucket_s.shape, lambda h: (0, 0))],
        out_specs=[pl.BlockSpec((1,) + bucket_p.shape, lambda h: (h, 0, 0)),
                   pl.BlockSpec((1,) + bucket_s.shape, lambda h: (h, 0, 0))],
        out_shape=[jax.ShapeDtypeStruct((A_HEADS,) + bucket_p.shape, F32),
                   jax.ShapeDtypeStruct((A_HEADS,) + bucket_s.shape, F32)],
        compiler_params=_cp(("arbitrary",)),
        name="t5_bias",
    )(rel_bias, jnp.asarray(bucket_p), jnp.asarray(bucket_s))


def _inproj_even_kernel(x_ref, g_ref, w_ref, q_ref, kv_ref, ug_ref, last_ref, *, tiles_per_seq, n_prompt_tiles):
    h = _rms(x_ref[...], g_ref[...]).astype(BF16)
    p = _dot(h, w_ref[...])
    q_ref[...] = (p[:, :A_Q_W] * np.float32(HEAD_DIM ** -0.5)).astype(BF16)
    kv = p[:, A_Q_W:A_Q_W + 2 * A_KV_W]
    kv_ref[...] = kv
    ug_ref[...] = p[:, A_Q_W + 2 * A_KV_W:]
    i = pl.program_id(0)

    @pl.when(((i + 1) % tiles_per_seq == 0) & (i < n_prompt_tiles))
    def _():
        last_ref[...] = kv[TM - WINDOW:]


def _inproj_even(x, norm_g, w_in, n_prompt_tiles, s):
    t, d = x.shape
    n = w_in.shape[1]
    row = lambda i: (i, 0)
    per_seq = s // TM
    nb = n_prompt_tiles // per_seq
    return pl.pallas_call(
        functools.partial(_inproj_even_kernel, tiles_per_seq=per_seq, n_prompt_tiles=n_prompt_tiles),
        grid=(t // TM,),
        in_specs=[pl.BlockSpec((TM, d), row), _const_spec((1, d)), _const_spec((d, n))],
        out_specs=[pl.BlockSpec((TM, A_Q_W), row), pl.BlockSpec((TM, 2 * A_KV_W), row),
                   pl.BlockSpec((TM, 2 * SG_WIDTH), row),
                   pl.BlockSpec((None, WINDOW, 2 * A_KV_W), lambda i: (jnp.minimum(i // per_seq, nb - 1), 0, 0))],
        out_shape=[jax.ShapeDtypeStruct((t, A_Q_W), BF16), jax.ShapeDtypeStruct((t, 2 * A_KV_W), F32),
                   jax.ShapeDtypeStruct((t, 2 * SG_WIDTH), F32),
                   jax.ShapeDtypeStruct((nb, WINDOW, 2 * A_KV_W), F32)],
        compiler_params=_cp(("arbitrary",)),
        name="inproj_even",
    )(x, norm_g.reshape(1, d), w_in)


def _head_variants(x):
    lo = lax.broadcasted_iota(jnp.int32, x.shape, 1) < HEAD_DIM
    xr = pltpu.roll(x, HEAD_DIM, 1)
    z = jnp.zeros_like(x)
    return [[jnp.where(lo, x, z).astype(BF16), jnp.where(lo, z, xr).astype(BF16)],
            [jnp.where(lo, xr, z).astype(BF16), jnp.where(lo, z, x).astype(BF16)]]


def _sink_softmax_pv(s, sink, v):
    mx = jnp.maximum(jnp.max(s, axis=-1, keepdims=True), sink)
    p = jnp.exp(s - mx)
    den = jnp.sum(p, axis=-1, keepdims=True) + jnp.exp(sink - mx)
    return _dot(p.astype(BF16), v) * (1.0 / den)


def _layer_norm(x, g, b):
    xc = x - jnp.mean(x, axis=-1, keepdims=True)
    y = xc * lax.rsqrt(jnp.mean(xc * xc, axis=-1, keepdims=True) + EPS)
    return y * g + b


def _even_prompt_kernel(q_ref, kv_ref, kvp_ref, ug_ref, bias_ref, sink_ref, lng_ref, lnb_ref, ws_ref,
                        bst_ref, o_ref, *, tq):
    nblk = tq // WINDOW
    step = pl.program_id(1)
    kv_all = jnp.concatenate([kvp_ref[...], kv_ref[...]], axis=0)
    kvar = _head_variants(kv_all[:, :A_KV_W])
    vvar = _head_variants(kv_all[:, A_KV_W:])
    qi = lax.broadcasted_iota(jnp.int32, (WINDOW, 2 * WINDOW), 0)
    kj = lax.broadcasted_iota(jnp.int32, (WINDOW, 2 * WINDOW), 1)
    dist = qi - kj + WINDOW
    band = (dist >= 0) & (dist <= WINDOW)
    cur = kj >= WINDOW
    ri = lax.broadcasted_iota(jnp.int32, (CHUNK, CHUNK), 0)
    ci = lax.broadcasted_iota(jnp.int32, (CHUNK, CHUNK), 1)
    wt = [jnp.where(ri >= ci, ws_ref[g], 0.0).astype(BF16) for g in range(SG_GROUPS)]
    lng, lnb = lng_ref[...], lnb_ref[...]
    for n in range(nblk):
        r0 = n * WINDOW
        has_prev = (step * nblk + n) > 0
        mask = band & (cur | has_prev)
        for m in range(A_HEADS // 2):
            kvh = (2 * m) // (A_HEADS // A_KV_HEADS)
            qp = q_ref[r0:r0 + WINDOW, LANE * m:LANE * (m + 1)]
            o = None
            for half in range(2):
                hd = 2 * m + half
                s = _dot_nt(qp, kvar[kvh][half][r0:r0 + 2 * WINDOW])
                s = jnp.where(mask, s + bias_ref[hd], NEG)
                oh = _sink_softmax_pv(s, sink_ref[hd], vvar[kvh][half][r0:r0 + 2 * WINDOW])
                o = oh if o is None else o + oh
            o_ref[r0:r0 + WINDOW, LANE * m:LANE * (m + 1)] = o.astype(BF16)
        u = _gelu(ug_ref[r0:r0 + WINDOW, :SG_WIDTH])
        v = _layer_norm(_gelu(ug_ref[r0:r0 + WINDOW, SG_WIDTH:]), lng, lnb)
        for g in range(SG_GROUPS):
            sl = slice(SG_GROUP_DIM * g, SG_GROUP_DIM * (g + 1))
            y = _dot(wt[g], v[:, sl].astype(BF16)) + bst_ref[:, g:g + 1]
            o_ref[r0:r0 + WINDOW, A_Q_W + SG_GROUP_DIM * g:A_Q_W + SG_GROUP_DIM * (g + 1)] = (
                u[:, sl] * y).astype(BF16)


def _mix_even_prompt(q, kv, ug, bias_p, sinks, ln_g, ln_b, w_s, b_s, nb, s):
    tq = TQ_EVEN
    nq = s // tq
    blk = lambda b, i: (b * nq + i, 0)
    prev = lambda b, i: (jnp.maximum((b * nq + i) * (tq // WINDOW) - 1, 0), 0)
    return pl.pallas_call(
        functools.partial(_even_prompt_kernel, tq=tq),
        grid=(nb, nq),
        in_specs=[pl.BlockSpec((tq, A_Q_W), blk), pl.BlockSpec((tq, 2 * A_KV_W), blk),
                  pl.BlockSpec((WINDOW, 2 * A_KV_W), prev), pl.BlockSpec((tq, 2 * SG_WIDTH), blk),
                  _const_spec(bias_p.shape), _smem_spec(), _const_spec((1, SG_WIDTH)),
                  _const_spec((1, SG_WIDTH)), _const_spec(w_s.shape), _const_spec((CHUNK, SG_GROUPS))],
        out_specs=pl.BlockSpec((tq, A_Q_W + SG_WIDTH), blk),
        out_shape=jax.ShapeDtypeStruct((nb * s, A_Q_W + SG_WIDTH), BF16),
        compiler_params=_cp(("arbitrary", "arbitrary")),
        name="mix_even_prompt",
    )(q, kv, kv, ug, bias_p, sinks, ln_g.reshape(1, -1), ln_b.reshape(1, -1), w_s, b_s.T)


def _even_sample_kernel(q_ref, kvn_ref, ck_ref, cv_ref, bias_ref, sink_ref, a_ref, nk_ref, nv_ref, *, ts):
    bb = ck_ref.shape[0]
    gpb = 8 // ts
    wb = ck_ref.shape[1]
    n_cache = gpb * wb
    hpk = A_HEADS // A_KV_HEADS
    zpad = jnp.zeros((8, A_KV_W), F32)

    def lo_half(x, kvh):
        lo = lax.broadcasted_iota(jnp.int32, x.shape, 1) < HEAD_DIM
        return jnp.where(lo, x if kvh == 0 else pltpu.roll(x, HEAD_DIM, 1), 0.0).astype(BF16)

    hrow = lax.broadcasted_iota(jnp.int32, (8 * hpk, 1), 0) // 8
    bias_c, bias_n, sink = [], [], []
    for kvh in range(A_KV_HEADS):
        heads = [hpk * kvh + i for i in range(hpk)]
        bias_c.append(jnp.concatenate([bias_ref[h][:, :n_cache] for h in heads], axis=0))
        bias_n.append(jnp.concatenate([bias_ref[h][:, n_cache:] for h in heads], axis=0))
        sk = jnp.full((8 * hpk, 1), sink_ref[heads[0]], F32)
        for i in range(1, hpk):
            sk = jnp.where(hrow == i, sink_ref[heads[i]], sk)
        sink.append(sk)
    for g in range(bb // gpb):
        r0 = 8 * g
        qg = q_ref[r0:r0 + 8, :]
        kn = kvn_ref[r0:r0 + 8, :A_KV_W]
        vn = kvn_ref[r0:r0 + 8, A_KV_W:]
        kc = jnp.concatenate([ck_ref[gpb * g + i] for i in range(gpb)], axis=0)
        vc = jnp.concatenate([cv_ref[gpb * g + i] for i in range(gpb)], axis=0)
        knp = jnp.concatenate([kn, zpad], axis=0)
        vnp = jnp.concatenate([vn, zpad], axis=0)
        for kvh in range(A_KV_HEADS):
            c0 = 2 * LANE * kvh
            qs = jnp.concatenate([lo_half(qg[:, c0 + LANE * (i // 2):c0 + LANE * (i // 2 + 1)], i % 2)
                                  for i in range(hpk)], axis=0)
            sc = _dot_nt(qs, lo_half(kc, kvh))
            sn = _dot_nt(qs, lo_half(knp, kvh))
            sc = jnp.where(bias_c[kvh] > 0.5 * NEG, sc + bias_c[kvh], NEG)
            sn = jnp.where(bias_n[kvh] > 0.5 * NEG, sn + bias_n[kvh], NEG)
            mx = jnp.maximum(jnp.maximum(jnp.max(sc, axis=-1, keepdims=True), jnp.max(sn, axis=-1, keepdims=True)),
                             sink[kvh])
            pc = jnp.exp(sc - mx)
            pn = jnp.exp(sn - mx)
            den = (jnp.sum(pc, axis=-1, keepdims=True) + jnp.sum(pn, axis=-1, keepdims=True)
                   + jnp.exp(sink[kvh] - mx))
            o = (_dot(pc.astype(BF16), lo_half(vc, kvh)) + _dot(pn.astype(BF16), lo_half(vnp, kvh))) * (1.0 / den)
            for i in range(hpk // 2):
                pair = o[16 * i:16 * i + 8] + pltpu.roll(o[16 * i + 8:16 * i + 16], HEAD_DIM, 1)
                a_ref[r0:r0 + 8, c0 + LANE * i:c0 + LANE * (i + 1)] = pair
        for i in range(gpb):
            b = gpb * g + i
            nk_ref[b, 0:wb - ts, :] = ck_ref[b, ts:wb, :]
            nk_ref[b, wb - ts:wb, :] = kn[ts * i:ts * (i + 1)]
            nv_ref[b, 0:wb - ts, :] = cv_ref[b, ts:wb, :]
            nv_ref[b, wb - ts:wb, :] = vn[ts * i:ts * (i + 1)]


def _mix_even_sample(q_b, kvn_b, cache_k, cache_v, bias_s, sinks, ts):
    bd, wb, _ = cache_k.shape
    bb = BB_EVEN
    rows = bb * ts
    blk2 = lambda i: (i, 0)
    blk3 = lambda i: (i, 0, 0)
    return pl.pallas_call(
        functools.partial(_even_sample_kernel, ts=ts),
        grid=(bd // bb,),
        in_specs=[pl.BlockSpec((rows, A_Q_W), blk2), pl.BlockSpec((rows, 2 * A_KV_W), blk2),
                  pl.BlockSpec((bb, wb, A_KV_W), blk3), pl.BlockSpec((bb, wb, A_KV_W), blk3),
                  _const_spec(bias_s.shape), _smem_spec()],
        out_specs=[pl.BlockSpec((rows, A_Q_W), blk2), pl.BlockSpec((bb, wb, A_KV_W), blk3),
                   pl.BlockSpec((bb, wb, A_KV_W), blk3)],
        out_shape=[jax.ShapeDtypeStruct((bd * ts, A_Q_W), F32),
                   jax.ShapeDtypeStruct(cache_k.shape, F32), jax.ShapeDtypeStruct(cache_v.shape, F32)],
        compiler_params=_cp(("arbitrary",)),
        name="mix_even_sample",
    )(q_b, kvn_b, cache_k, cache_v, bias_s, sinks)


def _token_col(rows, ts):
    return lax.broadcasted_iota(jnp.int32, (rows, 1), 0) % ts


def _sgu_sample_kernel(ug_ref, lng_ref, lnb_ref, w_ref, b_ref, sg_ref, v_ref, *, ts):
    u = _gelu(ug_ref[:, :SG_WIDTH])
    v = _layer_norm(_gelu(ug_ref[:, SG_WIDTH:]), lng_ref[...], lnb_ref[...])
    v_ref[...] = v
    rows = v.shape[0]
    t = _token_col(rows, ts)
    back = [v] + [pltpu.roll(v, k, 0) for k in range(1, ts)]
    for g in range(SG_GROUPS):
        sl = slice(SG_GROUP_DIM * g, SG_GROUP_DIM * (g + 1))
        y = jnp.zeros((rows, 1), F32)
        for i in range(ts):
            y = jnp.where(t == i, b_ref[g, i], y)
        for k in range(ts):
            wk = jnp.zeros((rows, 1), F32)
            for i in range(k, ts):
                wk = jnp.where(t == i, w_ref[g, i * ts + i - k], wk)
            y = y + wk * back[k][:, sl]
        sg_ref[:, sl] = (u[:, sl] * y).astype(BF16)


def _sgu_sample(ug, ln_g, ln_b, w_s, b_s, n_prompt_rows, ts, bd):
    rows = ts * bd
    off = n_prompt_rows // rows
    w_small = w_s[:, :ts, :ts].reshape(SG_GROUPS, ts * ts)
    b_small = b_s[:, :ts]
    return pl.pallas_call(
        functools.partial(_sgu_sample_kernel, ts=ts),
        grid=(1,),
        in_specs=[pl.BlockSpec((rows, 2 * SG_WIDTH), lambda i: (off, 0)), _const_spec((1, SG_WIDTH)),
                  _const_spec((1, SG_WIDTH)), _smem_spec(), _smem_spec()],
        out_specs=[pl.BlockSpec((rows, SG_WIDTH), lambda i: (0, 0)),
                   pl.BlockSpec((rows, SG_WIDTH), lambda i: (0, 0))],
        out_shape=[jax.ShapeDtypeStruct((rows, SG_WIDTH), BF16), jax.ShapeDtypeStruct((rows, SG_WIDTH), F32)],
        compiler_params=_cp(("arbitrary",)),
        name="sgu_sample",
    )(ug, ln_g.reshape(1, -1), ln_b.reshape(1, -1), w_small, b_small)


def _rope_tables_np(s, past, ts, bd):
    half = ROPE_DIM // 2
    inv = np.exp(-math.log(ROPE_THETA) * np.arange(half, dtype=np.float32) / np.float32(half)).astype(np.float32)
    pos = np.concatenate([np.arange(s), past + np.tile(np.arange(ts), bd)]).astype(np.float32)
    ang = (pos[:, None] * inv[None, :]).astype(np.float32).astype(np.float64)
    cos = np.cos(ang)
    sin = np.sin(ang)
    cos_t = np.tile(np.concatenate([cos, cos], axis=1), (1, MLA_HEADS)).astype(np.float32)
    sin_t = np.tile(np.concatenate([-sin, sin], axis=1), (1, MLA_HEADS)).astype(np.float32)
    return cos_t, sin_t


def _rope(x, cos_t, sin_t):
    n = x.shape[1]
    half = ROPE_DIM // 2
    first = (lax.broadcasted_iota(jnp.int32, x.shape, 1) % ROPE_DIM) < half
    partner = jnp.where(first, pltpu.roll(x, n - half, 1), pltpu.roll(x, half, 1))
    return x * cos_t + partner * sin_t


def _inproj_odd_kernel(x_ref, g_ref, w_ref, qg_ref, wuq_ref, kvg_ref, cos_ref, sin_ref,
                       ci_ref, qn_ref, qp_ref, kc_ref, ckvp_ref, kpep_ref, ckvs_ref, kpes_ref, *, n_prompt_tiles):
    h = _rms(x_ref[...], g_ref[...]).astype(BF16)
    p = _dot(h, w_ref[...])
    o_q = 3 * C_WIDTH
    o_kv = o_q + Q_LORA
    o_pe = o_kv + KV_LORA
    ci_ref[...] = p[:, :o_q].astype(BF16)
    scale = np.float32((NOPE_DIM + ROPE_DIM) ** -0.5 * math.log2(math.e))
    q = _dot(_rms(p[:, o_q:o_kv], qg_ref[...]).astype(BF16), wuq_ref[...])
    n_nope = MLA_HEADS * NOPE_DIM
    cos_t, sin_t = cos_ref[...], sin_ref[...]
    qn_ref[...] = (q[:, :n_nope] * scale).astype(BF16)
    qp_ref[...] = (_rope(q[:, n_nope:], cos_t, sin_t) * scale).astype(BF16)
    ckv = _rms(p[:, o_kv:o_pe], kvg_ref[...])
    kpe = _rope(p[:, o_pe:], cos_t[:, :LANE], sin_t[:, :LANE])
    kc_ref[:, :KV_LORA] = ckv.astype(BF16)
    kc_ref[:, KV_LORA:] = kpe.astype(BF16)
    is_sample = pl.program_id(0) >= n_prompt_tiles

    @pl.when(jnp.logical_not(is_sample))
    def _():
        ckvp_ref[...] = ckv
        kpep_ref[...] = kpe[:, :ROPE_DIM]

    @pl.when(is_sample)
    def _():
        ckvs_ref[...] = ckv
        kpes_ref[...] = kpe[:, :ROPE_DIM]


def _inproj_odd(x, norm_g, w_in, q_norm, w_uq, kv_norm, cos_t, sin_t, n_prompt_tiles, s):
    t, d = x.shape
    n = w_in.shape[1]
    row = lambda i: (i, 0)
    last_p = n_prompt_tiles - 1
    prompt_row = lambda i: (jnp.minimum(i, last_p), 0)
    sample_row = lambda i: (0, 0)
    per_seq = s // TM
    tab = lambda i: (jnp.where(i < n_prompt_tiles, i % per_seq, per_seq), 0)
    n_pe = MLA_HEADS * ROPE_DIM
    n_p = n_prompt_tiles * TM
    widths = [(3 * C_WIDTH, BF16), (MLA_HEADS * NOPE_DIM, BF16), (n_pe, BF16), (KC_W, BF16)]
    return pl.pallas_call(
        functools.partial(_inproj_odd_kernel, n_prompt_tiles=n_prompt_tiles),
        grid=(t // TM,),
        in_specs=[pl.BlockSpec((TM, d), row), _const_spec((1, d)), _const_spec((d, n)),
                  _const_spec((1, Q_LORA)), _const_spec(w_uq.shape), _const_spec((1, KV_LORA)),
                  pl.BlockSpec((TM, n_pe), tab), pl.BlockSpec((TM, n_pe), tab)],
        out_specs=[pl.BlockSpec((TM, w), row) for w, _ in widths] + [
            pl.BlockSpec((TM, KV_LORA), prompt_row), pl.BlockSpec((TM, ROPE_DIM), prompt_row),
            pl.BlockSpec((TM, KV_LORA), sample_row), pl.BlockSpec((TM, ROPE_DIM), sample_row)],
        out_shape=[jax.ShapeDtypeStruct((t, w), dt) for w, dt in widths] + [
            jax.ShapeDtypeStruct((n_p, KV_LORA), F32), jax.ShapeDtypeStruct((n_p, ROPE_DIM), F32),
            jax.ShapeDtypeStruct((TM, KV_LORA), F32), jax.ShapeDtypeStruct((TM, ROPE_DIM), F32)],
        compiler_params=_cp(("arbitrary",)),
        name="inproj_odd",
    )(x, norm_g.reshape(1, d), w_in, q_norm.reshape(1, -1), w_uq, kv_norm.reshape(1, -1), cos_t, sin_t)


def _build_qx(qn_ref, qp_ref, wuk_ref, qx_ref, rows):
    qp = qp_ref[...].astype(F32)
    keep = lax.broadcasted_iota(jnp.int32, (rows, LANE), 1) < ROPE_DIM
    n_pe = MLA_HEADS * ROPE_DIM
    for h in range(MLA_HEADS):
        pair = qn_ref[:, LANE * (h // 2):LANE * (h // 2 + 1)]
        qx_ref[h * rows:(h + 1) * rows, :KV_LORA] = _dot(pair, wuk_ref[h]).astype(qx_ref.dtype)
        pe = qp if h == 0 else pltpu.roll(qp, n_pe - ROPE_DIM * h, 1)
        qx_ref[h * rows:(h + 1) * rows, KV_LORA:] = jnp.where(keep, pe[:, :LANE], 0.0).astype(qx_ref.dtype)


def _odd_prompt_kernel(ci_ref, cih_ref, qn_ref, qp_ref, kc_ref, wuk_ref, wuv_ref, cw_ref,
                       o_ref, cst_ref, qx_ref, m_ref, l_ref, acc_ref, z_ref, *, tq, kb):
    step = pl.program_id(1)
    halo = 16
    zh = cih_ref[:, C_WIDTH:2 * C_WIDTH].astype(F32) * cih_ref[:, 2 * C_WIDTH:].astype(F32)
    z_ref[0:halo, :] = jnp.where(step > 0, zh, 0.0)
    z = ci_ref[:, C_WIDTH:2 * C_WIDTH].astype(F32) * ci_ref[:, 2 * C_WIDTH:].astype(F32)
    z_ref[halo:halo + tq, :] = z
    cw = cw_ref[...]
    y = cw[0:1] * z_ref[halo - 2:halo - 2 + tq, :] + cw[1:2] * z_ref[halo - 1:halo - 1 + tq, :] + cw[2:3] * z
    o_ref[:, :C_WIDTH] = (ci_ref[:, :C_WIDTH].astype(F32) * y).astype(BF16)
    cst_ref[0] = z[tq - 8:tq]

    _build_qx(qn_ref, qp_ref, wuk_ref, qx_ref, tq)
    rows = MLA_HEADS * tq
    n_full = (step * tq) // kb
    hpc = MLA_HEADS // CHAINS_ODD

    def kv_step(j, width, mask_from, first):
        k0 = j * kb if isinstance(j, int) else pl.multiple_of(j * kb, kb)
        kc = kc_ref[pl.ds(k0, width), :]
        n_grp = width // LANE
        if mask_from < n_grp:
            qpos = step * tq + lax.broadcasted_iota(jnp.int32, (tq, LANE), 0)
            kpos = k0 + lax.broadcasted_iota(jnp.int32, (tq, LANE), 1)
            visible = {g: kpos + LANE * g <= qpos for g in range(mask_from, n_grp)}
        for c in range(CHAINS_ODD):
            rc = slice(c * hpc * tq, (c + 1) * hpc * tq)
            s_all = _dot_nt(qx_ref[rc, :], kc)
            ps, alphas = [], []
            for hh in range(hpc):
                r = slice((c * hpc + hh) * tq, (c * hpc + hh + 1) * tq)
                sg = [s_all[hh * tq:(hh + 1) * tq, LANE * g:LANE * (g + 1)] for g in range(n_grp)]
                sg = [jnp.where(visible[g], x, NEG) if g >= mask_from else x for g, x in enumerate(sg)]
                m_blk = jnp.max(functools.reduce(jnp.maximum, sg), axis=-1, keepdims=True)
                if first:
                    m_new = jnp.broadcast_to(m_blk, (tq, LANE))
                else:
                    m_old = m_ref[r, :]
                    m_new = jnp.maximum(m_old, m_blk)
                    alpha = jnp.exp2(m_old - m_new)
                    alphas.append(jnp.concatenate([alpha] * (KV_LORA // LANE), axis=1))
                pg = [jnp.exp2(x - m_new) for x in sg]
                psum = functools.reduce(jnp.add, pg)
                l_ref[r, :] = psum if first else alpha * l_ref[r, :] + psum
                m_ref[r, :] = m_new
                ps.append(jnp.concatenate([x.astype(BF16) for x in pg], axis=1))
            pv = _dot(jnp.concatenate(ps, axis=0), kc[:, :KV_LORA])
            acc_ref[rc, :] = pv if first else jnp.concatenate(alphas, axis=0) * acc_ref[rc, :] + pv

    @pl.when(n_full >= 2)
    def _():
        kv_step(0, kb, kb // LANE, True)

    def full_step(j, carry):
        kv_step(j, kb, kb // LANE, False)
        return carry

    lax.fori_loop(1, n_full - 1, full_step, 0)
    for i, width in enumerate(range(tq, kb + 1, tq)):
        on_diag = step % (kb // tq) == i

        @pl.when(on_diag & (n_full == 0))
        def _(width=width):
            kv_step(0, width, 0, True)

        @pl.when(on_diag & (n_full == 1))
        def _(width=width):
            kv_step(0, kb + width, kb // LANE, True)

        @pl.when(on_diag & (n_full >= 2))
        def _(width=width):
            kv_step(n_full - 1, kb + width, kb // LANE, False)

    for m in range(MLA_HEADS // 2):
        o = None
        for half in range(2):
            h = 2 * m + half
            inv_l = jnp.broadcast_to(1.0 / jnp.sum(l_ref[h * tq:(h + 1) * tq, :], axis=-1, keepdims=True), (tq, LANE))
            ol = (acc_ref[h * tq:(h + 1) * tq, :] * jnp.concatenate([inv_l] * (KV_LORA // LANE), axis=1)).astype(BF16)
            oh = _dot(ol, wuv_ref[h])
            o = oh if o is None else o + oh
        o_ref[:, C_WIDTH + LANE * m:C_WIDTH + LANE * (m + 1)] = o.astype(BF16)


def _mix_odd_prompt(ci, qn, qp, kc, wuk_ext, wuv_ext, conv_w, nb, s):
    tq, kb = TQ_ODD, KB_ODD
    nq = s // tq
    blk = lambda b, i: (b * nq + i, 0)
    halo = lambda b, i: (jnp.maximum((b * nq + i) * (tq // 16) - 1, 0), 0)
    rows = MLA_HEADS * tq
    n_out = C_WIDTH + MLA_HEADS * V_DIM
    return pl.pallas_call(
        functools.partial(_odd_prompt_kernel, tq=tq, kb=kb),
        grid=(nb, nq),
        in_specs=[pl.BlockSpec((tq, 3 * C_WIDTH), blk), pl.BlockSpec((16, 3 * C_WIDTH), halo),
                  pl.BlockSpec((tq, MLA_HEADS * NOPE_DIM), blk), pl.BlockSpec((tq, MLA_HEADS * ROPE_DIM), blk),
                  pl.BlockSpec((s, KC_W), lambda b, i: (b, 0)),
                  _const_spec(wuk_ext.shape), _const_spec(wuv_ext.shape), _const_spec(conv_w.shape)],
        out_specs=[pl.BlockSpec((tq, n_out), blk), pl.BlockSpec((1, 8, C_WIDTH), lambda b, i: (b, 0, 0))],
        out_shape=[jax.ShapeDtypeStruct((nb * s, n_out), BF16), jax.ShapeDtypeStruct((nb, 8, C_WIDTH), F32)],
        scratch_shapes=[pltpu.VMEM((rows, KC_W), BF16),
                        pltpu.VMEM((rows, LANE), F32), pltpu.VMEM((rows, LANE), F32), pltpu.VMEM((rows, KV_LORA), F32),
                        pltpu.VMEM((tq + 16, C_WIDTH), F32)],
        compiler_params=_cp(("arbitrary", "arbitrary")),
        name="mix_odd_prompt",
    )(ci, ci, qn, qp, kc, wuk_ext, wuv_ext, conv_w)


def _qx_sample_kernel(qn_ref, qp_ref, wuk_ref, qx_ref, *, rows):
    _build_qx(qn_ref, qp_ref, wuk_ref, qx_ref, rows)


def _qx_sample(qn, qp, wuk_ext, n_prompt_rows, rows):
    off = n_prompt_rows // rows
    return pl.pallas_call(
        functools.partial(_qx_sample_kernel, rows=rows),
        grid=(1,),
        in_specs=[pl.BlockSpec((rows, MLA_HEADS * NOPE_DIM), lambda i: (off, 0)),
                  pl.BlockSpec((rows, MLA_HEADS * ROPE_DIM), lambda i: (off, 0)), _const_spec(wuk_ext.shape)],
        out_specs=pl.BlockSpec((MLA_HEADS * rows, KC_W), lambda i: (0, 0)),
        out_shape=jax.ShapeDtypeStruct((MLA_HEADS * rows, KC_W), F32),
        compiler_params=_cp(("arbitrary",)),
        name="qx_sample",
    )(qn, qp, wuk_ext)


def _page_copies(pt_ref, pool_ckv, pool_kpet, ckv_buf, kpe_buf, sem, layer, n_pages, step, slot):
    copies = []
    for i in range(MLA_SAMPLE_BATCHES):
        b = step * MLA_SAMPLE_BATCHES + i
        for p in range(n_pages):
            page = pt_ref[b, p]
            keys = pl.ds(p * PAGE_SIZE, PAGE_SIZE)
            copies.append(pltpu.make_async_copy(pool_ckv.at[layer, page], ckv_buf.at[slot, i, keys],
                                                sem.at[0, slot]))
            copies.append(pltpu.make_async_copy(pool_kpet.at[layer, page], kpe_buf.at[slot, i, :, keys],
                                                sem.at[1, slot]))
    return copies


def _mla_sample_kernel(pt_ref, qx_ref, kn_ref, pool_ckv, pool_kpet, o_ref, ckv_buf, kpe_buf, q_scr, sem, *,
                       ts, n_pages, layer, chunk):
    step = pl.program_id(0)
    n_steps = pl.num_programs(0)
    slot = step % 2
    copies = functools.partial(_page_copies, pt_ref, pool_ckv, pool_kpet, ckv_buf, kpe_buf, sem, layer, n_pages)

    @pl.when(step == 0)
    def _():
        for c in copies(0, 0):
            c.start()

    @pl.when(step + 1 < n_steps)
    def _():
        for c in copies(step + 1, 1 - slot):
            c.start()

    for c in copies(step, slot):
        c.wait()

    rows = MLA_HEADS * ts
    n_new = kn_ref.shape[0]
    t_q = lax.broadcasted_iota(jnp.int32, (rows, n_new), 0) % ts
    col = lax.broadcasted_iota(jnp.int32, (rows, n_new), 1)
    kn = kn_ref[...].astype(BF16)
    past = n_pages * PAGE_SIZE
    starts = range(0, past, chunk)
    for i in range(MLA_SAMPLE_BATCHES):
        for h in range(MLA_HEADS):
            q_scr[i, ts * h:ts * (h + 1), :] = qx_ref[h, ts * i:ts * (i + 1), :]
        qx = q_scr[i].astype(BF16)
        q_lat = qx[:, :KV_LORA]
        q_pe = qx[:, KV_LORA:KV_LORA + ROPE_DIM]
        sn = jnp.where((col // ts == i) & (col % ts <= t_q), _dot_nt(qx, kn), NEG)
        kcs = [ckv_buf[slot, i, k0:k0 + chunk, :].astype(BF16) for k0 in starts]
        ss = [_dot_nt(q_lat, kc) + _dot(q_pe, kpe_buf[slot, i, :, k0:k0 + chunk].astype(BF16))
              for k0, kc in zip(starts, kcs)]
        m = functools.reduce(jnp.maximum, [jnp.max(s, axis=-1, keepdims=True) for s in ss],
                             jnp.max(sn, axis=-1, keepdims=True))
        pn = jnp.exp2(sn - m)
        den = jnp.sum(pn, axis=-1, keepdims=True)
        pnb = pn.astype(BF16)
        accs = [_dot(pnb, kn[:, LANE * g:LANE * (g + 1)]) for g in range(KV_LORA // LANE)]
        for s, kc in zip(ss, kcs):
            p = jnp.exp2(s - m)
            den = den + jnp.sum(p, axis=-1, keepdims=True)
            pb = p.astype(BF16)
            accs = [a + _dot(pb, kc[:, LANE * g:LANE * (g + 1)]) for g, a in enumerate(accs)]
        o = jnp.concatenate(accs, axis=1) * (1.0 / den)
        for h in range(MLA_HEADS):
            o_ref[h, ts * i:ts * (i + 1), :] = o[ts * h:ts * (h + 1)]


def _mla_sample(page_table, qx, kn, pool_ckv, pool_kpet, layer, ts):
    bd, n_pages = page_table.shape
    past = n_pages * PAGE_SIZE
    nbs = MLA_SAMPLE_BATCHES
    step_rows = nbs * ts
    assert step_rows % 8 == 0
    grid_spec = pltpu.PrefetchScalarGridSpec(
        num_scalar_prefetch=1,
        grid=(bd // nbs,),
        in_specs=[pl.BlockSpec((MLA_HEADS, step_rows, KC_W), lambda g, pt: (0, g, 0)),
                  pl.BlockSpec((step_rows, KC_W), lambda g, pt: (g, 0)),
                  pl.BlockSpec(memory_space=pl.ANY), pl.BlockSpec(memory_space=pl.ANY)],
        out_specs=pl.BlockSpec((MLA_HEADS, step_rows, KV_LORA), lambda g, pt: (0, g, 0)),
        scratch_shapes=[pltpu.VMEM((2, nbs, past, KV_LORA), F32), pltpu.VMEM((2, nbs, ROPE_DIM, past), F32),
                        pltpu.VMEM((nbs, MLA_HEADS * ts, KC_W), F32), pltpu.SemaphoreType.DMA((2, 2))],
    )
    return pl.pallas_call(
        functools.partial(_mla_sample_kernel, ts=ts, n_pages=n_pages, layer=layer,
                          chunk=min(MLA_SAMPLE_CHUNK, past)),
        grid_spec=grid_spec,
        out_shape=jax.ShapeDtypeStruct((MLA_HEADS, bd * ts, KV_LORA), F32),
        compiler_params=_cp(("arbitrary",), vmem=MLA_SAMPLE_VMEM),
        name="mla_sample",
    )(page_table, qx, kn, pool_ckv, pool_kpet)


def _odd_sample_tail_kernel(ci_ref, st_ref, cw_ref, ol_ref, wuv_ref, o_ref, z_ref, *, ts):
    cw = cw_ref[...]
    z = ci_ref[:, C_WIDTH:2 * C_WIDTH].astype(F32) * ci_ref[:, 2 * C_WIDTH:].astype(F32)
    z_ref[...] = z
    rows = z.shape[0]
    t = _token_col(rows, ts)
    st = st_ref[...]
    back2 = jnp.where(t < 2, st, pltpu.roll(z, 2, 0))
    back1 = jnp.where(t == 0, pltpu.roll(st, rows - 1, 0), pltpu.roll(z, 1, 0))
    y = cw[0:1] * back2 + cw[1:2] * back1 + cw[2:3] * z
    o_ref[:, :C_WIDTH] = (ci_ref[:, :C_WIDTH].astype(F32) * y).astype(BF16)
    for m in range(MLA_HEADS // 2):
        o = (_dot(ol_ref[2 * m].astype(BF16), wuv_ref[2 * m])
             + _dot(ol_ref[2 * m + 1].astype(BF16), wuv_ref[2 * m + 1]))
        o_ref[:, C_WIDTH + LANE * m:C_WIDTH + LANE * (m + 1)] = o.astype(BF16)


def _odd_sample_tail(ci, state_rows, conv_w, o_lat, wuv_ext, n_prompt_rows, ts):
    rows = state_rows.shape[0]
    off = n_prompt_rows // rows
    n_out = C_WIDTH + MLA_HEADS * V_DIM
    return pl.pallas_call(
        functools.partial(_odd_sample_tail_kernel, ts=ts),
        grid=(1,),
        in_specs=[pl.BlockSpec((rows, 3 * C_WIDTH), lambda i: (off, 0)), _const_spec(state_rows.shape),
                  _const_spec(conv_w.shape), _const_spec(o_lat.shape), _const_spec(wuv_ext.shape)],
        out_specs=[pl.BlockSpec((rows, n_out), lambda i: (0, 0)), pl.BlockSpec((rows, C_WIDTH), lambda i: (0, 0))],
        out_shape=[jax.ShapeDtypeStruct((rows, n_out), BF16), jax.ShapeDtypeStruct((rows, C_WIDTH), F32)],
        compiler_params=_cp(("arbitrary",)),
        name="odd_sample_tail",
    )(ci, state_rows, conv_w, o_lat, wuv_ext)


def _prep_odd_weights(w_in, w_uq, w_uk, w_uv):
    d, n = w_in.shape
    n_pad = -n % LANE
    w_in_p = jnp.pad(w_in, ((0, 0), (0, n_pad))).astype(BF16)
    per = NOPE_DIM + ROPE_DIM
    wq = w_uq.reshape(Q_LORA, MLA_HEADS, per)
    w_uq_r = jnp.concatenate([wq[:, :, :NOPE_DIM].reshape(Q_LORA, -1), wq[:, :, NOPE_DIM:].reshape(Q_LORA, -1)],
                             axis=1).astype(BF16)
    ukt = jnp.transpose(w_uk, (1, 2, 0))
    uvt = jnp.transpose(w_uv, (1, 0, 2))
    wuk_ext, wuv_ext = [], []
    for h in range(MLA_HEADS):
        lo = (h % 2) * NOPE_DIM
        wuk_ext.append(jnp.pad(ukt[h], ((lo, LANE - NOPE_DIM - lo), (0, 0))))
        wuv_ext.append(jnp.pad(uvt[h], ((0, 0), (lo, LANE - V_DIM - lo))))
    return w_in_p, w_uq_r, jnp.stack(wuk_ext).astype(BF16), jnp.stack(wuv_ext).astype(BF16)


def kernel(x_prompt, x_sample, cache_swa_k, cache_swa_v, state_conv, cache_mla_ckv, cache_mla_kpe, page_table,
           rel_bias, ffn1_norm, ffn1_w_gate, ffn1_w_up, ffn1_w_down, mix_norm, ffn2_norm, ffn2_w_gate,
           ffn2_w_up, ffn2_w_down, even_w_in, even_w_out, attn_sinks, sgu_ln_g, sgu_ln_b, sgu_w, sgu_b,
           odd_w_in, odd_w_out, conv_w, mla_q_norm, mla_w_uq, mla_kv_norm, mla_w_uk, mla_w_uv, final_norm):
    nb, s, d = x_prompt.shape
    bd, ts, _ = x_sample.shape
    depth = ffn1_norm.shape[0]
    n_p, n_s = nb * s, bd * ts
    past = page_table.shape[1] * PAGE_SIZE
    wb = cache_swa_k.shape[2]
    assert n_s == TM and n_p % TM == 0 and s % TM == 0, "sample group must fill exactly one token tile"
    assert 8 % ts == 0 and bd % BB_EVEN == 0 and bd % 8 == 0 and wb == WINDOW
    assert CONV_WIDTH == 3 and ts >= CONV_WIDTH - 1 and bd % MLA_SAMPLE_BATCHES == 0
    assert s % KB_ODD == 0 and s % TQ_EVEN == 0 and KB_ODD % TQ_ODD == 0
    n_pt = n_p // TM

    x = (x_prompt.reshape(n_p, d), x_sample.reshape(n_s, d))
    bias_p, bias_s = _bias_tables(rel_bias, ts)
    cos_np, sin_np = _rope_tables_np(s, past, ts, bd)
    cos_t, sin_t = jnp.asarray(cos_np), jnp.asarray(sin_np)

    w1 = tuple(w.astype(BF16) for w in (ffn1_w_gate, ffn1_w_up, ffn1_w_down))
    w2 = tuple(w.astype(BF16) for w in (ffn2_w_gate, ffn2_w_up, ffn2_w_down))
    w_out_even, w_out_odd = even_w_out.astype(BF16), odd_w_out.astype(BF16)
    pool_kpet = jnp.swapaxes(cache_mla_kpe, 2, 3)

    outs = {k: [] for k in ("kp", "vp", "ks", "vs", "sv", "cp", "cs", "ckp", "kpp", "cks", "kps")}
    for l in range(depth):
        x = _ffn(x, ffn1_norm[l], *w1, l, n_pt)
        if l % 2 == 0:
            e = l // 2
            q, kv, ug, kv_p = _inproj_even(x, mix_norm[l], even_w_in[e].astype(BF16), n_pt, s)
            mix_p = _mix_even_prompt(q, kv, ug, bias_p, attn_sinks[e], sgu_ln_g[e], sgu_ln_b[e], sgu_w[e],
                                     sgu_b[e], nb, s)
            a_s, new_k, new_v = _mix_even_sample(q[n_p:].astype(F32), kv[n_p:],
                                                 cache_swa_k[e].reshape(bd, wb, A_KV_W),
                                                 cache_swa_v[e].reshape(bd, wb, A_KV_W), bias_s, attn_sinks[e], ts)
            sg_s, v_rows = _sgu_sample(ug, sgu_ln_g[e], sgu_ln_b[e], sgu_w[e], sgu_b[e], n_p, ts, bd)
            mix_s = jnp.concatenate([a_s.astype(BF16), sg_s], axis=1)
            w_out = (w_out_even, e)
            outs["kp"].append(kv_p[..., :A_KV_W].reshape(nb, wb, A_KV_HEADS, HEAD_DIM))
            outs["vp"].append(kv_p[..., A_KV_W:].reshape(nb, wb, A_KV_HEADS, HEAD_DIM))
            outs["ks"].append(new_k.reshape(bd, wb, A_KV_HEADS, HEAD_DIM))
            outs["vs"].append(new_v.reshape(bd, wb, A_KV_HEADS, HEAD_DIM))
            outs["sv"].append(v_rows.reshape(bd, ts, SG_WIDTH))
        else:
            j = l // 2
            w_in_p, w_uq_r, wuk_ext, wuv_ext = _prep_odd_weights(odd_w_in[j], mla_w_uq[j], mla_w_uk[j], mla_w_uv[j])
            ci, qn, qp, kc, ckv_p, kpe_p, ckv_s, kpe_s = _inproj_odd(
                x, mix_norm[l], w_in_p, mla_q_norm[j], w_uq_r, mla_kv_norm[j], cos_t, sin_t, n_pt, s)
            mix_p, cst_p = _mix_odd_prompt(ci, qn, qp, kc, wuk_ext, wuv_ext, conv_w[j], nb, s)
            qx = _qx_sample(qn, qp, wuk_ext, n_p, n_s).reshape(MLA_HEADS, n_s, KC_W)
            o_lat = _mla_sample(page_table, qx, kc[n_p:].astype(F32), cache_mla_ckv, pool_kpet, j, ts)
            n_st = CONV_WIDTH - 1
            state_rows = jnp.pad(state_conv[j], ((0, 0), (0, ts - n_st), (0, 0))).reshape(n_s, C_WIDTH)
            mix_s, z_s = _odd_sample_tail(ci, state_rows, conv_w[j], o_lat, wuv_ext, n_p, ts)
            w_out = (w_out_odd, j)
            outs["cp"].append(cst_p[:, 8 - n_st:])
            outs["cs"].append(z_s.reshape(bd, ts, C_WIDTH)[:, ts - n_st:])
            outs["ckp"].append(ckv_p.reshape(nb, s, KV_LORA))
            outs["kpp"].append(kpe_p.reshape(nb, s, ROPE_DIM))
            outs["cks"].append(ckv_s.reshape(bd, ts, KV_LORA))
            outs["kps"].append(kpe_s.reshape(bd, ts, ROPE_DIM))
        x = _ffn(x, ffn2_norm[l], *w2, l, n_pt, mix=(mix_p, mix_s) + w_out,
                 final_g=final_norm if l == depth - 1 else None)
    y_prompt = x[0].reshape(nb, s, d)
    y_sample = x[1].reshape(bd, ts, d)
    st = lambda k: jnp.stack(outs[k])
    return (y_prompt, y_sample, st("kp"), st("vp"), st("ks"), st("vs"), st("sv"), st("cp"), st("cs"),
            st("ckp"), st("kpp"), st("cks"), st("kps"))
```

```python
import functools
import math

import numpy as np
import jax
import jax.numpy as jnp
from jax import lax
from jax.experimental import pallas as pl
from jax.experimental.pallas import tpu as pltpu

F32 = jnp.float32
BF16 = jnp.bfloat16

HEAD_DIM = 64
A_HEADS = 8
A_KV_HEADS = 2
A_Q_W = A_HEADS * HEAD_DIM
A_KV_W = A_KV_HEADS * HEAD_DIM
WINDOW = 128
N_BUCKETS = 32
MAX_DISTANCE = 128
SG_GROUPS = 4
SG_GROUP_DIM = 128
SG_WIDTH = SG_GROUPS * SG_GROUP_DIM
CHUNK = 128
C_WIDTH = 512
CONV_WIDTH = 3
MLA_HEADS = 8
NOPE_DIM = 64
ROPE_DIM = 32
V_DIM = 64
Q_LORA = 768
KV_LORA = 256
ROPE_THETA = 10000.0
PAGE_SIZE = 128
EPS = 1e-6
NEG = -1e30

LANE = 128
SUBLANE = 8
CONV_HALO = 2 * SUBLANE
KC_W = KV_LORA + LANE
TM = 512
FC = 512
TQ_EVEN = 512
TQ_ODD = 256
KB_ODD = 512
CHAINS_ODD = 2
BB_EVEN = 16
MLA_SAMPLE_CHUNK = 2048
MLA_SAMPLE_BATCHES = 2
MLA_SAMPLE_VMEM = 60 * 1024 * 1024
VMEM_LIMIT = 56 * 1024 * 1024


def _cp(sem, vmem=VMEM_LIMIT):
    return pltpu.CompilerParams(dimension_semantics=sem, vmem_limit_bytes=vmem)


def _dot(a, b):
    return jnp.dot(a, b, preferred_element_type=F32)


def _dot_nt(a, b):
    return lax.dot_general(a, b, (((1,), (1,)), ((), ())), preferred_element_type=F32)


def _rms(x, g):
    ms = jnp.mean(x * x, axis=-1, keepdims=True)
    return x * lax.rsqrt(ms + EPS) * g


def _gelu(x):
    return 0.5 * x * (1.0 + lax.erf(x * np.float32(math.sqrt(0.5))))


def _const_spec(shape):
    n = len(shape)
    return pl.BlockSpec(shape, lambda *_: (0,) * n, pipeline_mode=pl.Buffered(1))


def _smem_spec():
    return pl.BlockSpec(memory_space=pltpu.SMEM)


def _ffn_chunks(f):
    chunks = [(c, FC) for c in range(0, f - f % FC, FC)]
    if f % FC:
        chunks.append((f - f % FC, f % FC))
    return chunks


def _ffn_kernel(*refs, split_in, has_mix, has_final, n_prompt_tiles, chunks):
    it = iter(refs)
    x_ref = next(it)
    if split_in:
        xs_ref = next(it)
    if has_mix:
        mixp_ref, mixs_ref, wout_ref = next(it), next(it), next(it)
    g_ref, wg_ref, wu_ref, wd_ref = next(it), next(it), next(it), next(it)
    if has_final:
        fg_ref = next(it)
    o_ref = next(it)
    if has_final:
        os_ref = next(it)
    acc_ref = next(it)

    is_sample = pl.program_id(0) >= n_prompt_tiles
    x = x_ref[...]
    if split_in:
        x = jnp.where(is_sample, xs_ref[...], x)
    if has_mix:
        mix = jnp.where(is_sample, mixs_ref[...], mixp_ref[...])
        x = x + _dot(mix, wout_ref[...])
    h = _rms(x, g_ref[...]).astype(BF16)
    for k, (c0, cw) in enumerate(chunks):
        g = _dot(h, wg_ref[:, c0:c0 + cw])
        u = _dot(h, wu_ref[:, c0:c0 + cw])
        a = (g * jax.nn.sigmoid(g) * u).astype(BF16)
        d = _dot(a, wd_ref[c0:c0 + cw, :])
        if k == 0:
            acc_ref[...] = d
        else:
            acc_ref[...] += d
    y = x + 0.5 * acc_ref[...]
    if has_final:
        y = _rms(y, fg_ref[...])

        @pl.when(jnp.logical_not(is_sample))
        def _():
            o_ref[...] = y

        @pl.when(is_sample)
        def _():
            os_ref[...] = y
    else:
        o_ref[...] = y


def _layer_spec(shape, layer):
    n = len(shape)
    return pl.BlockSpec((None,) + tuple(shape), lambda *_: (layer,) + (0,) * n, pipeline_mode=pl.Buffered(1))


def _ffn(x, norm_g, w_gate, w_up, w_down, layer, n_prompt_tiles, mix=None, final_g=None):
    n_tiles = n_prompt_tiles + 1
    last_p = n_prompt_tiles - 1
    row = lambda i: (i, 0)
    prompt_row = lambda i: (jnp.minimum(i, last_p), 0)
    sample_row = lambda i: (0, 0)
    split_in = isinstance(x, tuple)
    d = x[0].shape[1] if split_in else x.shape[1]
    t = n_tiles * TM
    f = w_down.shape[1]
    if split_in:
        args, specs = list(x), [pl.BlockSpec((TM, d), prompt_row), pl.BlockSpec((TM, d), sample_row)]
    else:
        args, specs = [x], [pl.BlockSpec((TM, d), row)]
    if mix is not None:
        mix_p, mix_s, w_out, e = mix
        dm = w_out.shape[1]
        args += [mix_p, mix_s, w_out]
        specs += [pl.BlockSpec((TM, dm), prompt_row), pl.BlockSpec((TM, dm), sample_row),
                  _layer_spec((dm, d), e)]
    args += [norm_g.reshape(1, d), w_gate, w_up, w_down]
    specs += [_const_spec((1, d)), _layer_spec((d, f), layer), _layer_spec((d, f), layer),
              _layer_spec((f, d), layer)]
    if final_g is not None:
        args.append(final_g.reshape(1, d))
        specs.append(_const_spec((1, d)))
    body = functools.partial(_ffn_kernel, split_in=split_in, has_mix=mix is not None,
                             has_final=final_g is not None, n_prompt_tiles=n_prompt_tiles, chunks=_ffn_chunks(f))
    if final_g is not None:
        out_specs = [pl.BlockSpec((TM, d), prompt_row), pl.BlockSpec((TM, d), sample_row)]
        out_shape = [jax.ShapeDtypeStruct((n_prompt_tiles * TM, d), F32), jax.ShapeDtypeStruct((TM, d), F32)]
    else:
        out_specs = pl.BlockSpec((TM, d), row)
        out_shape = jax.ShapeDtypeStruct((t, d), F32)
    return pl.pallas_call(
        body,
        grid=(n_tiles,),
        in_specs=specs,
        out_specs=out_specs,
        out_shape=out_shape,
        scratch_shapes=[pltpu.VMEM((TM, d), F32)],
        compiler_params=_cp(("arbitrary",)),
        name="ffn",
    )(*args)


def _t5_bucket_np(dist):
    n = np.maximum(dist, 0)
    max_exact = N_BUCKETS // 2
    nf = np.maximum(n, 1).astype(np.float32)
    large = max_exact + (np.log(nf / np.float32(max_exact)) / np.float32(math.log(MAX_DISTANCE / max_exact))
                         * np.float32(N_BUCKETS - max_exact)).astype(np.int32)
    return np.where(n < max_exact, n, np.minimum(large, N_BUCKETS - 1)).astype(np.int32)


def _bias_tables_np(ts):
    q = np.arange(WINDOW)[:, None]
    j = np.arange(2 * WINDOW)[None, :]
    bucket_p = _t5_bucket_np(q - j + WINDOW)
    nb = SUBLANE // ts
    cols = nb * WINDOW + 2 * SUBLANE
    bucket_s = -np.ones((SUBLANE, cols), np.int32)
    for r in range(SUBLANE):
        bi, t = divmod(r, ts)
        for c in range(cols):
            if c < nb * WINDOW:
                bj, jj = divmod(c, WINDOW)
                dist = t - (jj - WINDOW)
            elif c < nb * WINDOW + SUBLANE:
                bj, tt = divmod(c - nb * WINDOW, ts)
                dist = t - tt
            else:
                continue
            if bj == bi and 0 <= dist <= WINDOW:
                bucket_s[r, c] = _t5_bucket_np(np.array(dist))
    return bucket_p, bucket_s


def _bias_kernel(rb_ref, bp_ref, bs_ref, op_ref, os_ref):
    h = pl.program_id(0)
    bp = bp_ref[...]
    bs = bs_ref[...]
    accp = jnp.zeros(bp.shape, F32)
    accs = jnp.full(bs.shape, NEG, F32)
    for k in range(N_BUCKETS):
        val = rb_ref[k, h]
        accp = jnp.where(bp == k, val, accp)
        accs = jnp.where(bs == k, val, accs)
    op_ref[0] = accp
    os_ref[0] = accs


def _bias_tables(rel_bias, ts):
    bucket_p, bucket_s = _bias_tables_np(ts)
    return pl.pallas_call(
        _bias_kernel,
        grid=(A_HEADS,),
        in_specs=[_smem_spec(),
                  pl.BlockSpec(bucket_p.shape, lambda h: (0, 0)),
                  pl.BlockSpec(bucket_s.shape, lambda h: (0, 0))],
        out_specs=[pl.BlockSpec((1,) + bucket_p.shape, lambda h: (h, 0, 0)),
                   pl.BlockSpec((1,) + bucket_s.shape, lambda h: (h, 0, 0))],
        out_shape=[jax.ShapeDtypeStruct((A_HEADS,) + bucket_p.shape, F32),
                   jax.ShapeDtypeStruct((A_HEADS,) + bucket_s.shape, F32)],
        compiler_params=_cp(("arbitrary",)),
        name="t5_bias",
    )(rel_bias, jnp.asarray(bucket_p), jnp.asarray(bucket_s))


def _inproj_even_kernel(x_ref, g_ref, w_ref, q_ref, kv_ref, ug_ref, last_ref, *, tiles_per_seq, n_prompt_tiles):
    h = _rms(x_ref[...], g_ref[...]).astype(BF16)
    p = _dot(h, w_ref[...])
    q_ref[...] = (p[:, :A_Q_W] * np.float32(HEAD_DIM ** -0.5)).astype(BF16)
    kv = p[:, A_Q_W:A_Q_W + 2 * A_KV_W]
    kv_ref[...] = kv
    ug_ref[...] = p[:, A_Q_W + 2 * A_KV_W:].astype(BF16)
    i = pl.program_id(0)

    @pl.when(((i + 1) % tiles_per_seq == 0) & (i < n_prompt_tiles))
    def _():
        last_ref[...] = kv[TM - WINDOW:]


def _inproj_even(x, norm_g, w_in, n_prompt_tiles, s):
    t, d = x.shape
    n = w_in.shape[1]
    row = lambda i: (i, 0)
    per_seq = s // TM
    nb = n_prompt_tiles // per_seq
    return pl.pallas_call(
        functools.partial(_inproj_even_kernel, tiles_per_seq=per_seq, n_prompt_tiles=n_prompt_tiles),
        grid=(t // TM,),
        in_specs=[pl.BlockSpec((TM, d), row), _const_spec((1, d)), _const_spec((d, n))],
        out_specs=[pl.BlockSpec((TM, A_Q_W), row), pl.BlockSpec((TM, 2 * A_KV_W), row),
                   pl.BlockSpec((TM, 2 * SG_WIDTH), row),
                   pl.BlockSpec((None, WINDOW, 2 * A_KV_W), lambda i: (jnp.minimum(i // per_seq, nb - 1), 0, 0))],
        out_shape=[jax.ShapeDtypeStruct((t, A_Q_W), BF16), jax.ShapeDtypeStruct((t, 2 * A_KV_W), F32),
                   jax.ShapeDtypeStruct((t, 2 * SG_WIDTH), BF16),
                   jax.ShapeDtypeStruct((nb, WINDOW, 2 * A_KV_W), F32)],
        compiler_params=_cp(("arbitrary",)),
        name="inproj_even",
    )(x, norm_g.reshape(1, d), w_in)


def _head_variants(x):
    lo = lax.broadcasted_iota(jnp.int32, x.shape, 1) < HEAD_DIM
    xr = pltpu.roll(x, HEAD_DIM, 1)
    z = jnp.zeros_like(x)
    return [[jnp.where(lo, x, z).astype(BF16), jnp.where(lo, z, xr).astype(BF16)],
            [jnp.where(lo, xr, z).astype(BF16), jnp.where(lo, z, x).astype(BF16)]]


def _sink_softmax_pv(s, sink, v):
    mx = jnp.maximum(jnp.max(s, axis=-1, keepdims=True), sink)
    p = jnp.exp(s - mx)
    den = jnp.sum(p, axis=-1, keepdims=True) + jnp.exp(sink - mx)
    return _dot(p.astype(BF16), v) * (1.0 / den)


def _layer_norm(x, g, b):
    xc = x - jnp.mean(x, axis=-1, keepdims=True)
    y = xc * lax.rsqrt(jnp.mean(xc * xc, axis=-1, keepdims=True) + EPS)
    return y * g + b


def _even_prompt_kernel(q_ref, kv_ref, kvp_ref, ug_ref, bias_ref, sink_ref, lng_ref, lnb_ref, ws_ref,
                        bst_ref, o_ref, *, tq):
    nblk = tq // WINDOW
    step = pl.program_id(1)
    kv_all = jnp.concatenate([kvp_ref[...], kv_ref[...]], axis=0)
    kvar = _head_variants(kv_all[:, :A_KV_W])
    vvar = _head_variants(kv_all[:, A_KV_W:])
    qi = lax.broadcasted_iota(jnp.int32, (WINDOW, 2 * WINDOW), 0)
    kj = lax.broadcasted_iota(jnp.int32, (WINDOW, 2 * WINDOW), 1)
    dist = qi - kj + WINDOW
    band = (dist >= 0) & (dist <= WINDOW)
    cur = kj >= WINDOW
    ri = lax.broadcasted_iota(jnp.int32, (CHUNK, CHUNK), 0)
    ci = lax.broadcasted_iota(jnp.int32, (CHUNK, CHUNK), 1)
    wt = [jnp.where(ri >= ci, ws_ref[g], 0.0).astype(BF16) for g in range(SG_GROUPS)]
    lng, lnb = lng_ref[...], lnb_ref[...]
    for n in range(nblk):
        r0 = n * WINDOW
        has_prev = (step * nblk + n) > 0
        mask = band & (cur | has_prev)
        for m in range(A_HEADS // 2):
            kvh = (2 * m) // (A_HEADS // A_KV_HEADS)
            qp = q_ref[r0:r0 + WINDOW, LANE * m:LANE * (m + 1)]
            o = None
            for half in range(2):
                hd = 2 * m + half
                s = _dot_nt(qp, kvar[kvh][half][r0:r0 + 2 * WINDOW])
                s = jnp.where(mask, s + bias_ref[hd], NEG)
                oh = _sink_softmax_pv(s, sink_ref[hd], vvar[kvh][half][r0:r0 + 2 * WINDOW])
                o = oh if o is None else o + oh
            o_ref[r0:r0 + WINDOW, LANE * m:LANE * (m + 1)] = o.astype(BF16)
        u = _gelu(ug_ref[r0:r0 + WINDOW, :SG_WIDTH].astype(F32))
        v = _layer_norm(_gelu(ug_ref[r0:r0 + WINDOW, SG_WIDTH:].astype(F32)), lng, lnb)
        for g in range(SG_GROUPS):
            sl = slice(SG_GROUP_DIM * g, SG_GROUP_DIM * (g + 1))
            y = _dot(wt[g], v[:, sl].astype(BF16)) + bst_ref[:, g:g + 1]
            o_ref[r0:r0 + WINDOW, A_Q_W + SG_GROUP_DIM * g:A_Q_W + SG_GROUP_DIM * (g + 1)] = (
                u[:, sl] * y).astype(BF16)


def _mix_even_prompt(q, kv, ug, bias_p, sinks, ln_g, ln_b, w_s, b_s, nb, s):
    tq = TQ_EVEN
    nq = s // tq
    blk = lambda b, i: (b * nq + i, 0)
    prev = lambda b, i: (jnp.maximum((b * nq + i) * (tq // WINDOW) - 1, 0), 0)
    return pl.pallas_call(
        functools.partial(_even_prompt_kernel, tq=tq),
        grid=(nb, nq),
        in_specs=[pl.BlockSpec((tq, A_Q_W), blk), pl.BlockSpec((tq, 2 * A_KV_W), blk),
                  pl.BlockSpec((WINDOW, 2 * A_KV_W), prev), pl.BlockSpec((tq, 2 * SG_WIDTH), blk),
                  _const_spec(bias_p.shape), _smem_spec(), _const_spec((1, SG_WIDTH)),
                  _const_spec((1, SG_WIDTH)), _const_spec(w_s.shape), _const_spec((CHUNK, SG_GROUPS))],
        out_specs=pl.BlockSpec((tq, A_Q_W + SG_WIDTH), blk),
        out_shape=jax.ShapeDtypeStruct((nb * s, A_Q_W + SG_WIDTH), BF16),
        compiler_params=_cp(("arbitrary", "arbitrary")),
        name="mix_even_prompt",
    )(q, kv, kv, ug, bias_p, sinks, ln_g.reshape(1, -1), ln_b.reshape(1, -1), w_s, b_s.T)


def _even_sample_kernel(q_ref, kvn_ref, ck_ref, cv_ref, bias_ref, sink_ref, a_ref, nk_ref, nv_ref, *, ts):
    bb = ck_ref.shape[0]
    gr = SUBLANE
    gpb = gr // ts
    wb = ck_ref.shape[1]
    n_cache = gpb * wb
    hpk = A_HEADS // A_KV_HEADS
    zpad = jnp.zeros((gr, A_KV_W), F32)

    def lo_half(x, kvh):
        lo = lax.broadcasted_iota(jnp.int32, x.shape, 1) < HEAD_DIM
        return jnp.where(lo, x if kvh == 0 else pltpu.roll(x, HEAD_DIM, 1), 0.0).astype(BF16)

    hrow = lax.broadcasted_iota(jnp.int32, (gr * hpk, 1), 0) // gr
    bias_c, bias_n, sink = [], [], []
    for kvh in range(A_KV_HEADS):
        heads = [hpk * kvh + i for i in range(hpk)]
        bias_c.append(jnp.concatenate([bias_ref[h][:, :n_cache] for h in heads], axis=0))
        bias_n.append(jnp.concatenate([bias_ref[h][:, n_cache:] for h in heads], axis=0))
        sk = jnp.full((gr * hpk, 1), sink_ref[heads[0]], F32)
        for i in range(1, hpk):
            sk = jnp.where(hrow == i, sink_ref[heads[i]], sk)
        sink.append(sk)
    for g in range(bb // gpb):
        r0 = gr * g
        qg = q_ref[r0:r0 + gr, :]
        kn = kvn_ref[r0:r0 + gr, :A_KV_W]
        vn = kvn_ref[r0:r0 + gr, A_KV_W:]
        kc = jnp.concatenate([ck_ref[gpb * g + i] for i in range(gpb)], axis=0)
        vc = jnp.concatenate([cv_ref[gpb * g + i] for i in range(gpb)], axis=0)
        knp = jnp.concatenate([kn, zpad], axis=0)
        vnp = jnp.concatenate([vn, zpad], axis=0)
        for kvh in range(A_KV_HEADS):
            c0 = 2 * LANE * kvh
            qs = jnp.concatenate([lo_half(qg[:, c0 + LANE * (i // 2):c0 + LANE * (i // 2 + 1)], i % 2)
                                  for i in range(hpk)], axis=0)
            sc = _dot_nt(qs, lo_half(kc, kvh))
            sn = _dot_nt(qs, lo_half(knp, kvh))
            sc = jnp.where(bias_c[kvh] > 0.5 * NEG, sc + bias_c[kvh], NEG)
            sn = jnp.where(bias_n[kvh] > 0.5 * NEG, sn + bias_n[kvh], NEG)
            mx = jnp.maximum(jnp.maximum(jnp.max(sc, axis=-1, keepdims=True), jnp.max(sn, axis=-1, keepdims=True)),
                             sink[kvh])
            pc = jnp.exp(sc - mx)
            pn = jnp.exp(sn - mx)
            den = (jnp.sum(pc, axis=-1, keepdims=True) + jnp.sum(pn, axis=-1, keepdims=True)
                   + jnp.exp(sink[kvh] - mx))
            o = (_dot(pc.astype(BF16), lo_half(vc, kvh)) + _dot(pn.astype(BF16), lo_half(vnp, kvh))) * (1.0 / den)
            for i in range(hpk // 2):
                even, odd = o[2 * gr * i:2 * gr * i + gr], o[2 * gr * i + gr:2 * gr * (i + 1)]
                a_ref[r0:r0 + gr, c0 + LANE * i:c0 + LANE * (i + 1)] = even + pltpu.roll(odd, HEAD_DIM, 1)
        for i in range(gpb):
            b = gpb * g + i
            nk_ref[b, 0:wb - ts, :] = ck_ref[b, ts:wb, :]
            nk_ref[b, wb - ts:wb, :] = kn[ts * i:ts * (i + 1)]
            nv_ref[b, 0:wb - ts, :] = cv_ref[b, ts:wb, :]
            nv_ref[b, wb - ts:wb, :] = vn[ts * i:ts * (i + 1)]


def _mix_even_sample(q_b, kvn_b, cache_k, cache_v, bias_s, sinks, ts):
    bd, wb, _ = cache_k.shape
    bb = BB_EVEN
    rows = bb * ts
    blk2 = lambda i: (i, 0)
    blk3 = lambda i: (i, 0, 0)
    return pl.pallas_call(
        functools.partial(_even_sample_kernel, ts=ts),
        grid=(bd // bb,),
        in_specs=[pl.BlockSpec((rows, A_Q_W), blk2), pl.BlockSpec((rows, 2 * A_KV_W), blk2),
                  pl.BlockSpec((bb, wb, A_KV_W), blk3), pl.BlockSpec((bb, wb, A_KV_W), blk3),
                  _const_spec(bias_s.shape), _smem_spec()],
        out_specs=[pl.BlockSpec((rows, A_Q_W), blk2), pl.BlockSpec((bb, wb, A_KV_W), blk3),
                   pl.BlockSpec((bb, wb, A_KV_W), blk3)],
        out_shape=[jax.ShapeDtypeStruct((bd * ts, A_Q_W), F32),
                   jax.ShapeDtypeStruct(cache_k.shape, F32), jax.ShapeDtypeStruct(cache_v.shape, F32)],
        compiler_params=_cp(("arbitrary",)),
        name="mix_even_sample",
    )(q_b, kvn_b, cache_k, cache_v, bias_s, sinks)


def _token_col(rows, ts):
    return lax.broadcasted_iota(jnp.int32, (rows, 1), 0) % ts


def _sgu_sample_kernel(ug_ref, lng_ref, lnb_ref, w_ref, b_ref, sg_ref, v_ref, *, ts):
    u = _gelu(ug_ref[:, :SG_WIDTH].astype(F32))
    v = _layer_norm(_gelu(ug_ref[:, SG_WIDTH:].astype(F32)), lng_ref[...], lnb_ref[...])
    v_ref[...] = v
    rows = v.shape[0]
    t = _token_col(rows, ts)
    back = [v] + [pltpu.roll(v, k, 0) for k in range(1, ts)]
    for g in range(SG_GROUPS):
        sl = slice(SG_GROUP_DIM * g, SG_GROUP_DIM * (g + 1))
        y = jnp.zeros((rows, 1), F32)
        for i in range(ts):
            y = jnp.where(t == i, b_ref[g, i], y)
        for k in range(ts):
            wk = jnp.zeros((rows, 1), F32)
            for i in range(k, ts):
                wk = jnp.where(t == i, w_ref[g, i * ts + i - k], wk)
            y = y + wk * back[k][:, sl]
        sg_ref[:, sl] = (u[:, sl] * y).astype(BF16)


def _sgu_sample(ug, ln_g, ln_b, w_s, b_s, n_prompt_rows, ts, bd):
    rows = ts * bd
    off = n_prompt_rows // rows
    w_small = w_s[:, :ts, :ts].reshape(SG_GROUPS, ts * ts)
    b_small = b_s[:, :ts]
    return pl.pallas_call(
        functools.partial(_sgu_sample_kernel, ts=ts),
        grid=(1,),
        in_specs=[pl.BlockSpec((rows, 2 * SG_WIDTH), lambda i: (off, 0)), _const_spec((1, SG_WIDTH)),
                  _const_spec((1, SG_WIDTH)), _smem_spec(), _smem_spec()],
        out_specs=[pl.BlockSpec((rows, SG_WIDTH), lambda i: (0, 0)),
                   pl.BlockSpec((rows, SG_WIDTH), lambda i: (0, 0))],
        out_shape=[jax.ShapeDtypeStruct((rows, SG_WIDTH), BF16), jax.ShapeDtypeStruct((rows, SG_WIDTH), F32)],
        compiler_params=_cp(("arbitrary",)),
        name="sgu_sample",
    )(ug, ln_g.reshape(1, -1), ln_b.reshape(1, -1), w_small, b_small)


def _rope_tables_np(s, past, ts, bd):
    half = ROPE_DIM // 2
    inv = np.exp(-math.log(ROPE_THETA) * np.arange(half, dtype=np.float32) / np.float32(half)).astype(np.float32)
    pos = np.concatenate([np.arange(s), past + np.tile(np.arange(ts), bd)]).astype(np.float32)
    ang = (pos[:, None] * inv[None, :]).astype(np.float32).astype(np.float64)
    cos = np.cos(ang)
    sin = np.sin(ang)
    cos_t = np.tile(np.concatenate([cos, cos], axis=1), (1, MLA_HEADS)).astype(np.float32)
    sin_t = np.tile(np.concatenate([-sin, sin], axis=1), (1, MLA_HEADS)).astype(np.float32)
    return cos_t, sin_t


def _rope(x, cos_t, sin_t):
    n = x.shape[1]
    half = ROPE_DIM // 2
    first = (lax.broadcasted_iota(jnp.int32, x.shape, 1) % ROPE_DIM) < half
    partner = jnp.where(first, pltpu.roll(x, n - half, 1), pltpu.roll(x, half, 1))
    return x * cos_t + partner * sin_t


def _inproj_odd_kernel(x_ref, g_ref, w_ref, qg_ref, wuq_ref, kvg_ref, cos_ref, sin_ref,
                       ci_ref, qn_ref, qp_ref, kc_ref, ckvp_ref, kpep_ref, ckvs_ref, kpes_ref, *, n_prompt_tiles):
    h = _rms(x_ref[...], g_ref[...]).astype(BF16)
    p = _dot(h, w_ref[...])
    o_q = 3 * C_WIDTH
    o_kv = o_q + Q_LORA
    o_pe = o_kv + KV_LORA
    ci_ref[...] = p[:, :o_q].astype(BF16)
    scale = np.float32((NOPE_DIM + ROPE_DIM) ** -0.5 * math.log2(math.e))
    q = _dot(_rms(p[:, o_q:o_kv], qg_ref[...]).astype(BF16), wuq_ref[...])
    n_nope = MLA_HEADS * NOPE_DIM
    cos_t, sin_t = cos_ref[...], sin_ref[...]
    qn_ref[...] = (q[:, :n_nope] * scale).astype(BF16)
    qp_ref[...] = (_rope(q[:, n_nope:], cos_t, sin_t) * scale).astype(BF16)
    ckv = _rms(p[:, o_kv:o_pe], kvg_ref[...])
    kpe = _rope(p[:, o_pe:], cos_t[:, :LANE], sin_t[:, :LANE])
    kc_ref[:, :KV_LORA] = ckv.astype(BF16)
    kc_ref[:, KV_LORA:] = kpe.astype(BF16)
    is_sample = pl.program_id(0) >= n_prompt_tiles

    @pl.when(jnp.logical_not(is_sample))
    def _():
        ckvp_ref[...] = ckv
        kpep_ref[...] = kpe[:, :ROPE_DIM]

    @pl.when(is_sample)
    def _():
        ckvs_ref[...] = ckv
        kpes_ref[...] = kpe[:, :ROPE_DIM]


def _inproj_odd(x, norm_g, w_in, q_norm, w_uq, kv_norm, cos_t, sin_t, n_prompt_tiles, s):
    t, d = x.shape
    n = w_in.shape[1]
    row = lambda i: (i, 0)
    last_p = n_prompt_tiles - 1
    prompt_row = lambda i: (jnp.minimum(i, last_p), 0)
    sample_row = lambda i: (0, 0)
    per_seq = s // TM
    tab = lambda i: (jnp.where(i < n_prompt_tiles, i % per_seq, per_seq), 0)
    n_pe = MLA_HEADS * ROPE_DIM
    n_p = n_prompt_tiles * TM
    widths = [(3 * C_WIDTH, BF16), (MLA_HEADS * NOPE_DIM, BF16), (n_pe, BF16), (KC_W, BF16)]
    return pl.pallas_call(
        functools.partial(_inproj_odd_kernel, n_prompt_tiles=n_prompt_tiles),
        grid=(t // TM,),
        in_specs=[pl.BlockSpec((TM, d), row), _const_spec((1, d)), _const_spec((d, n)),
                  _const_spec((1, Q_LORA)), _const_spec(w_uq.shape), _const_spec((1, KV_LORA)),
                  pl.BlockSpec((TM, n_pe), tab), pl.BlockSpec((TM, n_pe), tab)],
        out_specs=[pl.BlockSpec((TM, w), row) for w, _ in widths] + [
            pl.BlockSpec((TM, KV_LORA), prompt_row), pl.BlockSpec((TM, ROPE_DIM), prompt_row),
            pl.BlockSpec((TM, KV_LORA), sample_row), pl.BlockSpec((TM, ROPE_DIM), sample_row)],
        out_shape=[jax.ShapeDtypeStruct((t, w), dt) for w, dt in widths] + [
            jax.ShapeDtypeStruct((n_p, KV_LORA), F32), jax.ShapeDtypeStruct((n_p, ROPE_DIM), F32),
            jax.ShapeDtypeStruct((TM, KV_LORA), F32), jax.ShapeDtypeStruct((TM, ROPE_DIM), F32)],
        compiler_params=_cp(("arbitrary",)),
        name="inproj_odd",
    )(x, norm_g.reshape(1, d), w_in, q_norm.reshape(1, -1), w_uq, kv_norm.reshape(1, -1), cos_t, sin_t)


def _build_qx(qn_ref, qp_ref, wuk_ref, qx_ref, rows):
    qp = qp_ref[...].astype(F32)
    keep = lax.broadcasted_iota(jnp.int32, (rows, LANE), 1) < ROPE_DIM
    n_pe = MLA_HEADS * ROPE_DIM
    for h in range(MLA_HEADS):
        pair = qn_ref[:, LANE * (h // 2):LANE * (h // 2 + 1)]
        qx_ref[h * rows:(h + 1) * rows, :KV_LORA] = _dot(pair, wuk_ref[h]).astype(qx_ref.dtype)
        pe = qp if h == 0 else pltpu.roll(qp, n_pe - ROPE_DIM * h, 1)
        qx_ref[h * rows:(h + 1) * rows, KV_LORA:] = jnp.where(keep, pe[:, :LANE], 0.0).astype(qx_ref.dtype)


def _odd_prompt_kernel(ci_ref, cih_ref, qn_ref, qp_ref, kc_ref, wuk_ref, wuv_ref, cw_ref,
                       o_ref, cst_ref, qx_ref, m_ref, l_ref, acc_ref, z_ref, *, tq, kb):
    step = pl.program_id(1)
    halo = CONV_HALO
    zh = cih_ref[:, C_WIDTH:2 * C_WIDTH].astype(F32) * cih_ref[:, 2 * C_WIDTH:].astype(F32)
    z_ref[0:halo, :] = jnp.where(step > 0, zh, 0.0)
    z = ci_ref[:, C_WIDTH:2 * C_WIDTH].astype(F32) * ci_ref[:, 2 * C_WIDTH:].astype(F32)
    z_ref[halo:halo + tq, :] = z
    cw = cw_ref[...]
    y = cw[0:1] * z_ref[halo - 2:halo - 2 + tq, :] + cw[1:2] * z_ref[halo - 1:halo - 1 + tq, :] + cw[2:3] * z
    o_ref[:, :C_WIDTH] = (ci_ref[:, :C_WIDTH].astype(F32) * y).astype(BF16)
    cst_ref[0] = z[tq - SUBLANE:tq]

    _build_qx(qn_ref, qp_ref, wuk_ref, qx_ref, tq)
    rows = MLA_HEADS * tq
    n_full = (step * tq) // kb
    hpc = MLA_HEADS // CHAINS_ODD

    def kv_step(j, width, mask_from, first):
        k0 = j * kb if isinstance(j, int) else pl.multiple_of(j * kb, kb)
        kc = kc_ref[pl.ds(k0, width), :]
        n_grp = width // LANE
        if mask_from < n_grp:
            qpos = step * tq + lax.broadcasted_iota(jnp.int32, (tq, LANE), 0)
            kpos = k0 + lax.broadcasted_iota(jnp.int32, (tq, LANE), 1)
            visible = {g: kpos + LANE * g <= qpos for g in range(mask_from, n_grp)}
        for c in range(CHAINS_ODD):
            rc = slice(c * hpc * tq, (c + 1) * hpc * tq)
            s_all = _dot_nt(qx_ref[rc, :], kc)
            ps, alphas = [], []
            for hh in range(hpc):
                r = slice((c * hpc + hh) * tq, (c * hpc + hh + 1) * tq)
                sg = [s_all[hh * tq:(hh + 1) * tq, LANE * g:LANE * (g + 1)] for g in range(n_grp)]
                sg = [jnp.where(visible[g], x, NEG) if g >= mask_from else x for g, x in enumerate(sg)]
                m_blk = jnp.max(functools.reduce(jnp.maximum, sg), axis=-1, keepdims=True)
                if first:
                    m_new = jnp.broadcast_to(m_blk, (tq, LANE))
                else:
                    m_old = m_ref[r, :]
                    m_new = jnp.maximum(m_old, m_blk)
                    alpha = jnp.exp2(m_old - m_new)
                    alphas.append(jnp.concatenate([alpha] * (KV_LORA // LANE), axis=1))
                pg = [jnp.exp2(x - m_new) for x in sg]
                psum = functools.reduce(jnp.add, pg)
                l_ref[r, :] = psum if first else alpha * l_ref[r, :] + psum
                m_ref[r, :] = m_new
                ps.append(jnp.concatenate([x.astype(BF16) for x in pg], axis=1))
            pv = _dot(jnp.concatenate(ps, axis=0), kc[:, :KV_LORA])
            acc_ref[rc, :] = pv if first else jnp.concatenate(alphas, axis=0) * acc_ref[rc, :] + pv

    @pl.when(n_full >= 2)
    def _():
        kv_step(0, kb, kb // LANE, True)

    def full_step(j, carry):
        kv_step(j, kb, kb // LANE, False)
        return carry

    lax.fori_loop(1, n_full - 1, full_step, 0)
    for i, width in enumerate(range(tq, kb + 1, tq)):
        on_diag = step % (kb // tq) == i

        @pl.when(on_diag & (n_full == 0))
        def _(width=width):
            kv_step(0, width, 0, True)

        @pl.when(on_diag & (n_full == 1))
        def _(width=width):
            kv_step(0, kb + width, kb // LANE, True)

        @pl.when(on_diag & (n_full >= 2))
        def _(width=width):
            kv_step(n_full - 1, kb + width, kb // LANE, False)

    for m in range(MLA_HEADS // 2):
        o = None
        for half in range(2):
            h = 2 * m + half
            inv_l = jnp.broadcast_to(1.0 / jnp.sum(l_ref[h * tq:(h + 1) * tq, :], axis=-1, keepdims=True), (tq, LANE))
            ol = (acc_ref[h * tq:(h + 1) * tq, :] * jnp.concatenate([inv_l] * (KV_LORA // LANE), axis=1)).astype(BF16)
            oh = _dot(ol, wuv_ref[h])
            o = oh if o is None else o + oh
        o_ref[:, C_WIDTH + LANE * m:C_WIDTH + LANE * (m + 1)] = o.astype(BF16)


def _mix_odd_prompt(ci, qn, qp, kc, wuk_ext, wuv_ext, conv_w, nb, s):
    tq, kb = TQ_ODD, KB_ODD
    nq = s // tq
    blk = lambda b, i: (b * nq + i, 0)
    halo = lambda b, i: (jnp.maximum((b * nq + i) * (tq // CONV_HALO) - 1, 0), 0)
    rows = MLA_HEADS * tq
    n_out = C_WIDTH + MLA_HEADS * V_DIM
    return pl.pallas_call(
        functools.partial(_odd_prompt_kernel, tq=tq, kb=kb),
        grid=(nb, nq),
        in_specs=[pl.BlockSpec((tq, 3 * C_WIDTH), blk), pl.BlockSpec((CONV_HALO, 3 * C_WIDTH), halo),
                  pl.BlockSpec((tq, MLA_HEADS * NOPE_DIM), blk), pl.BlockSpec((tq, MLA_HEADS * ROPE_DIM), blk),
                  pl.BlockSpec((s, KC_W), lambda b, i: (b, 0)),
                  _const_spec(wuk_ext.shape), _const_spec(wuv_ext.shape), _const_spec(conv_w.shape)],
        out_specs=[pl.BlockSpec((tq, n_out), blk), pl.BlockSpec((1, SUBLANE, C_WIDTH), lambda b, i: (b, 0, 0))],
        out_shape=[jax.ShapeDtypeStruct((nb * s, n_out), BF16), jax.ShapeDtypeStruct((nb, SUBLANE, C_WIDTH), F32)],
        scratch_shapes=[pltpu.VMEM((rows, KC_W), BF16),
                        pltpu.VMEM((rows, LANE), F32), pltpu.VMEM((rows, LANE), F32), pltpu.VMEM((rows, KV_LORA), F32),
                        pltpu.VMEM((tq + CONV_HALO, C_WIDTH), F32)],
        compiler_params=_cp(("arbitrary", "arbitrary")),
        name="mix_odd_prompt",
    )(ci, ci, qn, qp, kc, wuk_ext, wuv_ext, conv_w)


def _qx_sample_kernel(qn_ref, qp_ref, wuk_ref, qx_ref, *, rows):
    _build_qx(qn_ref, qp_ref, wuk_ref, qx_ref, rows)


def _qx_sample(qn, qp, wuk_ext, n_prompt_rows, rows):
    off = n_prompt_rows // rows
    return pl.pallas_call(
        functools.partial(_qx_sample_kernel, rows=rows),
        grid=(1,),
        in_specs=[pl.BlockSpec((rows, MLA_HEADS * NOPE_DIM), lambda i: (off, 0)),
                  pl.BlockSpec((rows, MLA_HEADS * ROPE_DIM), lambda i: (off, 0)), _const_spec(wuk_ext.shape)],
        out_specs=pl.BlockSpec((MLA_HEADS * rows, KC_W), lambda i: (0, 0)),
        out_shape=jax.ShapeDtypeStruct((MLA_HEADS * rows, KC_W), F32),
        compiler_params=_cp(("arbitrary",)),
        name="qx_sample",
    )(qn, qp, wuk_ext)


def _page_copies(pt_ref, pool_ckv, pool_kpet, ckv_buf, kpe_buf, sem, layer, n_pages, step, slot):
    copies = []
    for i in range(MLA_SAMPLE_BATCHES):
        b = step * MLA_SAMPLE_BATCHES + i
        for p in range(n_pages):
            page = pt_ref[b, p]
            keys = pl.ds(p * PAGE_SIZE, PAGE_SIZE)
            copies.append(pltpu.make_async_copy(pool_ckv.at[layer, page], ckv_buf.at[slot, i, keys],
                                                sem.at[0, slot]))
            copies.append(pltpu.make_async_copy(pool_kpet.at[layer, page], kpe_buf.at[slot, i, :, keys],
                                                sem.at[1, slot]))
    return copies


def _mla_sample_kernel(pt_ref, qx_ref, kn_ref, pool_ckv, pool_kpet, o_ref, ckv_buf, kpe_buf, q_scr, sem, *,
                       ts, n_pages, layer, chunk):
    step = pl.program_id(0)
    n_steps = pl.num_programs(0)
    slot = step % 2
    copies = functools.partial(_page_copies, pt_ref, pool_ckv, pool_kpet, ckv_buf, kpe_buf, sem, layer, n_pages)

    @pl.when(step == 0)
    def _():
        for c in copies(0, 0):
            c.start()

    @pl.when(step + 1 < n_steps)
    def _():
        for c in copies(step + 1, 1 - slot):
            c.start()

    for c in copies(step, slot):
        c.wait()

    rows = MLA_HEADS * ts
    n_new = kn_ref.shape[0]
    t_q = lax.broadcasted_iota(jnp.int32, (rows, n_new), 0) % ts
    col = lax.broadcasted_iota(jnp.int32, (rows, n_new), 1)
    kn = kn_ref[...].astype(BF16)
    past = n_pages * PAGE_SIZE
    starts = range(0, past, chunk)
    for i in range(MLA_SAMPLE_BATCHES):
        for h in range(MLA_HEADS):
            q_scr[i, ts * h:ts * (h + 1), :] = qx_ref[h, ts * i:ts * (i + 1), :]
        qx = q_scr[i].astype(BF16)
        q_lat = qx[:, :KV_LORA]
        q_pe = qx[:, KV_LORA:KV_LORA + ROPE_DIM]
        sn = jnp.where((col // ts == i) & (col % ts <= t_q), _dot_nt(qx, kn), NEG)
        kcs = [ckv_buf[slot, i, k0:k0 + chunk, :].astype(BF16) for k0 in starts]
        ss = [_dot_nt(q_lat, kc) + _dot(q_pe, kpe_buf[slot, i, :, k0:k0 + chunk].astype(BF16))
              for k0, kc in zip(starts, kcs)]
        m = functools.reduce(jnp.maximum, [jnp.max(s, axis=-1, keepdims=True) for s in ss],
                             jnp.max(sn, axis=-1, keepdims=True))
        pn = jnp.exp2(sn - m)
        den = jnp.sum(pn, axis=-1, keepdims=True)
        pnb = pn.astype(BF16)
        accs = [_dot(pnb, kn[:, LANE * g:LANE * (g + 1)]) for g in range(KV_LORA // LANE)]
        for s, kc in zip(ss, kcs):
            p = jnp.exp2(s - m)
            den = den + jnp.sum(p, axis=-1, keepdims=True)
            pb = p.astype(BF16)
            accs = [a + _dot(pb, kc[:, LANE * g:LANE * (g + 1)]) for g, a in enumerate(accs)]
        o = jnp.concatenate(accs, axis=1) * (1.0 / den)
        for h in range(MLA_HEADS):
            o_ref[h, ts * i:ts * (i + 1), :] = o[ts * h:ts * (h + 1)]


def _mla_sample(page_table, qx, kn, pool_ckv, pool_kpet, layer, ts):
    bd, n_pages = page_table.shape
    past = n_pages * PAGE_SIZE
    nbs = MLA_SAMPLE_BATCHES
    step_rows = nbs * ts
    assert step_rows % SUBLANE == 0
    grid_spec = pltpu.PrefetchScalarGridSpec(
        num_scalar_prefetch=1,
        grid=(bd // nbs,),
        in_specs=[pl.BlockSpec((MLA_HEADS, step_rows, KC_W), lambda g, pt: (0, g, 0)),
                  pl.BlockSpec((step_rows, KC_W), lambda g, pt: (g, 0)),
                  pl.BlockSpec(memory_space=pl.ANY), pl.BlockSpec(memory_space=pl.ANY)],
        out_specs=pl.BlockSpec((MLA_HEADS, step_rows, KV_LORA), lambda g, pt: (0, g, 0)),
        scratch_shapes=[pltpu.VMEM((2, nbs, past, KV_LORA), F32), pltpu.VMEM((2, nbs, ROPE_DIM, past), F32),
                        pltpu.VMEM((nbs, MLA_HEADS * ts, KC_W), F32), pltpu.SemaphoreType.DMA((2, 2))],
    )
    return pl.pallas_call(
        functools.partial(_mla_sample_kernel, ts=ts, n_pages=n_pages, layer=layer,
                          chunk=min(MLA_SAMPLE_CHUNK, past)),
        grid_spec=grid_spec,
        out_shape=jax.ShapeDtypeStruct((MLA_HEADS, bd * ts, KV_LORA), F32),
        compiler_params=_cp(("arbitrary",), vmem=MLA_SAMPLE_VMEM),
        name="mla_sample",
    )(page_table, qx, kn, pool_ckv, pool_kpet)


def _odd_sample_tail_kernel(ci_ref, st_ref, cw_ref, ol_ref, wuv_ref, o_ref, z_ref, *, ts):
    cw = cw_ref[...]
    z = ci_ref[:, C_WIDTH:2 * C_WIDTH].astype(F32) * ci_ref[:, 2 * C_WIDTH:].astype(F32)
    z_ref[...] = z
    rows = z.shape[0]
    t = _token_col(rows, ts)
    st = st_ref[...]
    back2 = jnp.where(t < 2, st, pltpu.roll(z, 2, 0))
    back1 = jnp.where(t == 0, pltpu.roll(st, rows - 1, 0), pltpu.roll(z, 1, 0))
    y = cw[0:1] * back2 + cw[1:2] * back1 + cw[2:3] * z
    o_ref[:, :C_WIDTH] = (ci_ref[:, :C_WIDTH].astype(F32) * y).astype(BF16)
    for m in range(MLA_HEADS // 2):
        o = (_dot(ol_ref[2 * m].astype(BF16), wuv_ref[2 * m])
             + _dot(ol_ref[2 * m + 1].astype(BF16), wuv_ref[2 * m + 1]))
        o_ref[:, C_WIDTH + LANE * m:C_WIDTH + LANE * (m + 1)] = o.astype(BF16)


def _odd_sample_tail(ci, state_rows, conv_w, o_lat, wuv_ext, n_prompt_rows, ts):
    rows = state_rows.shape[0]
    off = n_prompt_rows // rows
    n_out = C_WIDTH + MLA_HEADS * V_DIM
    return pl.pallas_call(
        functools.partial(_odd_sample_tail_kernel, ts=ts),
        grid=(1,),
        in_specs=[pl.BlockSpec((rows, 3 * C_WIDTH), lambda i: (off, 0)), _const_spec(state_rows.shape),
                  _const_spec(conv_w.shape), _const_spec(o_lat.shape), _const_spec(wuv_ext.shape)],
        out_specs=[pl.BlockSpec((rows, n_out), lambda i: (0, 0)), pl.BlockSpec((rows, C_WIDTH), lambda i: (0, 0))],
        out_shape=[jax.ShapeDtypeStruct((rows, n_out), BF16), jax.ShapeDtypeStruct((rows, C_WIDTH), F32)],
        compiler_params=_cp(("arbitrary",)),
        name="odd_sample_tail",
    )(ci, state_rows, conv_w, o_lat, wuv_ext)


def _prep_odd_weights(w_in, w_uq, w_uk, w_uv):
    d, n = w_in.shape
    n_pad = -n % LANE
    w_in_p = jnp.pad(w_in, ((0, 0), (0, n_pad))).astype(BF16)
    per = NOPE_DIM + ROPE_DIM
    wq = w_uq.reshape(Q_LORA, MLA_HEADS, per)
    w_uq_r = jnp.concatenate([wq[:, :, :NOPE_DIM].reshape(Q_LORA, -1), wq[:, :, NOPE_DIM:].reshape(Q_LORA, -1)],
                             axis=1).astype(BF16)
    ukt = jnp.transpose(w_uk, (1, 2, 0))
    uvt = jnp.transpose(w_uv, (1, 0, 2))
    wuk_ext, wuv_ext = [], []
    for h in range(MLA_HEADS):
        lo = (h % 2) * NOPE_DIM
        wuk_ext.append(jnp.pad(ukt[h], ((lo, LANE - NOPE_DIM - lo), (0, 0))))
        wuv_ext.append(jnp.pad(uvt[h], ((0, 0), (lo, LANE - V_DIM - lo))))
    return w_in_p, w_uq_r, jnp.stack(wuk_ext).astype(BF16), jnp.stack(wuv_ext).astype(BF16)


def kernel(x_prompt, x_sample, cache_swa_k, cache_swa_v, state_conv, cache_mla_ckv, cache_mla_kpe, page_table,
           rel_bias, ffn1_norm, ffn1_w_gate, ffn1_w_up, ffn1_w_down, mix_norm, ffn2_norm, ffn2_w_gate,
           ffn2_w_up, ffn2_w_down, even_w_in, even_w_out, attn_sinks, sgu_ln_g, sgu_ln_b, sgu_w, sgu_b,
           odd_w_in, odd_w_out, conv_w, mla_q_norm, mla_w_uq, mla_kv_norm, mla_w_uk, mla_w_uv, final_norm):
    nb, s, d = x_prompt.shape
    bd, ts, _ = x_sample.shape
    depth = ffn1_norm.shape[0]
    n_p, n_s = nb * s, bd * ts
    past = page_table.shape[1] * PAGE_SIZE
    wb = cache_swa_k.shape[2]
    assert n_s == TM and n_p % TM == 0 and s % TM == 0, "sample group must fill exactly one token tile"
    assert SUBLANE % ts == 0 and bd % BB_EVEN == 0 and wb == WINDOW
    assert CONV_WIDTH == 3 and ts >= CONV_WIDTH - 1 and bd % MLA_SAMPLE_BATCHES == 0
    assert s % KB_ODD == 0 and s % TQ_EVEN == 0 and KB_ODD % TQ_ODD == 0
    n_pt = n_p // TM

    x = (x_prompt.reshape(n_p, d), x_sample.reshape(n_s, d))
    bias_p, bias_s = _bias_tables(rel_bias, ts)
    cos_np, sin_np = _rope_tables_np(s, past, ts, bd)
    cos_t, sin_t = jnp.asarray(cos_np), jnp.asarray(sin_np)

    w1 = tuple(w.astype(BF16) for w in (ffn1_w_gate, ffn1_w_up, ffn1_w_down))
    w2 = tuple(w.astype(BF16) for w in (ffn2_w_gate, ffn2_w_up, ffn2_w_down))
    w_out_even, w_out_odd = even_w_out.astype(BF16), odd_w_out.astype(BF16)
    pool_kpet = jnp.swapaxes(cache_mla_kpe, 2, 3)

    outs = {k: [] for k in ("kp", "vp", "ks", "vs", "sv", "cp", "cs", "ckp", "kpp", "cks", "kps")}
    for l in range(depth):
        x = _ffn(x, ffn1_norm[l], *w1, l, n_pt)
        if l % 2 == 0:
            e = l // 2
            q, kv, ug, kv_p = _inproj_even(x, mix_norm[l], even_w_in[e].astype(BF16), n_pt, s)
            mix_p = _mix_even_prompt(q, kv, ug, bias_p, attn_sinks[e], sgu_ln_g[e], sgu_ln_b[e], sgu_w[e],
                                     sgu_b[e], nb, s)
            a_s, new_k, new_v = _mix_even_sample(q[n_p:].astype(F32), kv[n_p:],
                                                 cache_swa_k[e].reshape(bd, wb, A_KV_W),
                                                 cache_swa_v[e].reshape(bd, wb, A_KV_W), bias_s, attn_sinks[e], ts)
            sg_s, v_rows = _sgu_sample(ug, sgu_ln_g[e], sgu_ln_b[e], sgu_w[e], sgu_b[e], n_p, ts, bd)
            mix_s = jnp.concatenate([a_s.astype(BF16), sg_s], axis=1)
            w_out = (w_out_even, e)
            outs["kp"].append(kv_p[..., :A_KV_W].reshape(nb, wb, A_KV_HEADS, HEAD_DIM))
            outs["vp"].append(kv_p[..., A_KV_W:].reshape(nb, wb, A_KV_HEADS, HEAD_DIM))
            outs["ks"].append(new_k.reshape(bd, wb, A_KV_HEADS, HEAD_DIM))
            outs["vs"].append(new_v.reshape(bd, wb, A_KV_HEADS, HEAD_DIM))
            outs["sv"].append(v_rows.reshape(bd, ts, SG_WIDTH))
        else:
            j = l // 2
            w_in_p, w_uq_r, wuk_ext, wuv_ext = _prep_odd_weights(odd_w_in[j], mla_w_uq[j], mla_w_uk[j], mla_w_uv[j])
            ci, qn, qp, kc, ckv_p, kpe_p, ckv_s, kpe_s = _inproj_odd(
                x, mix_norm[l], w_in_p, mla_q_norm[j], w_uq_r, mla_kv_norm[j], cos_t, sin_t, n_pt, s)
            mix_p, cst_p = _mix_odd_prompt(ci, qn, qp, kc, wuk_ext, wuv_ext, conv_w[j], nb, s)
            qx = _qx_sample(qn, qp, wuk_ext, n_p, n_s).reshape(MLA_HEADS, n_s, KC_W)
            o_lat = _mla_sample(page_table, qx, kc[n_p:].astype(F32), cache_mla_ckv, pool_kpet, j, ts)
            n_st = CONV_WIDTH - 1
            state_rows = jnp.pad(state_conv[j], ((0, 0), (0, ts - n_st), (0, 0))).reshape(n_s, C_WIDTH)
            mix_s, z_s = _odd_sample_tail(ci, state_rows, conv_w[j], o_lat, wuv_ext, n_p, ts)
            w_out = (w_out_odd, j)
            outs["cp"].append(cst_p[:, SUBLANE - n_st:])
            outs["cs"].append(z_s.reshape(bd, ts, C_WIDTH)[:, ts - n_st:])
            outs["ckp"].append(ckv_p.reshape(nb, s, KV_LORA))
            outs["kpp"].append(kpe_p.reshape(nb, s, ROPE_DIM))
            outs["cks"].append(ckv_s.reshape(bd, ts, KV_LORA))
            outs["kps"].append(kpe_s.reshape(bd, ts, ROPE_DIM))
        x = _ffn(x, ffn2_norm[l], *w2, l, n_pt, mix=(mix_p, mix_s) + w_out,
                 final_g=final_norm if l == depth - 1 else None)
    y_prompt = x[0].reshape(nb, s, d)
    y_sample = x[1].reshape(bd, ts, d)
    st = lambda k: jnp.stack(outs[k])
    return (y_prompt, y_sample, st("kp"), st("vp"), st("ks"), st("vs"), st("sv"), st("cp"), st("cs"),
            st("ckp"), st("kpp"), st("cks"), st("kps"))
```

```python
import functools
import math

import numpy as np
import jax
import jax.numpy as jnp
from jax import lax
from jax.experimental import pallas as pl
from jax.experimental.pallas import tpu as pltpu

F32 = jnp.float32
BF16 = jnp.bfloat16

HEAD_DIM = 64
A_HEADS = 8
A_KV_HEADS = 2
A_Q_W = A_HEADS * HEAD_DIM
A_KV_W = A_KV_HEADS * HEAD_DIM
WINDOW = 128
N_BUCKETS = 32
MAX_DISTANCE = 128
SG_GROUPS = 4
SG_GROUP_DIM = 128
SG_WIDTH = SG_GROUPS * SG_GROUP_DIM
CHUNK = 128
C_WIDTH = 512
CONV_WIDTH = 3
MLA_HEADS = 8
NOPE_DIM = 64
ROPE_DIM = 32
V_DIM = 64
Q_LORA = 768
KV_LORA = 256
ROPE_THETA = 10000.0
PAGE_SIZE = 128
EPS = 1e-6
NEG = -1e30

LANE = 128
SUBLANE = 8
CONV_HALO = 2 * SUBLANE
KC_W = KV_LORA + LANE
TM = 512
FC = 512
TQ_EVEN = 512
TQ_ODD = 256
KB_ODD = 512
CHAINS_ODD = 2
BB_EVEN = 16
MLA_SAMPLE_CHUNK = 2048
MLA_SAMPLE_BATCHES = 2
MLA_SAMPLE_VMEM = 60 * 1024 * 1024
VMEM_LIMIT = 56 * 1024 * 1024


def _cp(sem, vmem=VMEM_LIMIT):
    return pltpu.CompilerParams(dimension_semantics=sem, vmem_limit_bytes=vmem)


def _dot(a, b):
    return jnp.dot(a, b, preferred_element_type=F32)


def _dot_nt(a, b):
    return lax.dot_general(a, b, (((1,), (1,)), ((), ())), preferred_element_type=F32)


def _rms(x, g):
    ms = jnp.mean(x * x, axis=-1, keepdims=True)
    return x * lax.rsqrt(ms + EPS) * g


def _gelu(x):
    return 0.5 * x * (1.0 + lax.erf(x * np.float32(math.sqrt(0.5))))


def _const_spec(shape):
    n = len(shape)
    return pl.BlockSpec(shape, lambda *_: (0,) * n, pipeline_mode=pl.Buffered(1))


def _smem_spec():
    return pl.BlockSpec(memory_space=pltpu.SMEM)


def _ffn_chunks(f):
    chunks = [(c, FC) for c in range(0, f - f % FC, FC)]
    if f % FC:
        chunks.append((f - f % FC, f % FC))
    return chunks


def _ffn_kernel(*refs, split_in, has_mix, has_final, n_prompt_tiles, chunks):
    it = iter(refs)
    x_ref = next(it)
    if split_in:
        xs_ref = next(it)
    if has_mix:
        mixp_ref, mixs_ref, wout_ref = next(it), next(it), next(it)
    g_ref, wg_ref, wu_ref, wd_ref = next(it), next(it), next(it), next(it)
    if has_final:
        fg_ref = next(it)
    o_ref = next(it)
    if has_final:
        os_ref = next(it)
    acc_ref = next(it)

    is_sample = pl.program_id(0) >= n_prompt_tiles
    x = x_ref[...]
    if split_in:
        x = jnp.where(is_sample, xs_ref[...], x)
    if has_mix:
        mix = jnp.where(is_sample, mixs_ref[...], mixp_ref[...])
        x = x + _dot(mix, wout_ref[...])
    h = _rms(x, g_ref[...]).astype(BF16)
    for k, (c0, cw) in enumerate(chunks):
        g = _dot(h, wg_ref[:, c0:c0 + cw])
        u = _dot(h, wu_ref[:, c0:c0 + cw])
        a = (g * jax.nn.sigmoid(g) * u).astype(BF16)
        d = _dot(a, wd_ref[c0:c0 + cw, :])
        if k == 0:
            acc_ref[...] = d
        else:
            acc_ref[...] += d
    y = x + 0.5 * acc_ref[...]
    if has_final:
        y = _rms(y, fg_ref[...])

        @pl.when(jnp.logical_not(is_sample))
        def _():
            o_ref[...] = y

        @pl.when(is_sample)
        def _():
            os_ref[...] = y
    else:
        o_ref[...] = y


def _layer_spec(shape, layer):
    n = len(shape)
    return pl.BlockSpec((None,) + tuple(shape), lambda *_: (layer,) + (0,) * n, pipeline_mode=pl.Buffered(1))


def _ffn(x, norm_g, w_gate, w_up, w_down, layer, n_prompt_tiles, mix=None, final_g=None):
    n_tiles = n_prompt_tiles + 1
    last_p = n_prompt_tiles - 1
    row = lambda i: (i, 0)
    prompt_row = lambda i: (jnp.minimum(i, last_p), 0)
    sample_row = lambda i: (0, 0)
    split_in = isinstance(x, tuple)
    d = x[0].shape[1] if split_in else x.shape[1]
    t = n_tiles * TM
    f = w_down.shape[1]
    if split_in:
        args, specs = list(x), [pl.BlockSpec((TM, d), prompt_row), pl.BlockSpec((TM, d), sample_row)]
    else:
        args, specs = [x], [pl.BlockSpec((TM, d), row)]
    if mix is not None:
        mix_p, mix_s, w_out, e = mix
        dm = w_out.shape[1]
        args += [mix_p, mix_s, w_out]
        specs += [pl.BlockSpec((TM, dm), prompt_row), pl.BlockSpec((TM, dm), sample_row),
                  _layer_spec((dm, d), e)]
    args += [norm_g.reshape(1, d), w_gate, w_up, w_down]
    specs += [_const_spec((1, d)), _layer_spec((d, f), layer), _layer_spec((d, f), layer),
              _layer_spec((f, d), layer)]
    if final_g is not None:
        args.append(final_g.reshape(1, d))
        specs.append(_const_spec((1, d)))
    body = functools.partial(_ffn_kernel, split_in=split_in, has_mix=mix is not None,
                             has_final=final_g is not None, n_prompt_tiles=n_prompt_tiles, chunks=_ffn_chunks(f))
    if final_g is not None:
        out_specs = [pl.BlockSpec((TM, d), prompt_row), pl.BlockSpec((TM, d), sample_row)]
        out_shape = [jax.ShapeDtypeStruct((n_prompt_tiles * TM, d), F32), jax.ShapeDtypeStruct((TM, d), F32)]
    else:
        out_specs = pl.BlockSpec((TM, d), row)
        out_shape = jax.ShapeDtypeStruct((t, d), F32)
    return pl.pallas_call(
        body,
        grid=(n_tiles,),
        in_specs=specs,
        out_specs=out_specs,
        out_shape=out_shape,
        scratch_shapes=[pltpu.VMEM((TM, d), F32)],
        compiler_params=_cp(("arbitrary",)),
        name="ffn",
    )(*args)


def _t5_bucket_np(dist):
    n = np.maximum(dist, 0)
    max_exact = N_BUCKETS // 2
    nf = np.maximum(n, 1).astype(np.float32)
    large = max_exact + (np.log(nf / np.float32(max_exact)) / np.float32(math.log(MAX_DISTANCE / max_exact))
                         * np.float32(N_BUCKETS - max_exact)).astype(np.int32)
    return np.where(n < max_exact, n, np.minimum(large, N_BUCKETS - 1)).astype(np.int32)


def _bias_tables_np(ts):
    q = np.arange(WINDOW)[:, None]
    j = np.arange(2 * WINDOW)[None, :]
    bucket_p = _t5_bucket_np(q - j + WINDOW)
    nb = SUBLANE // ts
    cols = nb * WINDOW + 2 * SUBLANE
    bucket_s = -np.ones((SUBLANE, cols), np.int32)
    for r in range(SUBLANE):
        bi, t = divmod(r, ts)
        for c in range(cols):
            if c < nb * WINDOW:
                bj, jj = divmod(c, WINDOW)
                dist = t - (jj - WINDOW)
            elif c < nb * WINDOW + SUBLANE:
                bj, tt = divmod(c - nb * WINDOW, ts)
                dist = t - tt
            else:
                continue
            if bj == bi and 0 <= dist <= WINDOW:
                bucket_s[r, c] = _t5_bucket_np(np.array(dist))
    return bucket_p, bucket_s


def _bias_kernel(rb_ref, bp_ref, bs_ref, op_ref, os_ref):
    h = pl.program_id(0)
    bp = bp_ref[...]
    bs = bs_ref[...]
    accp = jnp.zeros(bp.shape, F32)
    accs = jnp.full(bs.shape, NEG, F32)
    for k in range(N_BUCKETS):
        val = rb_ref[k, h]
        accp = jnp.where(bp == k, val, accp)
        accs = jnp.where(bs == k, val, accs)
    op_ref[0] = accp
    os_ref[0] = accs


def _bias_tables(rel_bias, ts):
    bucket_p, bucket_s = _bias_tables_np(ts)
    return pl.pallas_call(
        _bias_kernel,
        grid=(A_HEADS,),
        in_specs=[_smem_spec(),
                  pl.BlockSpec(bucket_p.shape, lambda h: (0, 0)),
                  pl.BlockSpec(bucket_s.shape, lambda h: (0, 0))],
        out_specs=[pl.BlockSpec((1,) + bucket_p.shape, lambda h: (h, 0, 0)),
                   pl.BlockSpec((1,) + bucket_s.shape, lambda h: (h, 0, 0))],
        out_shape=[jax.ShapeDtypeStruct((A_HEADS,) + bucket_p.shape, F32),
                   jax.ShapeDtypeStruct((A_HEADS,) + bucket_s.shape, F32)],
        compiler_params=_cp(("arbitrary",)),
        name="t5_bias",
    )(rel_bias, jnp.asarray(bucket_p), jnp.asarray(bucket_s))


def _inproj_even_kernel(x_ref, g_ref, w_ref, q_ref, kv_ref, ug_ref, last_ref, *, tiles_per_seq, n_prompt_tiles):
    h = _rms(x_ref[...], g_ref[...]).astype(BF16)
    p = _dot(h, w_ref[...])
    q_ref[...] = (p[:, :A_Q_W] * np.float32(HEAD_DIM ** -0.5)).astype(BF16)
    kv = p[:, A_Q_W:A_Q_W + 2 * A_KV_W]
    kv_ref[...] = kv
    ug_ref[...] = p[:, A_Q_W + 2 * A_KV_W:].astype(BF16)
    i = pl.program_id(0)

    @pl.when(((i + 1) % tiles_per_seq == 0) & (i < n_prompt_tiles))
    def _():
        last_ref[...] = kv[TM - WINDOW:]


def _inproj_even(x, norm_g, w_in, n_prompt_tiles, s):
    t, d = x.shape
    n = w_in.shape[1]
    row = lambda i: (i, 0)
    per_seq = s // TM
    nb = n_prompt_tiles // per_seq
    return pl.pallas_call(
        functools.partial(_inproj_even_kernel, tiles_per_seq=per_seq, n_prompt_tiles=n_prompt_tiles),
        grid=(t // TM,),
        in_specs=[pl.BlockSpec((TM, d), row), _const_spec((1, d)), _const_spec((d, n))],
        out_specs=[pl.BlockSpec((TM, A_Q_W), row), pl.BlockSpec((TM, 2 * A_KV_W), row),
                   pl.BlockSpec((TM, 2 * SG_WIDTH), row),
                   pl.BlockSpec((None, WINDOW, 2 * A_KV_W), lambda i: (jnp.minimum(i // per_seq, nb - 1), 0, 0))],
        out_shape=[jax.ShapeDtypeStruct((t, A_Q_W), BF16), jax.ShapeDtypeStruct((t, 2 * A_KV_W), F32),
                   jax.ShapeDtypeStruct((t, 2 * SG_WIDTH), BF16),
                   jax.ShapeDtypeStruct((nb, WINDOW, 2 * A_KV_W), F32)],
        compiler_params=_cp(("arbitrary",)),
        name="inproj_even",
    )(x, norm_g.reshape(1, d), w_in)


def _head_variants(x):
    lo = lax.broadcasted_iota(jnp.int32, x.shape, 1) < HEAD_DIM
    xr = pltpu.roll(x, HEAD_DIM, 1)
    z = jnp.zeros_like(x)
    return [[jnp.where(lo, x, z).astype(BF16), jnp.where(lo, z, xr).astype(BF16)],
            [jnp.where(lo, xr, z).astype(BF16), jnp.where(lo, z, x).astype(BF16)]]


def _sink_softmax_pv(s, sink, v):
    mx = jnp.maximum(jnp.max(s, axis=-1, keepdims=True), sink)
    p = jnp.exp(s - mx)
    den = jnp.sum(p, axis=-1, keepdims=True) + jnp.exp(sink - mx)
    return _dot(p.astype(BF16), v) * (1.0 / den)


def _layer_norm(x, g, b):
    xc = x - jnp.mean(x, axis=-1, keepdims=True)
    y = xc * lax.rsqrt(jnp.mean(xc * xc, axis=-1, keepdims=True) + EPS)
    return y * g + b


def _even_prompt_kernel(q_ref, kv_ref, kvp_ref, ug_ref, bias_ref, sink_ref, lng_ref, lnb_ref, ws_ref,
                        bst_ref, o_ref, *, tq):
    nblk = tq // WINDOW
    step = pl.program_id(1)
    kv_all = jnp.concatenate([kvp_ref[...], kv_ref[...]], axis=0)
    kvar = _head_variants(kv_all[:, :A_KV_W])
    vvar = _head_variants(kv_all[:, A_KV_W:])
    qi = lax.broadcasted_iota(jnp.int32, (WINDOW, 2 * WINDOW), 0)
    kj = lax.broadcasted_iota(jnp.int32, (WINDOW, 2 * WINDOW), 1)
    dist = qi - kj + WINDOW
    band = (dist >= 0) & (dist <= WINDOW)
    cur = kj >= WINDOW
    ri = lax.broadcasted_iota(jnp.int32, (CHUNK, CHUNK), 0)
    ci = lax.broadcasted_iota(jnp.int32, (CHUNK, CHUNK), 1)
    wt = [jnp.where(ri >= ci, ws_ref[g], 0.0).astype(BF16) for g in range(SG_GROUPS)]
    lng, lnb = lng_ref[...], lnb_ref[...]
    for n in range(nblk):
        r0 = n * WINDOW
        has_prev = (step * nblk + n) > 0
        mask = band & (cur | has_prev)
        for m in range(A_HEADS // 2):
            kvh = (2 * m) // (A_HEADS // A_KV_HEADS)
            qp = q_ref[r0:r0 + WINDOW, LANE * m:LANE * (m + 1)]
            o = None
            for half in range(2):
                hd = 2 * m + half
                s = _dot_nt(qp, kvar[kvh][half][r0:r0 + 2 * WINDOW])
                s = jnp.where(mask, s + bias_ref[hd], NEG)
                oh = _sink_softmax_pv(s, sink_ref[hd], vvar[kvh][half][r0:r0 + 2 * WINDOW])
                o = oh if o is None else o + oh
            o_ref[r0:r0 + WINDOW, LANE * m:LANE * (m + 1)] = o.astype(BF16)
        u = _gelu(ug_ref[r0:r0 + WINDOW, :SG_WIDTH].astype(F32))
        v = _layer_norm(_gelu(ug_ref[r0:r0 + WINDOW, SG_WIDTH:].astype(F32)), lng, lnb)
        for g in range(SG_GROUPS):
            sl = slice(SG_GROUP_DIM * g, SG_GROUP_DIM * (g + 1))
            y = _dot(wt[g], v[:, sl].astype(BF16)) + bst_ref[:, g:g + 1]
            o_ref[r0:r0 + WINDOW, A_Q_W + SG_GROUP_DIM * g:A_Q_W + SG_GROUP_DIM * (g + 1)] = (
                u[:, sl] * y).astype(BF16)


def _mix_even_prompt(q, kv, ug, bias_p, sinks, ln_g, ln_b, w_s, b_s, nb, s):
    tq = TQ_EVEN
    nq = s // tq
    blk = lambda b, i: (b * nq + i, 0)
    prev = lambda b, i: (jnp.maximum((b * nq + i) * (tq // WINDOW) - 1, 0), 0)
    return pl.pallas_call(
        functools.partial(_even_prompt_kernel, tq=tq),
        grid=(nb, nq),
        in_specs=[pl.BlockSpec((tq, A_Q_W), blk), pl.BlockSpec((tq, 2 * A_KV_W), blk),
                  pl.BlockSpec((WINDOW, 2 * A_KV_W), prev), pl.BlockSpec((tq, 2 * SG_WIDTH), blk),
                  _const_spec(bias_p.shape), _smem_spec(), _const_spec((1, SG_WIDTH)),
                  _const_spec((1, SG_WIDTH)), _const_spec(w_s.shape), _const_spec((CHUNK, SG_GROUPS))],
        out_specs=pl.BlockSpec((tq, A_Q_W + SG_WIDTH), blk),
        out_shape=jax.ShapeDtypeStruct((nb * s, A_Q_W + SG_WIDTH), BF16),
        compiler_params=_cp(("arbitrary", "arbitrary")),
        name="mix_even_prompt",
    )(q, kv, kv, ug, bias_p, sinks, ln_g.reshape(1, -1), ln_b.reshape(1, -1), w_s, b_s.T)


def _even_sample_kernel(q_ref, kvn_ref, ck_ref, cv_ref, bias_ref, sink_ref, a_ref, nk_ref, nv_ref, *, ts):
    bb = ck_ref.shape[0]
    gr = SUBLANE
    gpb = gr // ts
    wb = ck_ref.shape[1]
    n_cache = gpb * wb
    hpk = A_HEADS // A_KV_HEADS
    zpad = jnp.zeros((gr, A_KV_W), F32)

    def lo_half(x, kvh):
        lo = lax.broadcasted_iota(jnp.int32, x.shape, 1) < HEAD_DIM
        return jnp.where(lo, x if kvh == 0 else pltpu.roll(x, HEAD_DIM, 1), 0.0).astype(BF16)

    hrow = lax.broadcasted_iota(jnp.int32, (gr * hpk, 1), 0) // gr
    bias_c, bias_n, sink = [], [], []
    for kvh in range(A_KV_HEADS):
        heads = [hpk * kvh + i for i in range(hpk)]
        bias_c.append(jnp.concatenate([bias_ref[h][:, :n_cache] for h in heads], axis=0))
        bias_n.append(jnp.concatenate([bias_ref[h][:, n_cache:] for h in heads], axis=0))
        sk = jnp.full((gr * hpk, 1), sink_ref[heads[0]], F32)
        for i in range(1, hpk):
            sk = jnp.where(hrow == i, sink_ref[heads[i]], sk)
        sink.append(sk)
    for g in range(bb // gpb):
        r0 = gr * g
        qg = q_ref[r0:r0 + gr, :]
        kn = kvn_ref[r0:r0 + gr, :A_KV_W]
        vn = kvn_ref[r0:r0 + gr, A_KV_W:]
        kc = jnp.concatenate([ck_ref[gpb * g + i] for i in range(gpb)], axis=0)
        vc = jnp.concatenate([cv_ref[gpb * g + i] for i in range(gpb)], axis=0)
        knp = jnp.concatenate([kn, zpad], axis=0)
        vnp = jnp.concatenate([vn, zpad], axis=0)
        for kvh in range(A_KV_HEADS):
            c0 = 2 * LANE * kvh
            qs = jnp.concatenate([lo_half(qg[:, c0 + LANE * (i // 2):c0 + LANE * (i // 2 + 1)], i % 2)
                                  for i in range(hpk)], axis=0)
            sc = _dot_nt(qs, lo_half(kc, kvh))
            sn = _dot_nt(qs, lo_half(knp, kvh))
            sc = jnp.where(bias_c[kvh] > 0.5 * NEG, sc + bias_c[kvh], NEG)
            sn = jnp.where(bias_n[kvh] > 0.5 * NEG, sn + bias_n[kvh], NEG)
            mx = jnp.maximum(jnp.maximum(jnp.max(sc, axis=-1, keepdims=True), jnp.max(sn, axis=-1, keepdims=True)),
                             sink[kvh])
            pc = jnp.exp(sc - mx)
            pn = jnp.exp(sn - mx)
            den = (jnp.sum(pc, axis=-1, keepdims=True) + jnp.sum(pn, axis=-1, keepdims=True)
                   + jnp.exp(sink[kvh] - mx))
            o = (_dot(pc.astype(BF16), lo_half(vc, kvh)) + _dot(pn.astype(BF16), lo_half(vnp, kvh))) * (1.0 / den)
            for i in range(hpk // 2):
                even, odd = o[2 * gr * i:2 * gr * i + gr], o[2 * gr * i + gr:2 * gr * (i + 1)]
                a_ref[r0:r0 + gr, c0 + LANE * i:c0 + LANE * (i + 1)] = even + pltpu.roll(odd, HEAD_DIM, 1)
        for i in range(gpb):
            b = gpb * g + i
            nk_ref[b, 0:wb - ts, :] = ck_ref[b, ts:wb, :]
            nk_ref[b, wb - ts:wb, :] = kn[ts * i:ts * (i + 1)]
            nv_ref[b, 0:wb - ts, :] = cv_ref[b, ts:wb, :]
            nv_ref[b, wb - ts:wb, :] = vn[ts * i:ts * (i + 1)]


def _mix_even_sample(q_b, kvn_b, cache_k, cache_v, bias_s, sinks, ts):
    bd, wb, _ = cache_k.shape
    bb = BB_EVEN
    rows = bb * ts
    blk2 = lambda i: (i, 0)
    blk3 = lambda i: (i, 0, 0)
    return pl.pallas_call(
        functools.partial(_even_sample_kernel, ts=ts),
        grid=(bd // bb,),
        in_specs=[pl.BlockSpec((rows, A_Q_W), blk2), pl.BlockSpec((rows, 2 * A_KV_W), blk2),
                  pl.BlockSpec((bb, wb, A_KV_W), blk3), pl.BlockSpec((bb, wb, A_KV_W), blk3),
                  _const_spec(bias_s.shape), _smem_spec()],
        out_specs=[pl.BlockSpec((rows, A_Q_W), blk2), pl.BlockSpec((bb, wb, A_KV_W), blk3),
                   pl.BlockSpec((bb, wb, A_KV_W), blk3)],
        out_shape=[jax.ShapeDtypeStruct((bd * ts, A_Q_W), F32),
                   jax.ShapeDtypeStruct(cache_k.shape, F32), jax.ShapeDtypeStruct(cache_v.shape, F32)],
        compiler_params=_cp(("arbitrary",)),
        name="mix_even_sample",
    )(q_b, kvn_b, cache_k, cache_v, bias_s, sinks)


def _token_col(rows, ts):
    return lax.broadcasted_iota(jnp.int32, (rows, 1), 0) % ts


def _sgu_sample_kernel(ug_ref, lng_ref, lnb_ref, w_ref, b_ref, sg_ref, v_ref, *, ts):
    u = _gelu(ug_ref[:, :SG_WIDTH].astype(F32))
    v = _layer_norm(_gelu(ug_ref[:, SG_WIDTH:].astype(F32)), lng_ref[...], lnb_ref[...])
    v_ref[...] = v
    rows = v.shape[0]
    t = _token_col(rows, ts)
    back = [v] + [pltpu.roll(v, k, 0) for k in range(1, ts)]
    for g in range(SG_GROUPS):
        sl = slice(SG_GROUP_DIM * g, SG_GROUP_DIM * (g + 1))
        y = jnp.zeros((rows, 1), F32)
        for i in range(ts):
            y = jnp.where(t == i, b_ref[g, i], y)
        for k in range(ts):
            wk = jnp.zeros((rows, 1), F32)
            for i in range(k, ts):
                wk = jnp.where(t == i, w_ref[g, i * ts + i - k], wk)
            y = y + wk * back[k][:, sl]
        sg_ref[:, sl] = (u[:, sl] * y).astype(BF16)


def _sgu_sample(ug, ln_g, ln_b, w_s, b_s, n_prompt_rows, ts, bd):
    rows = ts * bd
    off = n_prompt_rows // rows
    w_small = w_s[:, :ts, :ts].reshape(SG_GROUPS, ts * ts)
    b_small = b_s[:, :ts]
    return pl.pallas_call(
        functools.partial(_sgu_sample_kernel, ts=ts),
        grid=(1,),
        in_specs=[pl.BlockSpec((rows, 2 * SG_WIDTH), lambda i: (off, 0)), _const_spec((1, SG_WIDTH)),
                  _const_spec((1, SG_WIDTH)), _smem_spec(), _smem_spec()],
        out_specs=[pl.BlockSpec((rows, SG_WIDTH), lambda i: (0, 0)),
                   pl.BlockSpec((rows, SG_WIDTH), lambda i: (0, 0))],
        out_shape=[jax.ShapeDtypeStruct((rows, SG_WIDTH), BF16), jax.ShapeDtypeStruct((rows, SG_WIDTH), F32)],
        compiler_params=_cp(("arbitrary",)),
        name="sgu_sample",
    )(ug, ln_g.reshape(1, -1), ln_b.reshape(1, -1), w_small, b_small)


def _rope_tables_np(s, past, ts, bd):
    half = ROPE_DIM // 2
    inv = np.exp(-math.log(ROPE_THETA) * np.arange(half, dtype=np.float32) / np.float32(half)).astype(np.float32)
    pos = np.concatenate([np.arange(s), past + np.tile(np.arange(ts), bd)]).astype(np.float32)
    ang = (pos[:, None] * inv[None, :]).astype(np.float32).astype(np.float64)
    cos = np.cos(ang)
    sin = np.sin(ang)
    cos_t = np.tile(np.concatenate([cos, cos], axis=1), (1, MLA_HEADS)).astype(np.float32)
    sin_t = np.tile(np.concatenate([-sin, sin], axis=1), (1, MLA_HEADS)).astype(np.float32)
    return cos_t, sin_t


def _rope(x, cos_t, sin_t):
    n = x.shape[1]
    half = ROPE_DIM // 2
    first = (lax.broadcasted_iota(jnp.int32, x.shape, 1) % ROPE_DIM) < half
    partner = jnp.where(first, pltpu.roll(x, n - half, 1), pltpu.roll(x, half, 1))
    return x * cos_t + partner * sin_t


def _inproj_odd_kernel(x_ref, g_ref, w_ref, qg_ref, wuq_ref, kvg_ref, cos_ref, sin_ref,
                       ci_ref, qn_ref, qp_ref, kc_ref, ckvp_ref, kpep_ref, ckvs_ref, kpes_ref, *, n_prompt_tiles):
    h = _rms(x_ref[...], g_ref[...]).astype(BF16)
    p = _dot(h, w_ref[...])
    o_q = 3 * C_WIDTH
    o_kv = o_q + Q_LORA
    o_pe = o_kv + KV_LORA
    ci_ref[...] = p[:, :o_q].astype(BF16)
    scale = np.float32((NOPE_DIM + ROPE_DIM) ** -0.5 * math.log2(math.e))
    q = _dot(_rms(p[:, o_q:o_kv], qg_ref[...]).astype(BF16), wuq_ref[...])
    n_nope = MLA_HEADS * NOPE_DIM
    cos_t, sin_t = cos_ref[...], sin_ref[...]
    qn_ref[...] = (q[:, :n_nope] * scale).astype(BF16)
    qp_ref[...] = (_rope(q[:, n_nope:], cos_t, sin_t) * scale).astype(BF16)
    ckv = _rms(p[:, o_kv:o_pe], kvg_ref[...])
    kpe = _rope(p[:, o_pe:], cos_t[:, :LANE], sin_t[:, :LANE])
    kc_ref[:, :KV_LORA] = ckv.astype(BF16)
    kc_ref[:, KV_LORA:] = kpe.astype(BF16)
    is_sample = pl.program_id(0) >= n_prompt_tiles

    @pl.when(jnp.logical_not(is_sample))
    def _():
        ckvp_ref[...] = ckv
        kpep_ref[...] = kpe[:, :ROPE_DIM]

    @pl.when(is_sample)
    def _():
        ckvs_ref[...] = ckv
        kpes_ref[...] = kpe[:, :ROPE_DIM]


def _inproj_odd(x, norm_g, w_in, q_norm, w_uq, kv_norm, cos_t, sin_t, n_prompt_tiles, s):
    t, d = x.shape
    n = w_in.shape[1]
    row = lambda i: (i, 0)
    last_p = n_prompt_tiles - 1
    prompt_row = lambda i: (jnp.minimum(i, last_p), 0)
    sample_row = lambda i: (0, 0)
    per_seq = s // TM
    tab = lambda i: (jnp.where(i < n_prompt_tiles, i % per_seq, per_seq), 0)
    n_pe = MLA_HEADS * ROPE_DIM
    n_p = n_prompt_tiles * TM
    widths = [(3 * C_WIDTH, BF16), (MLA_HEADS * NOPE_DIM, BF16), (n_pe, BF16), (KC_W, BF16)]
    return pl.pallas_call(
        functools.partial(_inproj_odd_kernel, n_prompt_tiles=n_prompt_tiles),
        grid=(t // TM,),
        in_specs=[pl.BlockSpec((TM, d), row), _const_spec((1, d)), _const_spec((d, n)),
                  _const_spec((1, Q_LORA)), _const_spec(w_uq.shape), _const_spec((1, KV_LORA)),
                  pl.BlockSpec((TM, n_pe), tab), pl.BlockSpec((TM, n_pe), tab)],
        out_specs=[pl.BlockSpec((TM, w), row) for w, _ in widths] + [
            pl.BlockSpec((TM, KV_LORA), prompt_row), pl.BlockSpec((TM, ROPE_DIM), prompt_row),
            pl.BlockSpec((TM, KV_LORA), sample_row), pl.BlockSpec((TM, ROPE_DIM), sample_row)],
        out_shape=[jax.ShapeDtypeStruct((t, w), dt) for w, dt in widths] + [
            jax.ShapeDtypeStruct((n_p, KV_LORA), F32), jax.ShapeDtypeStruct((n_p, ROPE_DIM), F32),
            jax.ShapeDtypeStruct((TM, KV_LORA), F32), jax.ShapeDtypeStruct((TM, ROPE_DIM), F32)],
        compiler_params=_cp(("arbitrary",)),
        name="inproj_odd",
    )(x, norm_g.reshape(1, d), w_in, q_norm.reshape(1, -1), w_uq, kv_norm.reshape(1, -1), cos_t, sin_t)


def _build_qx(qn_ref, qp_ref, wuk_ref, qx_ref, rows):
    qp = qp_ref[...].astype(F32)
    keep = lax.broadcasted_iota(jnp.int32, (rows, LANE), 1) < ROPE_DIM
    n_pe = MLA_HEADS * ROPE_DIM
    for h in range(MLA_HEADS):
        pair = qn_ref[:, LANE * (h // 2):LANE * (h // 2 + 1)]
        qx_ref[h * rows:(h + 1) * rows, :KV_LORA] = _dot(pair, wuk_ref[h]).astype(qx_ref.dtype)
        pe = qp if h == 0 else pltpu.roll(qp, n_pe - ROPE_DIM * h, 1)
        qx_ref[h * rows:(h + 1) * rows, KV_LORA:] = jnp.where(keep, pe[:, :LANE], 0.0).astype(qx_ref.dtype)


def _odd_prompt_kernel(ci_ref, cih_ref, qn_ref, qp_ref, kc_ref, wuk_ref, wuv_ref, cw_ref,
                       o_ref, cst_ref, qx_ref, m_ref, l_ref, acc_ref, z_ref, *, tq, kb):
    step = pl.program_id(1)
    halo = CONV_HALO
    zh = cih_ref[:, C_WIDTH:2 * C_WIDTH].astype(F32) * cih_ref[:, 2 * C_WIDTH:].astype(F32)
    z_ref[0:halo, :] = jnp.where(step > 0, zh, 0.0)
    z = ci_ref[:, C_WIDTH:2 * C_WIDTH].astype(F32) * ci_ref[:, 2 * C_WIDTH:].astype(F32)
    z_ref[halo:halo + tq, :] = z
    cw = cw_ref[...]
    y = cw[0:1] * z_ref[halo - 2:halo - 2 + tq, :] + cw[1:2] * z_ref[halo - 1:halo - 1 + tq, :] + cw[2:3] * z
    o_ref[:, :C_WIDTH] = (ci_ref[:, :C_WIDTH].astype(F32) * y).astype(BF16)
    cst_ref[0] = z[tq - SUBLANE:tq]

    _build_qx(qn_ref, qp_ref, wuk_ref, qx_ref, tq)
    rows = MLA_HEADS * tq
    n_full = (step * tq) // kb
    hpc = MLA_HEADS // CHAINS_ODD

    def kv_step(j, width, mask_from, first):
        k0 = j * kb if isinstance(j, int) else pl.multiple_of(j * kb, kb)
        kc = kc_ref[pl.ds(k0, width), :]
        n_grp = width // LANE
        if mask_from < n_grp:
            qpos = step * tq + lax.broadcasted_iota(jnp.int32, (tq, LANE), 0)
            kpos = k0 + lax.broadcasted_iota(jnp.int32, (tq, LANE), 1)
            visible = {g: kpos + LANE * g <= qpos for g in range(mask_from, n_grp)}
        for c in range(CHAINS_ODD):
            rc = slice(c * hpc * tq, (c + 1) * hpc * tq)
            s_all = _dot_nt(qx_ref[rc, :], kc)
            ps, alphas = [], []
            for hh in range(hpc):
                r = slice((c * hpc + hh) * tq, (c * hpc + hh + 1) * tq)
                sg = [s_all[hh * tq:(hh + 1) * tq, LANE * g:LANE * (g + 1)] for g in range(n_grp)]
                sg = [jnp.where(visible[g], x, NEG) if g >= mask_from else x for g, x in enumerate(sg)]
                m_blk = jnp.max(functools.reduce(jnp.maximum, sg), axis=-1, keepdims=True)
                if first:
                    m_new = jnp.broadcast_to(m_blk, (tq, LANE))
                else:
                    m_old = m_ref[r, :]
                    m_new = jnp.maximum(m_old, m_blk)
                    alpha = jnp.exp2(m_old - m_new)
                    alphas.append(jnp.concatenate([alpha] * (KV_LORA // LANE), axis=1))
                pg = [jnp.exp2(x - m_new) for x in sg]
                psum = functools.reduce(jnp.add, pg)
                l_ref[r, :] = psum if first else alpha * l_ref[r, :] + psum
                m_ref[r, :] = m_new
                ps.append(jnp.concatenate([x.astype(BF16) for x in pg], axis=1))
            pv = _dot(jnp.concatenate(ps, axis=0), kc[:, :KV_LORA])
            acc_ref[rc, :] = pv if first else jnp.concatenate(alphas, axis=0) * acc_ref[rc, :] + pv

    @pl.when(n_full >= 2)
    def _():
        kv_step(0, kb, kb // LANE, True)

    def full_step(j, carry):
        kv_step(j, kb, kb // LANE, False)
        return carry

    lax.fori_loop(1, n_full - 1, full_step, 0)
    for i, width in enumerate(range(tq, kb + 1, tq)):
        on_diag = step % (kb // tq) == i

        @pl.when(on_diag & (n_full == 0))
        def _(width=width):
            kv_step(0, width, 0, True)

        @pl.when(on_diag & (n_full == 1))
        def _(width=width):
            kv_step(0, kb + width, kb // LANE, True)

        @pl.when(on_diag & (n_full >= 2))
        def _(width=width):
            kv_step(n_full - 1, kb + width, kb // LANE, False)

    for m in range(MLA_HEADS // 2):
        o = None
        for half in range(2):
            h = 2 * m + half
            inv_l = jnp.broadcast_to(1.0 / jnp.sum(l_ref[h * tq:(h + 1) * tq, :], axis=-1, keepdims=True), (tq, LANE))
            ol = (acc_ref[h * tq:(h + 1) * tq, :] * jnp.concatenate([inv_l] * (KV_LORA // LANE), axis=1)).astype(BF16)
            oh = _dot(ol, wuv_ref[h])
            o = oh if o is None else o + oh
        o_ref[:, C_WIDTH + LANE * m:C_WIDTH + LANE * (m + 1)] = o.astype(BF16)


def _mix_odd_prompt(ci, qn, qp, kc, wuk_ext, wuv_ext, conv_w, nb, s):
    tq, kb = TQ_ODD, KB_ODD
    nq = s // tq
    blk = lambda b, i: (b * nq + i, 0)
    halo = lambda b, i: (jnp.maximum((b * nq + i) * (tq // CONV_HALO) - 1, 0), 0)
    rows = MLA_HEADS * tq
    n_out = C_WIDTH + MLA_HEADS * V_DIM
    return pl.pallas_call(
        functools.partial(_odd_prompt_kernel, tq=tq, kb=kb),
        grid=(nb, nq),
        in_specs=[pl.BlockSpec((tq, 3 * C_WIDTH), blk), pl.BlockSpec((CONV_HALO, 3 * C_WIDTH), halo),
                  pl.BlockSpec((tq, MLA_HEADS * NOPE_DIM), blk), pl.BlockSpec((tq, MLA_HEADS * ROPE_DIM), blk),
                  pl.BlockSpec((s, KC_W), lambda b, i: (b, 0)),
                  _const_spec(wuk_ext.shape), _const_spec(wuv_ext.shape), _const_spec(conv_w.shape)],
        out_specs=[pl.BlockSpec((tq, n_out), blk), pl.BlockSpec((1, SUBLANE, C_WIDTH), lambda b, i: (b, 0, 0))],
        out_shape=[jax.ShapeDtypeStruct((nb * s, n_out), BF16), jax.ShapeDtypeStruct((nb, SUBLANE, C_WIDTH), F32)],
        scratch_shapes=[pltpu.VMEM((rows, KC_W), BF16),
                        pltpu.VMEM((rows, LANE), F32), pltpu.VMEM((rows, LANE), F32), pltpu.VMEM((rows, KV_LORA), F32),
                        pltpu.VMEM((tq + CONV_HALO, C_WIDTH), F32)],
        compiler_params=_cp(("arbitrary", "arbitrary")),
        name="mix_odd_prompt",
    )(ci, ci, qn, qp, kc, wuk_ext, wuv_ext, conv_w)


def _qx_sample_kernel(qn_ref, qp_ref, wuk_ref, qx_ref, *, rows):
    _build_qx(qn_ref, qp_ref, wuk_ref, qx_ref, rows)


def _qx_sample(qn, qp, wuk_ext, n_prompt_rows, rows):
    off = n_prompt_rows // rows
    return pl.pallas_call(
        functools.partial(_qx_sample_kernel, rows=rows),
        grid=(1,),
        in_specs=[pl.BlockSpec((rows, MLA_HEADS * NOPE_DIM), lambda i: (off, 0)),
                  pl.BlockSpec((rows, MLA_HEADS * ROPE_DIM), lambda i: (off, 0)), _const_spec(wuk_ext.shape)],
        out_specs=pl.BlockSpec((MLA_HEADS * rows, KC_W), lambda i: (0, 0)),
        out_shape=jax.ShapeDtypeStruct((MLA_HEADS * rows, KC_W), F32),
        compiler_params=_cp(("arbitrary",)),
        name="qx_sample",
    )(qn, qp, wuk_ext)


def _page_copies(pt_ref, pool_ckv, pool_kpet, ckv_buf, kpe_buf, sem, layer, n_pages, step, slot):
    copies = []
    for i in range(MLA_SAMPLE_BATCHES):
        b = step * MLA_SAMPLE_BATCHES + i
        for p in range(n_pages):
            page = pt_ref[b, p]
            keys = pl.ds(p * PAGE_SIZE, PAGE_SIZE)
            copies.append(pltpu.make_async_copy(pool_ckv.at[layer, page], ckv_buf.at[slot, i, keys],
                                                sem.at[0, slot]))
            copies.append(pltpu.make_async_copy(pool_kpet.at[layer, page], kpe_buf.at[slot, i, :, keys],
                                                sem.at[1, slot]))
    return copies


def _mla_sample_kernel(pt_ref, qx_ref, kn_ref, pool_ckv, pool_kpet, o_ref, ckv_buf, kpe_buf, q_scr, sem, *,
                       ts, n_pages, layer, chunk):
    step = pl.program_id(0)
    n_steps = pl.num_programs(0)
    slot = step % 2
    copies = functools.partial(_page_copies, pt_ref, pool_ckv, pool_kpet, ckv_buf, kpe_buf, sem, layer, n_pages)

    def start_all(cs):
        for n, c in enumerate(cs):
            c.start(priority=(n // 2) % 2)

    @pl.when(step == 0)
    def _():
        start_all(copies(0, 0))

    @pl.when(step + 1 < n_steps)
    def _():
        start_all(copies(step + 1, 1 - slot))

    for c in copies(step, slot):
        c.wait()

    rows = MLA_HEADS * ts
    n_new = kn_ref.shape[0]
    t_q = lax.broadcasted_iota(jnp.int32, (rows, n_new), 0) % ts
    col = lax.broadcasted_iota(jnp.int32, (rows, n_new), 1)
    kn = kn_ref[...].astype(BF16)
    past = n_pages * PAGE_SIZE
    starts = range(0, past, chunk)
    for i in range(MLA_SAMPLE_BATCHES):
        for h in range(MLA_HEADS):
            q_scr[i, ts * h:ts * (h + 1), :] = qx_ref[h, ts * i:ts * (i + 1), :]
        qx = q_scr[i].astype(BF16)
        q_lat = qx[:, :KV_LORA]
        q_pe = qx[:, KV_LORA:KV_LORA + ROPE_DIM]
        sn = jnp.where((col // ts == i) & (col % ts <= t_q), _dot_nt(qx, kn), NEG)
        kcs = [ckv_buf[slot, i, k0:k0 + chunk, :].astype(BF16) for k0 in starts]
        ss = [_dot_nt(q_lat, kc) + _dot(q_pe, kpe_buf[slot, i, :, k0:k0 + chunk].astype(BF16))
              for k0, kc in zip(starts, kcs)]
        m = functools.reduce(jnp.maximum, [jnp.max(s, axis=-1, keepdims=True) for s in ss],
                             jnp.max(sn, axis=-1, keepdims=True))
        pn = jnp.exp2(sn - m)
        den = jnp.sum(pn, axis=-1, keepdims=True)
        pnb = pn.astype(BF16)
        accs = [_dot(pnb, kn[:, LANE * g:LANE * (g + 1)]) for g in range(KV_LORA // LANE)]
        for s, kc in zip(ss, kcs):
            p = jnp.exp2(s - m)
            den = den + jnp.sum(p, axis=-1, keepdims=True)
            pb = p.astype(BF16)
            accs = [a + _dot(pb, kc[:, LANE * g:LANE * (g + 1)]) for g, a in enumerate(accs)]
        o = jnp.concatenate(accs, axis=1) * (1.0 / den)
        for h in range(MLA_HEADS):
            o_ref[h, ts * i:ts * (i + 1), :] = o[ts * h:ts * (h + 1)]


def _mla_sample(page_table, qx, kn, pool_ckv, pool_kpet, layer, ts):
    bd, n_pages = page_table.shape
    past = n_pages * PAGE_SIZE
    nbs = MLA_SAMPLE_BATCHES
    step_rows = nbs * ts
    assert step_rows % SUBLANE == 0
    grid_spec = pltpu.PrefetchScalarGridSpec(
        num_scalar_prefetch=1,
        grid=(bd // nbs,),
        in_specs=[pl.BlockSpec((MLA_HEADS, step_rows, KC_W), lambda g, pt: (0, g, 0)),
                  pl.BlockSpec((step_rows, KC_W), lambda g, pt: (g, 0)),
                  pl.BlockSpec(memory_space=pl.ANY), pl.BlockSpec(memory_space=pl.ANY)],
        out_specs=pl.BlockSpec((MLA_HEADS, step_rows, KV_LORA), lambda g, pt: (0, g, 0)),
        scratch_shapes=[pltpu.VMEM((2, nbs, past, KV_LORA), F32), pltpu.VMEM((2, nbs, ROPE_DIM, past), F32),
                        pltpu.VMEM((nbs, MLA_HEADS * ts, KC_W), F32), pltpu.SemaphoreType.DMA((2, 2))],
    )
    return pl.pallas_call(
        functools.partial(_mla_sample_kernel, ts=ts, n_pages=n_pages, layer=layer,
                          chunk=min(MLA_SAMPLE_CHUNK, past)),
        grid_spec=grid_spec,
        out_shape=jax.ShapeDtypeStruct((MLA_HEADS, bd * ts, KV_LORA), F32),
        compiler_params=_cp(("arbitrary",), vmem=MLA_SAMPLE_VMEM),
        name="mla_sample",
    )(page_table, qx, kn, pool_ckv, pool_kpet)


def _odd_sample_tail_kernel(ci_ref, st_ref, cw_ref, ol_ref, wuv_ref, o_ref, z_ref, *, ts):
    cw = cw_ref[...]
    z = ci_ref[:, C_WIDTH:2 * C_WIDTH].astype(F32) * ci_ref[:, 2 * C_WIDTH:].astype(F32)
    z_ref[...] = z
    rows = z.shape[0]
    t = _token_col(rows, ts)
    st = st_ref[...]
    back2 = jnp.where(t < 2, st, pltpu.roll(z, 2, 0))
    back1 = jnp.where(t == 0, pltpu.roll(st, rows - 1, 0), pltpu.roll(z, 1, 0))
    y = cw[0:1] * back2 + cw[1:2] * back1 + cw[2:3] * z
    o_ref[:, :C_WIDTH] = (ci_ref[:, :C_WIDTH].astype(F32) * y).astype(BF16)
    for m in range(MLA_HEADS // 2):
        o = (_dot(ol_ref[2 * m].astype(BF16), wuv_ref[2 * m])
             + _dot(ol_ref[2 * m + 1].astype(BF16), wuv_ref[2 * m + 1]))
        o_ref[:, C_WIDTH + LANE * m:C_WIDTH + LANE * (m + 1)] = o.astype(BF16)


def _odd_sample_tail(ci, state_rows, conv_w, o_lat, wuv_ext, n_prompt_rows, ts):
    rows = state_rows.shape[0]
    off = n_prompt_rows // rows
    n_out = C_WIDTH + MLA_HEADS * V_DIM
    return pl.pallas_call(
        functools.partial(_odd_sample_tail_kernel, ts=ts),
        grid=(1,),
        in_specs=[pl.BlockSpec((rows, 3 * C_WIDTH), lambda i: (off, 0)), _const_spec(state_rows.shape),
                  _const_spec(conv_w.shape), _const_spec(o_lat.shape), _const_spec(wuv_ext.shape)],
        out_specs=[pl.BlockSpec((rows, n_out), lambda i: (0, 0)), pl.BlockSpec((rows, C_WIDTH), lambda i: (0, 0))],
        out_shape=[jax.ShapeDtypeStruct((rows, n_out), BF16), jax.ShapeDtypeStruct((rows, C_WIDTH), F32)],
        compiler_params=_cp(("arbitrary",)),
        name="odd_sample_tail",
    )(ci, state_rows, conv_w, o_lat, wuv_ext)


def _prep_odd_weights(w_in, w_uq, w_uk, w_uv):
    d, n = w_in.shape
    n_pad = -n % LANE
    w_in_p = jnp.pad(w_in, ((0, 0), (0, n_pad))).astype(BF16)
    per = NOPE_DIM + ROPE_DIM
    wq = w_uq.reshape(Q_LORA, MLA_HEADS, per)
    w_uq_r = jnp.concatenate([wq[:, :, :NOPE_DIM].reshape(Q_LORA, -1), wq[:, :, NOPE_DIM:].reshape(Q_LORA, -1)],
                             axis=1).astype(BF16)
    ukt = jnp.transpose(w_uk, (1, 2, 0))
    uvt = jnp.transpose(w_uv, (1, 0, 2))
    wuk_ext, wuv_ext = [], []
    for h in range(MLA_HEADS):
        lo = (h % 2) * NOPE_DIM
        wuk_ext.append(jnp.pad(ukt[h], ((lo, LANE - NOPE_DIM - lo), (0, 0))))
        wuv_ext.append(jnp.pad(uvt[h], ((0, 0), (lo, LANE - V_DIM - lo))))
    return w_in_p, w_uq_r, jnp.stack(wuk_ext).astype(BF16), jnp.stack(wuv_ext).astype(BF16)


def kernel(x_prompt, x_sample, cache_swa_k, cache_swa_v, state_conv, cache_mla_ckv, cache_mla_kpe, page_table,
           rel_bias, ffn1_norm, ffn1_w_gate, ffn1_w_up, ffn1_w_down, mix_norm, ffn2_norm, ffn2_w_gate,
           ffn2_w_up, ffn2_w_down, even_w_in, even_w_out, attn_sinks, sgu_ln_g, sgu_ln_b, sgu_w, sgu_b,
           odd_w_in, odd_w_out, conv_w, mla_q_norm, mla_w_uq, mla_kv_norm, mla_w_uk, mla_w_uv, final_norm):
    nb, s, d = x_prompt.shape
    bd, ts, _ = x_sample.shape
    depth = ffn1_norm.shape[0]
    n_p, n_s = nb * s, bd * ts
    past = page_table.shape[1] * PAGE_SIZE
    wb = cache_swa_k.shape[2]
    assert n_s == TM and n_p % TM == 0 and s % TM == 0, "sample group must fill exactly one token tile"
    assert SUBLANE % ts == 0 and bd % BB_EVEN == 0 and wb == WINDOW
    assert CONV_WIDTH == 3 and ts >= CONV_WIDTH - 1 and bd % MLA_SAMPLE_BATCHES == 0
    assert s % KB_ODD == 0 and s % TQ_EVEN == 0 and KB_ODD % TQ_ODD == 0
    n_pt = n_p // TM

    x = (x_prompt.reshape(n_p, d), x_sample.reshape(n_s, d))
    bias_p, bias_s = _bias_tables(rel_bias, ts)
    cos_np, sin_np = _rope_tables_np(s, past, ts, bd)
    cos_t, sin_t = jnp.asarray(cos_np), jnp.asarray(sin_np)

    w1 = tuple(w.astype(BF16) for w in (ffn1_w_gate, ffn1_w_up, ffn1_w_down))
    w2 = tuple(w.astype(BF16) for w in (ffn2_w_gate, ffn2_w_up, ffn2_w_down))
    w_out_even, w_out_odd = even_w_out.astype(BF16), odd_w_out.astype(BF16)
    pool_kpet = jnp.swapaxes(cache_mla_kpe, 2, 3)

    outs = {k: [] for k in ("kp", "vp", "ks", "vs", "sv", "cp", "cs", "ckp", "kpp", "cks", "kps")}
    for l in range(depth):
        x = _ffn(x, ffn1_norm[l], *w1, l, n_pt)
        if l % 2 == 0:
            e = l // 2
            q, kv, ug, kv_p = _inproj_even(x, mix_norm[l], even_w_in[e].astype(BF16), n_pt, s)
            mix_p = _mix_even_prompt(q, kv, ug, bias_p, attn_sinks[e], sgu_ln_g[e], sgu_ln_b[e], sgu_w[e],
                                     sgu_b[e], nb, s)
            a_s, new_k, new_v = _mix_even_sample(q[n_p:].astype(F32), kv[n_p:],
                                                 cache_swa_k[e].reshape(bd, wb, A_KV_W),
                                                 cache_swa_v[e].reshape(bd, wb, A_KV_W), bias_s, attn_sinks[e], ts)
            sg_s, v_rows = _sgu_sample(ug, sgu_ln_g[e], sgu_ln_b[e], sgu_w[e], sgu_b[e], n_p, ts, bd)
            mix_s = jnp.concatenate([a_s.astype(BF16), sg_s], axis=1)
            w_out = (w_out_even, e)
            outs["kp"].append(kv_p[..., :A_KV_W].reshape(nb, wb, A_KV_HEADS, HEAD_DIM))
            outs["vp"].append(kv_p[..., A_KV_W:].reshape(nb, wb, A_KV_HEADS, HEAD_DIM))
            outs["ks"].append(new_k.reshape(bd, wb, A_KV_HEADS, HEAD_DIM))
            outs["vs"].append(new_v.reshape(bd, wb, A_KV_HEADS, HEAD_DIM))
            outs["sv"].append(v_rows.reshape(bd, ts, SG_WIDTH))
        else:
            j = l // 2
            w_in_p, w_uq_r, wuk_ext, wuv_ext = _prep_odd_weights(odd_w_in[j], mla_w_uq[j], mla_w_uk[j], mla_w_uv[j])
            ci, qn, qp, kc, ckv_p, kpe_p, ckv_s, kpe_s = _inproj_odd(
                x, mix_norm[l], w_in_p, mla_q_norm[j], w_uq_r, mla_kv_norm[j], cos_t, sin_t, n_pt, s)
            mix_p, cst_p = _mix_odd_prompt(ci, qn, qp, kc, wuk_ext, wuv_ext, conv_w[j], nb, s)
            qx = _qx_sample(qn, qp, wuk_ext, n_p, n_s).reshape(MLA_HEADS, n_s, KC_W)
            o_lat = _mla_sample(page_table, qx, kc[n_p:].astype(F32), cache_mla_ckv, pool_kpet, j, ts)
            n_st = CONV_WIDTH - 1
            state_rows = jnp.pad(state_conv[j], ((0, 0), (0, ts - n_st), (0, 0))).reshape(n_s, C_WIDTH)
            mix_s, z_s = _odd_sample_tail(ci, state_rows, conv_w[j], o_lat, wuv_ext, n_p, ts)
            w_out = (w_out_odd, j)
            outs["cp"].append(cst_p[:, SUBLANE - n_st:])
            outs["cs"].append(z_s.reshape(bd, ts, C_WIDTH)[:, ts - n_st:])
            outs["ckp"].append(ckv_p.reshape(nb, s, KV_LORA))
            outs["kpp"].append(kpe_p.reshape(nb, s, ROPE_DIM))
            outs["cks"].append(ckv_s.reshape(bd, ts, KV_LORA))
            outs["kps"].append(kpe_s.reshape(bd, ts, ROPE_DIM))
        x = _ffn(x, ffn2_norm[l], *w2, l, n_pt, mix=(mix_p, mix_s) + w_out,
                 final_g=final_norm if l == depth - 1 else None)
    y_prompt = x[0].reshape(nb, s, d)
    y_sample = x[1].reshape(bd, ts, d)
    st = lambda k: jnp.stack(outs[k])
    return (y_prompt, y_sample, st("kp"), st("vp"), st("ks"), st("vs"), st("sv"), st("cp"), st("cs"),
            st("ckp"), st("kpp"), st("cks"), st("kps"))
```
